```python
import jax, jax.numpy as jnp
from jax import lax
import numpy as np

D_MODEL = 1024
BATCH = 8
SEQ = 8192
DEPTH = 4

CTX_LEN = 256
GRID_W = 64

N_ADA = 9
D_FF = 2816
NORM_EPS = 1e-6
MIX_WIDTH = 512

ATTN_HEADS = 8
ATTN_KV_HEADS = 2
HEAD_DIM = 64
WINDOW = 128
ATTN_BLOCK = 128
ROPE_BASE = 10000.0
NEG_INF = -1e30

LRU_WIDTH = MIX_WIDTH
LRU_BLOCKS = 8
LRU_CONV = 4
LRU_C = 8.0

RWKV_HEADS = 8
RWKV_HEAD = 64
RWKV_WIDTH = RWKV_HEADS * RWKV_HEAD
RWKV_DECAY_RANK = 64
RWKV_A_RANK = 64
RWKV_G_RANK = 128
RWKV_LN_EPS = 64e-5

N_BRANCH = 3
ATTN_Q_COLS = ATTN_HEADS * HEAD_DIM
ATTN_KV_COLS = ATTN_KV_HEADS * HEAD_DIM
ATTN_COLS = ATTN_Q_COLS + 2 * ATTN_KV_COLS
LRU_COLS = 2 * LRU_WIDTH
RWKV_COLS = 3 * RWKV_WIDTH + 2 * RWKV_DECAY_RANK + 2 * RWKV_A_RANK + RWKV_G_RANK
GATE_COLS = N_BRANCH * D_MODEL
IN_COLS = ATTN_COLS + LRU_COLS + RWKV_COLS + GATE_COLS

kernel_name = 'hybrid_dit_gqa_rglru_rwkv7_macaron'


def _rms(x, g):
    xf = x.astype(jnp.float32)
    y = xf * lax.rsqrt(jnp.mean(xf * xf, axis=-1, keepdims=True) + NORM_EPS)
    return (y * g.astype(jnp.float32)).astype(x.dtype)


def _modulate(x, g, shift, scale):
    return _rms(x, g) * (1 + scale) + shift


def _swiglu(h, w_gu, w_d):
    gt, up = jnp.split(h @ w_gu, 2, axis=-1)
    return (jax.nn.silu(gt) * up) @ w_d


def _shift(u, off):
    if off == 0:
        return u
    T = u.shape[1]
    pad = [(0, 0)] * u.ndim
    if off > 0:
        pad[1] = (0, off)
        return jnp.pad(u, pad)[:, off:off + T]
    pad[1] = (-off, 0)
    return jnp.pad(u, pad)[:, :T]


def _depthwise_conv(u, w, b):
    K = w.shape[0]
    out = b + w[0] * _shift(u, -(K // 2))
    for j in range(1, K):
        out = out + w[j] * _shift(u, j - K // 2)
    return out


def _token_shift(u, mu):
    return u + mu[0] * (_shift(u, -1) - u) + mu[1] * (_shift(u, 1) - u)


def _axial_rope_tables(n_tokens):
    n_rows = n_tokens // GRID_W
    row = jnp.repeat(jnp.arange(n_rows), GRID_W).astype(jnp.float32)
    col = jnp.tile(jnp.arange(GRID_W), n_rows).astype(jnp.float32)
    n_freq = HEAD_DIM // 4
    inv = ROPE_BASE ** (-jnp.arange(n_freq, dtype=jnp.float32) / n_freq)
    ang = jnp.concatenate([row[:, None] * inv, col[:, None] * inv], axis=-1)
    return jnp.cos(ang), jnp.sin(ang)


def _apply_rope(x, cos, sin):
    B, T, Hh, Dh = x.shape
    nf = Dh // 4
    xr = x.reshape(B, T, Hh, 2, 2, nf)
    x1, x2 = xr[..., 0, :], xr[..., 1, :]
    cs = cos.reshape(1, T, 1, 2, nf)
    sn = sin.reshape(1, T, 1, 2, nf)
    out = jnp.stack([x1 * cs - x2 * sn, x2 * cs + x1 * sn], axis=-2)
    return out.reshape(B, T, Hh, Dh).astype(x.dtype)


def _windowed_attention(q, k, v, kc, vc, sink):
    B, T, H, Dh = q.shape
    G = k.shape[2]
    R = H // G
    C = kc.shape[1]
    n_blk = T // ATTN_BLOCK
    span = ATTN_BLOCK + 2 * WINDOW
    pad = ((0, 0), (WINDOW, WINDOW), (0, 0), (0, 0))
    kp = jnp.pad(k, pad)
    vp = jnp.pad(v, pad)
    qg = q.reshape(B, T, G, R, Dh)
    scale = Dh ** -0.5
    sink_col = jnp.broadcast_to(sink.astype(jnp.float32).reshape(1, G, R, 1, 1), (B, G, R, ATTN_BLOCK, 1))

    def one_block(i):
        start = i * ATTN_BLOCK
        qb = lax.dynamic_slice_in_dim(qg, start, ATTN_BLOCK, axis=1)
        kb = lax.dynamic_slice_in_dim(kp, start, span, axis=1)
        vb = lax.dynamic_slice_in_dim(vp, start, span, axis=1)
        q_pos = start + jnp.arange(ATTN_BLOCK)
        k_pos = start - WINDOW + jnp.arange(span)
        valid = (jnp.abs(q_pos[:, None] - k_pos[None, :]) <= WINDOW) & (k_pos[None, :] >= 0) & (k_pos[None, :] < T)
        s_loc = jnp.einsum('bqgrd,bkgd->bgrqk', qb, kb).astype(jnp.float32) * scale
        s_loc = jnp.where(valid, s_loc, NEG_INF)
        s_ctx = jnp.einsum('bqgrd,bcgd->bgrqc', qb, kc).astype(jnp.float32) * scale
        p = jax.nn.softmax(jnp.concatenate([s_loc, s_ctx, sink_col], axis=-1), axis=-1).astype(vb.dtype)
        o = jnp.einsum('bgrqk,bkgd->bqgrd', p[..., :span], vb)
        o = o + jnp.einsum('bgrqc,bcgd->bqgrd', p[..., span:span + C], vc)
        return o

    o = lax.map(one_block, jnp.arange(n_blk))
    return jnp.moveaxis(o, 0, 1).reshape(B, T, H * Dh)


def _context_attention(q, k, v, sink):
    B, C, H, Dh = q.shape
    G = k.shape[2]
    R = H // G
    qg = q.reshape(B, C, G, R, Dh)
    s = jnp.einsum('bqgrd,bkgd->bgrqk', qg, k).astype(jnp.float32) * Dh ** -0.5
    sink_col = jnp.broadcast_to(sink.astype(jnp.float32).reshape(1, G, R, 1, 1), (B, G, R, C, 1))
    p = jax.nn.softmax(jnp.concatenate([s, sink_col], axis=-1), axis=-1)[..., :C].astype(v.dtype)
    return jnp.einsum('bgrqk,bkgd->bqgrd', p, v).reshape(B, C, H * Dh)


def _split_qkv(za):
    B, T, _ = za.shape
    q, k, v = jnp.split(za, [ATTN_Q_COLS, ATTN_Q_COLS + ATTN_KV_COLS], axis=-1)
    return (q.reshape(B, T, ATTN_HEADS, HEAD_DIM),
            k.reshape(B, T, ATTN_KV_HEADS, HEAD_DIM),
            v.reshape(B, T, ATTN_KV_HEADS, HEAD_DIM))


def _attention_mixer(za, zac, cos, sin, q_gain, k_gain, sink, need_ctx):
    q, k, v = _split_qkv(za)
    q = _apply_rope(_rms(q, q_gain), cos, sin)
    k = _apply_rope(_rms(k, k_gain), cos, sin)
    qc, kc, vc = _split_qkv(zac)
    kc = _rms(kc, k_gain)
    y = _windowed_attention(q, k, v, kc, vc, sink)
    yc = _context_attention(_rms(qc, q_gain), kc, vc, sink) if need_ctx else None
    return y, yc


def _linear_combine(left, right):
    return left[0] * right[0], right[0] * left[1] + right[1]


def _rglru_direction(u, w_gate, b_gate, lam, h0, reverse):
    B, T, W = u.shape
    ub = u.reshape(B, T, LRU_BLOCKS, W // LRU_BLOCKS)
    gates = jnp.einsum('btnd,gnde->gbtne', ub, w_gate).reshape(2, B, T, W) + b_gate[:, None, None, :]
    r = jax.nn.sigmoid(gates[0].astype(jnp.float32))
    i = jax.nn.sigmoid(gates[1].astype(jnp.float32))
    log_a = -LRU_C * jax.nn.softplus(-lam.astype(jnp.float32)) * r
    a = jnp.exp(log_a)
    b = jnp.sqrt(-jnp.expm1(2.0 * log_a)) * (i * u.astype(jnp.float32))
    A, Hs = lax.associative_scan(_linear_combine, (a, b), axis=1, reverse=reverse)
    return A * h0[:, None, :] + Hs


def _lru_mixer(zl, zlc, conv_w, conv_b, gate_w, gate_b, lam, need_ctx):
    ux, ug = jnp.split(zl, 2, axis=-1)
    uxc, ugc = jnp.split(zlc, 2, axis=-1)
    xl = _depthwise_conv(ux, conv_w, conv_b)
    xc = _depthwise_conv(uxc, conv_w, conv_b)
    h0 = jnp.zeros((xc.shape[0], LRU_WIDTH), jnp.float32)
    h_lat = None
    h_ctx = None
    for d in range(2):
        rev = d == 1
        hc = _rglru_direction(xc, gate_w[d], gate_b[d], lam[d], h0, rev)
        h_fin = hc[:, 0] if rev else hc[:, -1]
        hl = _rglru_direction(xl, gate_w[d], gate_b[d], lam[d], h_fin, rev)
        h_lat = hl if h_lat is None else h_lat + hl
        if need_ctx:
            h_ctx = hc if h_ctx is None else h_ctx + hc
    y = h_lat.astype(ug.dtype) * jax.nn.gelu(ug)
    yc = h_ctx.astype(ugc.dtype) * jax.nn.gelu(ugc) if need_ctx else None
    return y, yc


def _rwkv_heads(t):
    return t.astype(jnp.float32).reshape(t.shape[0], t.shape[1], RWKV_HEADS, RWKV_HEAD)


def _rwkv_prepare(z, mu, w_up, w0, a_up, a0, g_up, k_k, k_a):
    Rw, Ra = RWKV_DECAY_RANK, RWKV_A_RANK
    cuts = [RWKV_WIDTH, 2 * RWKV_WIDTH, 3 * RWKV_WIDTH, 3 * RWKV_WIDTH + 2 * Rw, 3 * RWKV_WIDTH + 2 * Rw + 2 * Ra]
    r, k, v, wd, ad, gd = jnp.split(_token_shift(z, mu), cuts, axis=-1)
    kk = _rwkv_heads(k * k_k)
    kk = kk / jnp.maximum(jnp.linalg.norm(kk, axis=-1, keepdims=True), 1e-12)
    per_dir = []
    for d in range(2):
        logw = (w0[d] + jnp.tanh(wd[..., d * Rw:(d + 1) * Rw]) @ w_up[d]).astype(jnp.float32)
        decay = jnp.exp(-jnp.exp(-jax.nn.softplus(-logw) - 0.5))
        a = jax.nn.sigmoid(a0[d] + ad[..., d * Ra:(d + 1) * Ra] @ a_up[d])
        k_d = k * (1 + (a - 1) * k_a)
        per_dir.append((_rwkv_heads(decay), _rwkv_heads(k_d), _rwkv_heads(a)))
    gate = jax.nn.sigmoid(gd) @ g_up
    return _rwkv_heads(r), _rwkv_heads(v), kk, per_dir, gate


def _rwkv_scan(s0, r, w, k, v, kk, a, reverse):
    xs = tuple(jnp.moveaxis(t, 1, 0) for t in (r, w, k, v, kk, a))

    def step(s, inp):
        r_t, w_t, k_t, v_t, kk_t, a_t = inp
        sa = jnp.einsum('bhvk,bhk->bhv', s, -kk_t)
        s = s * w_t[:, :, None, :] + sa[..., None] * (kk_t * a_t)[:, :, None, :] + v_t[..., None] * k_t[:, :, None, :]
        return s, jnp.einsum('bhvk,bhk->bhv', s, r_t)

    s_fin, o = lax.scan(step, s0, xs, reverse=reverse)
    return jnp.moveaxis(o, 0, 1), s_fin


def _rwkv_bonus(r, k, v, r_k):
    return jnp.sum(r * k * r_k.astype(jnp.float32).reshape(RWKV_HEADS, RWKV_HEAD), axis=-1, keepdims=True) * v


def _head_groupnorm(o, g, b):
    mean = jnp.mean(o, axis=-1, keepdims=True)
    var = jnp.mean(jnp.square(o - mean), axis=-1, keepdims=True)
    y = (o - mean) * lax.rsqrt(var + RWKV_LN_EPS)
    return y.reshape(o.shape[0], o.shape[1], RWKV_WIDTH) * g + b


def _rwkv_mixer(zr, zrc, mu, w_up, w0, a_up, a0, g_up, k_k, k_a, r_k, ln_g, ln_b, need_ctx):
    r, v, kk, dirs, gate = _rwkv_prepare(zr, mu, w_up, w0, a_up, a0, g_up, k_k, k_a)
    rc, vc, kkc, dirs_c, gate_c = _rwkv_prepare(zrc, mu, w_up, w0, a_up, a0, g_up, k_k, k_a)
    s0 = jnp.zeros((rc.shape[0], RWKV_HEADS, RWKV_HEAD, RWKV_HEAD), jnp.float32)
    o = None
    oc = None
    for d in range(2):
        rev = d == 1
        w_c, k_c, a_c = dirs_c[d]
        o_c_d, s_c = _rwkv_scan(s0, rc, w_c, k_c, vc, kkc, a_c, rev)
        w_l, k_l, a_l = dirs[d]
        o_l_d, _ = _rwkv_scan(s_c, r, w_l, k_l, v, kk, a_l, rev)
        term = o_l_d + _rwkv_bonus(r, k_l, v, r_k)
        o = term if o is None else o + term
        if need_ctx:
            term_c = o_c_d + _rwkv_bonus(rc, k_c, vc, r_k)
            oc = term_c if oc is None else oc + term_c
    y = (_head_groupnorm(o, ln_g, ln_b) * gate).astype(zr.dtype)
    yc = (_head_groupnorm(oc, ln_g, ln_b) * gate_c).astype(zrc.dtype) if need_ctx else None
    return y, yc


def _gated_merge(outs, zg, proj, w_o):
    g = jax.nn.sigmoid(zg.reshape(zg.shape[0], zg.shape[1], N_BRANCH, D_MODEL))
    acc = g[..., 0, :] * (outs[0] @ proj[0])
    for b in range(1, N_BRANCH):
        acc = acc + g[..., b, :] * (outs[b] @ proj[b])
    return acc @ w_o


def _token_mixers(z, zc, cos, sin, q_gain, k_gain, sink, conv_w, conv_b, gate_w, gate_b, lam,
                  mu, w_up, w0, a_up, a0, g_up, k_k, k_a, r_k, ln_g, ln_b, proj, w_o, need_ctx):
    cuts = [ATTN_COLS, ATTN_COLS + LRU_COLS, ATTN_COLS + LRU_COLS + RWKV_COLS]
    za, zl, zr, zg = jnp.split(z, cuts, axis=-1)
    zac, zlc, zrc, zgc = jnp.split(zc, cuts, axis=-1)
    ya, yac = _attention_mixer(za, zac, cos, sin, q_gain, k_gain, sink, need_ctx)
    yl, ylc = _lru_mixer(zl, zlc, conv_w, conv_b, gate_w, gate_b, lam, need_ctx)
    yr, yrc = _rwkv_mixer(zr, zrc, mu, w_up, w0, a_up, a0, g_up, k_k, k_a, r_k, ln_g, ln_b, need_ctx)
    out = _gated_merge((ya, yl, yr), zg, proj, w_o)
    out_c = _gated_merge((yac, ylc, yrc), zgc, proj, w_o) if need_ctx else None
    return out, out_c


def setup_inputs(seed: int = 0) -> dict:
    key = jax.random.key(seed)
    ks = list(jax.random.split(key, 40))
    L, D = DEPTH, D_MODEL
    bw = LRU_WIDTH // LRU_BLOCKS

    def nrm(i, shape, scale):
        return jax.random.normal(ks[i], shape, jnp.float32) * scale

    a_init = jax.random.uniform(ks[30], (L, 2, LRU_WIDTH), jnp.float32, minval=0.9, maxval=0.999)
    return {
        'x': nrm(0, (BATCH, SEQ, D), 1.0),
        'c': nrm(1, (BATCH, D), 1.0),
        'ctx': nrm(2, (BATCH, CTX_LEN, D), 1.0),
        'c_ctx': nrm(3, (D,), 1.0),
        'ada_w': nrm(4, (L, D, N_ADA * D), 0.5 * D ** -0.5),
        'ada_b': nrm(5, (L, N_ADA * D), 0.02),
        'norm_g': 1.0 + nrm(6, (L, 3, D), 0.02),
        'ffn_w_gu': nrm(7, (L, 2, D, 2 * D_FF), D ** -0.5),
        'ffn_w_d': nrm(8, (L, 2, D_FF, D), D_FF ** -0.5),
        'w_in': nrm(9, (L, D, IN_COLS), D ** -0.5),
        'attn_q_gain': 1.0 + nrm(10, (L, HEAD_DIM), 0.02),
        'attn_k_gain': 1.0 + nrm(11, (L, HEAD_DIM), 0.02),
        'attn_sink': nrm(12, (L, ATTN_HEADS), 0.5),
        'lru_conv_w': nrm(13, (L, LRU_CONV, LRU_WIDTH), LRU_CONV ** -0.5),
        'lru_conv_b': nrm(14, (L, LRU_WIDTH), 0.02),
        'lru_gate_w': nrm(15, (L, 2, 2, LRU_BLOCKS, bw, bw), bw ** -0.5),
        'lru_gate_b': nrm(16, (L, 2, 2, LRU_WIDTH), 0.02),
        'lru_lambda': jnp.log(a_init) - jnp.log1p(-a_init),
        'rwkv_mu': jax.random.uniform(ks[17], (L, 2, RWKV_COLS), jnp.float32, minval=0.0, maxval=0.5),
        'rwkv_w_up': nrm(18, (L, 2, RWKV_DECAY_RANK, RWKV_WIDTH), 0.5 * RWKV_DECAY_RANK ** -0.5),
        'rwkv_w0': nrm(19, (L, 2, RWKV_WIDTH), 1.0) - 0.5,
        'rwkv_a_up': nrm(20, (L, 2, RWKV_A_RANK, RWKV_WIDTH), 0.5 * RWKV_A_RANK ** -0.5),
        'rwkv_a0': nrm(21, (L, 2, RWKV_WIDTH), 0.5),
        'rwkv_g_up': nrm(22, (L, RWKV_G_RANK, RWKV_WIDTH), RWKV_G_RANK ** -0.5),
        'rwkv_k_k': 0.85 + nrm(23, (L, RWKV_WIDTH), 0.02),
        'rwkv_k_a': 1.0 + nrm(24, (L, RWKV_WIDTH), 0.02),
        'rwkv_r_k': nrm(25, (L, RWKV_WIDTH), 0.1),
        'rwkv_ln_g': 1.0 + nrm(26, (L, RWKV_WIDTH), 0.02),
        'rwkv_ln_b': nrm(27, (L, RWKV_WIDTH), 0.02),
        'branch_proj': nrm(28, (L, N_BRANCH, MIX_WIDTH, D), MIX_WIDTH ** -0.5),
        'w_out': nrm(29, (L, D, D), D ** -0.5),
    }


def reference(x, c, ctx, c_ctx, ada_w, ada_b, norm_g, ffn_w_gu, ffn_w_d, w_in,
              attn_q_gain, attn_k_gain, attn_sink, lru_conv_w, lru_conv_b, lru_gate_w, lru_gate_b, lru_lambda,
              rwkv_mu, rwkv_w_up, rwkv_w0, rwkv_a_up, rwkv_a0, rwkv_g_up, rwkv_k_k, rwkv_k_a, rwkv_r_k,
              rwkv_ln_g, rwkv_ln_b, branch_proj, w_out):
    n_tok = x.shape[1]
    cos, sin = _axial_rope_tables(n_tok)
    s_lat = jax.nn.silu(c)[:, None, :]
    s_ctx = jax.nn.silu(c_ctx)[None, None, :]
    xc = ctx
    for l in range(DEPTH):
        need_ctx = l < DEPTH - 1
        m = jnp.split(s_lat @ ada_w[l] + ada_b[l], N_ADA, axis=-1)
        mc = jnp.split(s_ctx @ ada_w[l] + ada_b[l], N_ADA, axis=-1)
        x = x + 0.5 * m[2] * _swiglu(_modulate(x, norm_g[l, 0], m[0], m[1]), ffn_w_gu[l, 0], ffn_w_d[l, 0])
        xc = xc + 0.5 * mc[2] * _swiglu(_modulate(xc, norm_g[l, 0], mc[0], mc[1]), ffn_w_gu[l, 0], ffn_w_d[l, 0])
        z = _modulate(x, norm_g[l, 1], m[3], m[4]) @ w_in[l]
        zc = _modulate(xc, norm_g[l, 1], mc[3], mc[4]) @ w_in[l]
        mix, mix_c = _token_mixers(
            z, zc, cos, sin, attn_q_gain[l], attn_k_gain[l], attn_sink[l],
            lru_conv_w[l], lru_conv_b[l], lru_gate_w[l], lru_gate_b[l], lru_lambda[l],
            rwkv_mu[l], rwkv_w_up[l], rwkv_w0[l], rwkv_a_up[l], rwkv_a0[l], rwkv_g_up[l],
            rwkv_k_k[l], rwkv_k_a[l], rwkv_r_k[l], rwkv_ln_g[l], rwkv_ln_b[l],
            branch_proj[l], w_out[l], need_ctx)
        x = x + m[5] * mix
        x = x + 0.5 * m[8] * _swiglu(_modulate(x, norm_g[l, 2], m[6], m[7]), ffn_w_gu[l, 1], ffn_w_d[l, 1])
        if need_ctx:
            xc = xc + mc[5] * mix_c
            xc = xc + 0.5 * mc[8] * _swiglu(_modulate(xc, norm_g[l, 2], mc[6], mc[7]), ffn_w_gu[l, 1], ffn_w_d[l, 1])
    return x
```

```python
import functools

import jax
import jax.numpy as jnp
from jax import lax
from jax.experimental import pallas as pl
from jax.experimental.pallas import tpu as pltpu

F32 = jnp.float32
BF16 = jnp.bfloat16
HIGHEST = lax.Precision.HIGHEST

N_ADA = 9
NORM_EPS = 1e-6
GRID_W = 64
HEAD_DIM = 64
ATTN_HEADS = 8
ATTN_KV_HEADS = 2
ATTN_BLOCK = 128
ROPE_BASE = 10000.0
NEG_INF = -1e30
LRU_C = 8.0
LRU_BLOCKS = 8
RWKV_HEADS = 8
RWKV_LN_EPS = 64e-5
MIX = 512
LANES = 128
TOKEN_BLOCK = 256
LRU_CHUNK = 128
RWKV_CHUNK = 64
VMEM_LIMIT = 56 * 1024 * 1024


def _params(*sem):
    return pltpu.CompilerParams(dimension_semantics=sem, vmem_limit_bytes=VMEM_LIMIT)


def _resident(shape):
    nd = len(shape)
    return pl.BlockSpec(shape, lambda *_: (0,) * nd, pipeline_mode=pl.Buffered(1))


def _dot(a, b):
    return jnp.dot(a, b, preferred_element_type=F32)


def _dot_f32(a, b):
    return jnp.dot(a, b, preferred_element_type=F32, precision=HIGHEST)


def _dot_nt(a, b, precision=None):
    return lax.dot_general(a, b, (((1,), (1,)), ((), ())), preferred_element_type=F32, precision=precision)


def _dot_tn(a, b, precision=None):
    return lax.dot_general(a, b, (((0,), (0,)), ((), ())), preferred_element_type=F32, precision=precision)


def _sigmoid(x):
    return 1.0 / (1.0 + jnp.exp(-x))


def _softplus(x):
    return jnp.maximum(x, 0.0) + jnp.log(1.0 + jnp.exp(-jnp.abs(x)))


def _modulated_norm(x, g, shift, scale):
    y = x * lax.rsqrt(jnp.mean(x * x, axis=-1, keepdims=True) + NORM_EPS)
    return (y * g) * (1.0 + scale) + shift


def _ada_kernel(s_ref, w_ref, b_ref, o_ref):
    s = s_ref[...]
    s = (s * _sigmoid(s)).astype(BF16)
    o_ref[0] = _dot(s, w_ref[0].astype(BF16)) + b_ref[0]


def _ada_vectors(c_rows, ada_w, ada_b):
    L, D, N = ada_w.shape
    R = c_rows.shape[0]
    tn = 1024
    return pl.pallas_call(
        _ada_kernel,
        grid=(L, N // tn),
        in_specs=[pl.BlockSpec((R, D), lambda l, j: (0, 0)),
                  pl.BlockSpec((1, D, tn), lambda l, j: (l, 0, j)),
                  pl.BlockSpec((1, 1, tn), lambda l, j: (l, 0, j))],
        out_specs=pl.BlockSpec((1, R, tn), lambda l, j: (l, 0, j)),
        out_shape=jax.ShapeDtypeStruct((L, R, N), F32),
        compiler_params=_params("parallel", "parallel"),
        name="ada_vectors",
    )(c_rows, ada_w, ada_b.reshape(L, 1, N))


def _ffn_kernel(x_ref, mod_ref, g_ref, wg_ref, wu_ref, wd_ref, o_ref):
    x = x_ref[0]
    mod = mod_ref[0, 0]
    h = _modulated_norm(x, g_ref[...], mod[0:1], mod[1:2]).astype(BF16)
    gt = _dot(h, wg_ref[...])
    up = _dot(h, wu_ref[...])
    act = (gt * _sigmoid(gt) * up).astype(BF16)
    y = _dot(act, wd_ref[...])
    o_ref[0] = x + 0.5 * mod[2:3] * y


def _ffn(x, mod3, g, wg, wu, wd, n_ctx_blocks):
    B, TT, D = x.shape
    F = wg.shape[1]
    tm = TOKEN_BLOCK
    kind = lambda i: jnp.where(i < n_ctx_blocks, 0, 1)
    return pl.pallas_call(
        _ffn_kernel,
        grid=(B, TT // tm),
        in_specs=[pl.BlockSpec((1, tm, D), lambda b, i: (b, i, 0)),
                  pl.BlockSpec((1, 1, 3, D), lambda b, i: (b, kind(i), 0, 0)),
                  _resident((1, D)), _resident((D, F)), _resident((D, F)), _resident((F, D))],
        out_specs=pl.BlockSpec((1, tm, D), lambda b, i: (b, i, 0)),
        out_shape=jax.ShapeDtypeStruct((B, TT, D), F32),
        compiler_params=_params("parallel", "parallel"),
        name="ffn",
    )(x, mod3, g, wg, wu, wd)


def _head_rms_rope(xs, gain, cos, sin, lane):
    sq = xs * xs
    lo = lane < HEAD_DIM
    s_lo = jnp.sum(jnp.where(lo, sq, 0.0), axis=-1, keepdims=True)
    s_hi = jnp.sum(jnp.where(lo, 0.0, sq), axis=-1, keepdims=True)
    ms = jnp.where(lo, s_lo, s_hi) * (1.0 / HEAD_DIM)
    y = xs * lax.rsqrt(ms + NORM_EPS) * gain
    first = (lane % 32) < 16
    partner = jnp.where(first, pltpu.roll(y, LANES - 16, axis=1), pltpu.roll(y, 16, axis=1))
    return y * cos + partner * sin


def _inproj_kernel(x_ref, mod_ref, g_ref, w_ref, qg_ref, kg_ref, cos_ref, sin_ref,
                   q_ref, k_ref, v_ref, ux_ref, ug_ref, zr_ref, zg_ref):
    x = x_ref[0]
    mod = mod_ref[0, 0]
    h = _modulated_norm(x, g_ref[...], mod[0:1], mod[1:2]).astype(BF16)
    nq = ATTN_HEADS * HEAD_DIM
    nkv = ATTN_KV_HEADS * HEAD_DIM
    lane = lax.broadcasted_iota(jnp.int32, (1, LANES), 1)
    cos = cos_ref[...]
    sin = sin_ref[...]
    za = _dot(h, w_ref[:, 0:nq + 2 * nkv])
    for s in range(nq // LANES):
        qs = _head_rms_rope(za[:, s * LANES:(s + 1) * LANES], qg_ref[...], cos, sin, lane)
        q_ref[0, :, s * LANES:(s + 1) * LANES] = (qs * HEAD_DIM ** -0.5).astype(BF16)
    k_ref[0] = _head_rms_rope(za[:, nq:nq + nkv], kg_ref[...], cos, sin, lane).astype(BF16)
    v_ref[0] = za[:, nq + nkv:nq + 2 * nkv].astype(BF16)
    c0 = nq + 2 * nkv
    zl = _dot(h, w_ref[:, c0:c0 + 2 * MIX])
    ux_ref[...] = zl[:, 0:MIX]
    ug_ref[...] = zl[:, MIX:2 * MIX]
    c1 = c0 + 2 * MIX
    nr = zr_ref.shape[2]
    zr_ref[0] = _dot(h, w_ref[:, c1:c1 + nr])
    c2 = c1 + nr
    zg_ref[0] = _dot(h, w_ref[:, c2:c2 + zg_ref.shape[2]])


def _inproj(x, mod3, g, w_in, q_gain, k_gain, cos_t, sin_t, n_ctx_blocks, n_rwkv, n_gate):
    B, TT, D = x.shape
    tm = TOKEN_BLOCK
    nq = ATTN_HEADS * HEAD_DIM
    nkv = ATTN_KV_HEADS * HEAD_DIM
    kind = lambda i: jnp.where(i < n_ctx_blocks, 0, 1)
    tok = lambda w: pl.BlockSpec((1, tm, w), lambda b, i: (b, i, 0))
    tmajor = pl.BlockSpec((tm, MIX), lambda b, i: (i, b))
    return pl.pallas_call(
        _inproj_kernel,
        grid=(B, TT // tm),
        in_specs=[tok(D),
                  pl.BlockSpec((1, 1, 3, D), lambda b, i: (b, kind(i), 0, 0)),
                  _resident((1, D)), _resident(w_in.shape), _resident((1, LANES)), _resident((1, LANES)),
                  pl.BlockSpec((tm, LANES), lambda b, i: (i, 0)),
                  pl.BlockSpec((tm, LANES), lambda b, i: (i, 0))],
        out_specs=[tok(nq), tok(nkv), tok(nkv), tmajor, tmajor, tok(n_rwkv), tok(n_gate)],
        out_shape=[jax.ShapeDtypeStruct((B, TT, nq), BF16),
                   jax.ShapeDtypeStruct((B, TT, nkv), BF16),
                   jax.ShapeDtypeStruct((B, TT, nkv), BF16),
                   jax.ShapeDtypeStruct((TT, B * MIX), F32),
                   jax.ShapeDtypeStruct((TT, B * MIX), F32),
                   jax.ShapeDtypeStruct((B, TT, n_rwkv), F32),
                   jax.ShapeDtypeStruct((B, TT, n_gate), F32)],
        compiler_params=_params("parallel", "parallel"),
        name="inproj",
    )(x, mod3, g, w_in, q_gain, k_gain, cos_t, sin_t)


def _attn_kernel(sink_ref, q_ref, kp_ref, kc_ref, kn_ref, kx_ref, vp_ref, vc_ref, vn_ref, vx_ref, o_ref,
                 *, n_ctx_blocks, n_blocks):
    i = pl.program_id(1)
    blk = ATTN_BLOCK
    n_cx = kx_ref.shape[1]
    far = 4 * blk
    off_prev = jnp.where(i >= n_ctx_blocks + 1, 0, far)
    off_cur = jnp.where(i >= n_ctx_blocks, 0, far)
    off_next = jnp.where((i >= n_ctx_blocks) & (i + 1 <= n_blocks - 1), 0, far)
    qi = lax.broadcasted_iota(jnp.int32, (blk, 3 * blk + n_cx), 0)
    col = lax.broadcasted_iota(jnp.int32, (blk, 3 * blk + n_cx), 1)
    ki = col & (blk - 1)
    mask = (((col < blk) & (ki >= qi + off_prev))
            | ((col >= blk) & (col < 2 * blk) & (ki >= off_cur))
            | ((col >= 2 * blk) & (col < 3 * blk) & (ki + off_next <= qi))
            | (col >= 3 * blk))
    kcat = jnp.concatenate([kp_ref[0], kc_ref[0], kn_ref[0], kx_ref[0]], axis=0)
    vcat = jnp.concatenate([vp_ref[0], vc_ref[0], vn_ref[0], vx_ref[0]], axis=0)
    rep = ATTN_HEADS // ATTN_KV_HEADS
    for h in range(ATTN_HEADS):
        g = h // rep
        qh = q_ref[0, :, h * HEAD_DIM:(h + 1) * HEAD_DIM]
        kg = kcat[:, g * HEAD_DIM:(g + 1) * HEAD_DIM]
        vg = vcat[:, g * HEAD_DIM:(g + 1) * HEAD_DIM]
        s = jnp.where(mask, _dot_nt(qh, kg), NEG_INF)
        sink = sink_ref[h]
        m = jnp.maximum(jnp.max(s, axis=-1, keepdims=True), sink)
        p = jnp.exp(s - m)
        denom = jnp.sum(p, axis=-1, keepdims=True) + jnp.exp(sink - m)
        o = _dot(p.astype(BF16), vg) / denom
        o_ref[0, :, h * HEAD_DIM:(h + 1) * HEAD_DIM] = o.astype(BF16)


def _attention(q, k, v, sink, n_ctx):
    B, TT, nq = q.shape
    nkv = k.shape[2]
    blk = ATTN_BLOCK
    nb = TT // blk
    ncb = n_ctx // blk
    kv = lambda f: pl.BlockSpec((1, blk, nkv), lambda b, i: (b, f(i), 0))
    prev = kv(lambda i: jnp.maximum(i - 1, 0))
    cur = kv(lambda i: i)
    nxt = kv(lambda i: jnp.minimum(i + 1, nb - 1))
    cx = pl.BlockSpec((1, n_ctx, nkv), lambda b, i: (b, 0, 0))
    return pl.pallas_call(
        functools.partial(_attn_kernel, n_ctx_blocks=ncb, n_blocks=nb),
        grid=(B, nb),
        in_specs=[pl.BlockSpec(memory_space=pltpu.SMEM),
                  pl.BlockSpec((1, blk, nq), lambda b, i: (b, i, 0)),
                  prev, cur, nxt, cx, prev, cur, nxt, cx],
        out_specs=pl.BlockSpec((1, blk, nq), lambda b, i: (b, i, 0)),
        out_shape=jax.ShapeDtypeStruct((B, TT, nq), BF16),
        compiler_params=_params("parallel", "parallel"),
        name="attention",
    )(sink, q, k, k, k, k, v, v, v, v)


def _lru_chunk_index(s, n_ctx_chunks, n_chunks, reverse):
    if not reverse:
        return s
    return jnp.where(s < n_ctx_chunks, n_ctx_chunks - 1 - s, n_chunks - 1 - (s - n_ctx_chunks))


def _lru_kernel(up_ref, u_ref, un_ref, cw_ref, cb_ref, gw_ref, gb_ref, lam_ref, *rest,
                n_ctx_chunks, n_chunks, reverse):
    if reverse:
        hf_ref, ug_ref, y_ref, a_scr, b_scr, h_scr = rest
    else:
        h_out_ref, a_scr, b_scr, h_scr = rest
    s = pl.program_id(0)
    c = _lru_chunk_index(s, n_ctx_chunks, n_chunks, reverse)
    tc, nb, w = u_ref.shape

    @pl.when(s == 0)
    def _():
        h_scr[...] = jnp.zeros_like(h_scr)

    has_prev = (c != 0) & (c != n_ctx_chunks)
    has_next = (c != n_ctx_chunks - 1) & (c != n_chunks - 1)
    u = u_ref[...]
    ext = jnp.concatenate([jnp.where(has_prev, up_ref[...], 0.0), u, jnp.where(has_next, un_ref[...], 0.0)], axis=0)
    cw = cw_ref[...]
    xc = cb_ref[...] + cw[0:1] * ext[0:tc] + cw[1:2] * ext[1:tc + 1] + cw[2:3] * ext[2:tc + 2] + cw[3:4] * ext[3:tc + 3]

    gates = _dot(xc.reshape(tc * nb, w).astype(BF16), gw_ref[...]).reshape(tc, nb, 2 * w) + gb_ref[...]
    r = _sigmoid(gates[:, :, 0:w])
    ig = _sigmoid(gates[:, :, w:2 * w])
    log_a = (-LRU_C * _softplus(-lam_ref[...])) * r
    a2 = jnp.exp(2.0 * log_a)
    a_scr[...] = jnp.exp(log_a)
    b_scr[...] = jnp.sqrt(-jnp.tanh(log_a) * (a2 + 1.0)) * (ig * xc)

    def step(t, h):
        tt = tc - 1 - t if reverse else t
        h = a_scr[tt] * h + b_scr[tt]
        b_scr[tt] = h
        return h

    h_scr[...] = lax.fori_loop(0, tc, step, h_scr[...], unroll=8)

    if reverse:
        ug = ug_ref[...]
        gelu = 0.5 * ug * (1.0 + jnp.tanh(0.7978845608028654 * (ug + 0.044715 * ug * ug * ug)))
        y_ref[...] = ((hf_ref[...] + b_scr[...]) * gelu).astype(BF16)
    else:
        h_out_ref[...] = b_scr[...]


def _lru_direction(ux3, conv_w, conv_b, gate_w, gate_b, lam, n_ctx, reverse, h_fwd=None, ug3=None):
    TT, B, W = ux3.shape
    tc = LRU_CHUNK
    nch = TT // tc
    ncc = n_ctx // tc
    cidx = lambda s: _lru_chunk_index(s, ncc, nch, reverse)
    chunk = pl.BlockSpec((tc, B, W), lambda s: (cidx(s), 0, 0))
    in_specs = [pl.BlockSpec((2, B, W), lambda s: (jnp.maximum(cidx(s) * (tc // 2) - 1, 0), 0, 0)),
                chunk,
                pl.BlockSpec((1, B, W), lambda s: (jnp.minimum((cidx(s) + 1) * tc, TT - 1), 0, 0)),
                _resident((4, W)), _resident((1, W)), _resident((W, 2 * W)), _resident((1, 2 * W)),
                _resident((1, W))]
    args = [ux3, ux3, ux3, conv_w, conv_b, gate_w, gate_b, lam]
    if reverse:
        in_specs += [chunk, chunk]
        args += [h_fwd, ug3]
        out_dtype = BF16
    else:
        out_dtype = F32
    return pl.pallas_call(
        functools.partial(_lru_kernel, n_ctx_chunks=ncc, n_chunks=nch, reverse=reverse),
        grid=(nch,),
        in_specs=in_specs,
        out_specs=chunk,
        out_shape=jax.ShapeDtypeStruct((TT, B, W), out_dtype),
        scratch_shapes=[pltpu.VMEM((tc, B, W), F32), pltpu.VMEM((tc, B, W), F32), pltpu.VMEM((B, W), F32)],
        compiler_params=_params("arbitrary"),
        name="lru_rev" if reverse else "lru_fwd",
    )(*args)


def _rwkv_prep_kernel(zp_ref, z_ref, zn_ref, mu_ref, kk_ref, ka_ref, wup_ref, w0_ref, aup_ref, a0_ref, gup_ref,
                      hsum_ref, r_ref, v_ref, kn_ref, gate_ref, lw_ref, kd_ref, a_ref,
                      *, n_ctx_blocks, n_blocks):
    i = pl.program_id(1)
    has_prev = (i != 0) & (i != n_ctx_blocks)
    has_next = (i != n_ctx_blocks - 1) & (i != n_blocks - 1)
    z = z_ref[0]
    tm = z.shape[0]
    row = lax.broadcasted_iota(jnp.int32, (tm, 1), 0)
    z_first = jnp.where(has_prev, zp_ref[0, 7:8, :], 0.0)
    z_last = jnp.where(has_next, zn_ref[0, 0:1, :], 0.0)
    z_prev = jnp.where(row == 0, z_first, pltpu.roll(z, 1, axis=0))
    z_next = jnp.where(row == tm - 1, z_last, pltpu.roll(z, tm - 1, axis=0))
    mu = mu_ref[...]
    zs = z + mu[0:1] * (z_prev - z) + mu[1:2] * (z_next - z)
    w = MIX
    r = zs[:, 0:w]
    k = zs[:, w:2 * w]
    v = zs[:, 2 * w:3 * w]
    wd = jnp.tanh(zs[:, 3 * w:3 * w + LANES]).astype(BF16)
    ad = zs[:, 3 * w + LANES:3 * w + 2 * LANES].astype(BF16)
    gd = _sigmoid(zs[:, 3 * w + 2 * LANES:3 * w + 3 * LANES]).astype(BF16)
    r_ref[0] = r
    v_ref[0] = v
    kk = k * kk_ref[...]
    norm = jnp.sqrt(_dot_f32(kk * kk, hsum_ref[...]))
    kn_ref[0] = kk / jnp.maximum(norm, 1e-12)
    gate_ref[0] = _dot(gd, gup_ref[...])
    for d in range(2):
        logw = w0_ref[d:d + 1] + _dot(wd, wup_ref[d])
        lw_ref[d, 0] = -jnp.exp(-_softplus(-logw) - 0.5)
        a = _sigmoid(a0_ref[d:d + 1] + _dot(ad, aup_ref[d]))
        a_ref[d, 0] = a
        kd_ref[d, 0] = k * (1.0 + (a - 1.0) * ka_ref[...])


def _rwkv_prepare(zr, mu, k_k, k_a, w_up, w0, a_up, a0, g_up, hsum, n_ctx):
    B, TT, NR = zr.shape
    tm = TOKEN_BLOCK
    nb = TT // tm
    ncb = n_ctx // tm
    rows = tm // 8
    tok = pl.BlockSpec((1, tm, MIX), lambda b, i: (b, i, 0))
    tok2 = pl.BlockSpec((2, 1, tm, MIX), lambda b, i: (0, b, i, 0))
    one = jax.ShapeDtypeStruct((B, TT, MIX), F32)
    two = jax.ShapeDtypeStruct((2, B, TT, MIX), F32)
    return pl.pallas_call(
        functools.partial(_rwkv_prep_kernel, n_ctx_blocks=ncb, n_blocks=nb),
        grid=(B, nb),
        in_specs=[pl.BlockSpec((1, 8, NR), lambda b, i: (b, jnp.maximum(i * rows - 1, 0), 0)),
                  pl.BlockSpec((1, tm, NR), lambda b, i: (b, i, 0)),
                  pl.BlockSpec((1, 8, NR), lambda b, i: (b, jnp.minimum((i + 1) * rows, TT // 8 - 1), 0)),
                  _resident((2, NR)), _resident((1, MIX)), _resident((1, MIX)),
                  _resident((2, LANES, MIX)), _resident((2, MIX)), _resident((2, LANES, MIX)), _resident((2, MIX)),
                  _resident((LANES, MIX)), _resident((MIX, MIX))],
        out_specs=[tok, tok, tok, tok, tok2, tok2, tok2],
        out_shape=[one, one, one, one, two, two, two],
        compiler_params=_params("parallel", "parallel"),
        name="rwkv_prepare",
    )(zr, zr, zr, mu, k_k, k_a, w_up, w0, a_up, a0, g_up, hsum)


def _rwkv_scan_kernel(r_ref, v_ref, kn_ref, lw_ref, kd_ref, a_ref, rk_ref, o_ref, s_scr):
    d = pl.program_id(0)
    s = pl.program_id(2)
    C = r_ref.shape[1]
    N = HEAD_DIM

    @pl.when(s == 0)
    def _():
        s_scr[...] = jnp.zeros_like(s_scr)

    sgn = 1 - 2 * d
    ri = lax.broadcasted_iota(jnp.int32, (C, C), 0)
    ci = lax.broadcasted_iota(jnp.int32, (C, C), 1)
    tri = ((ri - ci) * sgn >= 0).astype(F32)
    ri2 = lax.broadcasted_iota(jnp.int32, (2 * C, 2 * C), 0)
    ci2 = lax.broadcasted_iota(jnp.int32, (2 * C, 2 * C), 1)
    rel2 = ((ri2 & (C - 1)) - (ci2 & (C - 1))) * sgn
    mask4 = rel2 > jnp.where(ri2 < C, 0, -1)

    for h in range(RWKV_HEADS):
        sl = slice(h * N, (h + 1) * N)
        r = r_ref[0, :, sl]
        v = v_ref[0, :, sl]
        kn = kn_ref[0, :, sl]
        lw = lw_ref[0, 0, :, sl]
        k = kd_ref[0, 0, :, sl]
        a = a_ref[0, 0, :, sl]
        beta = kn * a
        l_in = _dot_f32(tri, lw)
        l_tot = jnp.sum(lw, axis=0, keepdims=True)
        p_in = jnp.exp(l_in)
        p_inv = jnp.exp(-l_in)
        p_end = jnp.exp(l_tot - l_in)
        a_bar = -kn * jnp.exp(l_in - lw)
        r_bar = r * p_in
        lhs = jnp.concatenate([a_bar, r_bar], axis=0)
        rhs = jnp.concatenate([beta * p_inv, k * p_inv], axis=0)
        m = jnp.where(mask4, _dot_nt(lhs, rhs, HIGHEST), 0.0)
        m_ab, m_ak = m[0:C, 0:C], m[0:C, C:2 * C]
        m_rb, m_rk = m[C:2 * C, 0:C], m[C:2 * C, C:2 * C]
        st = s_scr[h]
        from_state = _dot_nt(lhs, st, HIGHEST)
        u = from_state[0:C] + _dot_f32(m_ak, v)
        pw = m_ab
        u = u + _dot_f32(pw, u)
        span = 2
        while span < C:
            pw = _dot_f32(pw, pw)
            u = u + _dot_f32(pw, u)
            span *= 2
        bonus = jnp.sum(r * k * rk_ref[:, sl], axis=-1, keepdims=True) * v
        o_ref[0, 0, :, sl] = from_state[C:2 * C] + _dot_f32(m_rk, v) + _dot_f32(m_rb, u) + bonus
        s_scr[h] = st * jnp.exp(l_tot) + _dot_tn(v, k * p_end, HIGHEST) + _dot_tn(u, beta * p_end, HIGHEST)


def _rwkv_scan(r, v, kn, lw, kd, a, r_k, n_ctx):
    B, TT, W = r.shape
    C = RWKV_CHUNK
    nch = TT // C
    ncc = n_ctx // C

    def cidx(d, s):
        rev = jnp.where(s < ncc, ncc - 1 - s, nch - 1 - (s - ncc))
        return jnp.where(d == 0, s, rev)

    one = pl.BlockSpec((1, C, W), lambda d, b, s: (b, cidx(d, s), 0))
    two = pl.BlockSpec((1, 1, C, W), lambda d, b, s: (d, b, cidx(d, s), 0))
    return pl.pallas_call(
        _rwkv_scan_kernel,
        grid=(2, B, nch),
        in_specs=[one, one, one, two, two, two, _resident((1, W))],
        out_specs=two,
        out_shape=jax.ShapeDtypeStruct((2, B, TT, W), F32),
        scratch_shapes=[pltpu.VMEM((RWKV_HEADS, HEAD_DIM, HEAD_DIM), F32)],
        compiler_params=_params("parallel", "parallel", "arbitrary"),
        name="rwkv_scan",
    )(r, v, kn, lw, kd, a, r_k)


def _merge_kernel(x_ref, mod_ref, ya_ref, yl_ref, o_ref, gate_ref, lng_ref, lnb_ref, havg_ref, zg_ref,
                  proj_ref, wo_ref, out_ref):
    D = x_ref.shape[2]
    o = o_ref[0, 0] + o_ref[1, 0]
    mean = _dot_f32(o, havg_ref[...])
    cen = o - mean
    var = _dot_f32(cen * cen, havg_ref[...])
    yr = ((cen * lax.rsqrt(var + RWKV_LN_EPS)) * lng_ref[...] + lnb_ref[...]) * gate_ref[0]
    ys = (ya_ref[0], yl_ref[...], yr.astype(BF16))
    acc = None
    for b in range(3):
        term = _sigmoid(zg_ref[0, :, b * D:(b + 1) * D]) * _dot(ys[b], proj_ref[b])
        acc = term if acc is None else acc + term
    out_ref[0] = x_ref[0] + mod_ref[0, 0][2:3] * _dot(acc.astype(BF16), wo_ref[...])


def _merge(x, mod3, ya, yl, o_rwkv, gate, ln_g, ln_b, havg, zg, proj, w_o, n_ctx_blocks):
    B, TT, D = x.shape
    tm = TOKEN_BLOCK
    kind = lambda i: jnp.where(i < n_ctx_blocks, 0, 1)
    tok = lambda w: pl.BlockSpec((1, tm, w), lambda b, i: (b, i, 0))
    return pl.pallas_call(
        _merge_kernel,
        grid=(B, TT // tm),
        in_specs=[tok(D),
                  pl.BlockSpec((1, 1, 3, D), lambda b, i: (b, kind(i), 0, 0)),
                  tok(MIX),
                  pl.BlockSpec((tm, MIX), lambda b, i: (i, b)),
                  pl.BlockSpec((2, 1, tm, MIX), lambda b, i: (0, b, i, 0)),
                  tok(MIX),
                  _resident((1, MIX)), _resident((1, MIX)), _resident((MIX, MIX)),
                  tok(3 * D),
                  _resident((3, MIX, D)), _resident((D, D))],
        out_specs=tok(D),
        out_shape=jax.ShapeDtypeStruct((B, TT, D), F32),
        compiler_params=_params("parallel", "parallel"),
        name="merge",
    )(x, mod3, ya, yl, o_rwkv, gate, ln_g, ln_b, havg, zg, proj, w_o)


def _rope_tables(n_ctx, n_tok):
    n_freq = HEAD_DIM // 4
    t = jnp.arange(n_tok)
    row = (t // GRID_W).astype(F32)
    col = (t % GRID_W).astype(F32)
    inv = ROPE_BASE ** (-jnp.arange(n_freq, dtype=F32) / n_freq)
    ang_r = row[:, None] * inv
    ang_c = col[:, None] * inv
    cos = jnp.concatenate([jnp.cos(ang_r)] * 2 + [jnp.cos(ang_c)] * 2, axis=-1)
    sin = jnp.concatenate([-jnp.sin(ang_r), jnp.sin(ang_r), -jnp.sin(ang_c), jnp.sin(ang_c)], axis=-1)
    cos = jnp.concatenate([jnp.ones((n_ctx, HEAD_DIM), F32), cos], axis=0)
    sin = jnp.concatenate([jnp.zeros((n_ctx, HEAD_DIM), F32), sin], axis=0)
    return jnp.tile(cos, (1, LANES // HEAD_DIM)), jnp.tile(sin, (1, LANES // HEAD_DIM))


def _block_diag(w):
    n, r, c = w.shape
    eye = jnp.eye(n, dtype=w.dtype)
    return (eye[:, None, :, None] * w[:, :, None, :]).reshape(n * r, n * c)


def _pad_rows(w, rank):
    z = jnp.zeros_like(w[0])
    return jnp.stack([jnp.concatenate([w[0], z], axis=0), jnp.concatenate([z, w[1]], axis=0)])


def kernel(x, c, ctx, c_ctx, ada_w, ada_b, norm_g, ffn_w_gu, ffn_w_d, w_in, attn_q_gain, attn_k_gain, attn_sink, lru_conv_w, lru_conv_b, lru_gate_w, lru_gate_b, lru_lambda, rwkv_mu, rwkv_w_up, rwkv_w0, rwkv_a_up, rwkv_a0, rwkv_g_up, rwkv_k_k, rwkv_k_a, rwkv_r_k, rwkv_ln_g, rwkv_ln_b, branch_proj, w_out):
    B, T, D = x.shape
    n_ctx = ctx.shape[1]
    L = ada_w.shape[0]
    TT = n_ctx + T
    assert n_ctx % TOKEN_BLOCK == 0 and T % TOKEN_BLOCK == 0 and T % GRID_W == 0
    ncb = n_ctx // TOKEN_BLOCK
    d_ff = ffn_w_d.shape[2]
    n_rwkv = rwkv_mu.shape[2]
    n_gate = 3 * D

    n_rows = -(-(B + 1) // 8) * 8
    c_rows = jnp.zeros((n_rows, D), F32).at[:B].set(c).at[B].set(c_ctx)
    ada = _ada_vectors(c_rows, ada_w, ada_b)
    mod_lat = ada[:, :B].reshape(L, B, 1, N_ADA, D)
    mod_ctx = jnp.broadcast_to(ada[:, B].reshape(L, 1, 1, N_ADA, D), (L, B, 1, N_ADA, D))
    mods = jnp.concatenate([mod_ctx, mod_lat], axis=2)

    cos_t, sin_t = _rope_tables(n_ctx, T)
    hsum = jnp.kron(jnp.eye(RWKV_HEADS, dtype=F32), jnp.ones((HEAD_DIM, HEAD_DIM), F32))
    havg = hsum / HEAD_DIM
    tile2 = lambda g: jnp.tile(g, LANES // HEAD_DIM).reshape(1, LANES)

    xs = jnp.concatenate([ctx, x], axis=1)
    for l in range(L):
        wg = ffn_w_gu[l, :, :, :d_ff].astype(BF16)
        wu = ffn_w_gu[l, :, :, d_ff:].astype(BF16)
        wd = ffn_w_d[l].astype(BF16)
        xs = _ffn(xs, mods[l, :, :, 0:3], norm_g[l, 0].reshape(1, D), wg[0], wu[0], wd[0], ncb)

        q, k, v, ux, ug, zr, zg = _inproj(
            xs, mods[l, :, :, 3:6], norm_g[l, 1].reshape(1, D), w_in[l].astype(BF16),
            tile2(attn_q_gain[l]), tile2(attn_k_gain[l]), cos_t, sin_t, ncb, n_rwkv, n_gate)

        ya = _attention(q, k, v, attn_sink[l], n_ctx)

        ux3 = ux.reshape(TT, B, MIX)
        ug3 = ug.reshape(TT, B, MIX)
        lru_args = lambda d: (
            lru_conv_w[l], lru_conv_b[l].reshape(1, MIX),
            jnp.concatenate([_block_diag(lru_gate_w[l, d, 0]), _block_diag(lru_gate_w[l, d, 1])], axis=1).astype(BF16),
            lru_gate_b[l, d].reshape(1, 2 * MIX), lru_lambda[l, d].reshape(1, MIX))
        h_fwd = _lru_direction(ux3, *lru_args(0), n_ctx, False)
        yl = _lru_direction(ux3, *lru_args(1), n_ctx, True, h_fwd, ug3).reshape(TT, B * MIX)

        r, vv, kn, gate, lw, kd, aa = _rwkv_prepare(
            zr, rwkv_mu[l], rwkv_k_k[l].reshape(1, MIX), rwkv_k_a[l].reshape(1, MIX),
            _pad_rows(rwkv_w_up[l], rwkv_w_up.shape[2]).astype(BF16), rwkv_w0[l],
            _pad_rows(rwkv_a_up[l], rwkv_a_up.shape[2]).astype(BF16), rwkv_a0[l],
            rwkv_g_up[l].astype(BF16), hsum, n_ctx)
        o_rwkv = _rwkv_scan(r, vv, kn, lw, kd, aa, rwkv_r_k[l].reshape(1, MIX), n_ctx)

        xs = _merge(xs, mods[l, :, :, 3:6], ya, yl, o_rwkv, gate,
                    rwkv_ln_g[l].reshape(1, MIX), rwkv_ln_b[l].reshape(1, MIX), havg, zg,
                    branch_proj[l].astype(BF16), w_out[l].astype(BF16), ncb)

        xs = _ffn(xs, mods[l, :, :, 6:9], norm_g[l, 2].reshape(1, D), wg[1], wu[1], wd[1], ncb)
    return xs[:, n_ctx:]
```

```python
import functools

import jax
import jax.numpy as jnp
from jax import lax
from jax.experimental import pallas as pl
from jax.experimental.pallas import tpu as pltpu

F32 = jnp.float32
BF16 = jnp.bfloat16
HIGHEST = lax.Precision.HIGHEST

N_ADA = 9
NORM_EPS = 1e-6
GRID_W = 64
HEAD_DIM = 64
ATTN_HEADS = 8
ATTN_KV_HEADS = 2
ATTN_BLOCK = 128
ROPE_BASE = 10000.0
NEG_INF = -1e30
LRU_C = 8.0
LRU_BLOCKS = 8
RWKV_HEADS = 8
RWKV_LN_EPS = 64e-5
MIX = 512
LANES = 128
TOKEN_BLOCK = 256
LRU_CHUNK = 128
RWKV_CHUNK = 64
VMEM_LIMIT = 56 * 1024 * 1024


def _params(*sem):
    return pltpu.CompilerParams(dimension_semantics=sem, vmem_limit_bytes=VMEM_LIMIT)


def _resident(shape):
    nd = len(shape)
    return pl.BlockSpec(shape, lambda *_: (0,) * nd, pipeline_mode=pl.Buffered(1))


def _dot(a, b):
    return jnp.dot(a, b, preferred_element_type=F32)


def _dot_f32(a, b):
    return jnp.dot(a, b, preferred_element_type=F32, precision=HIGHEST)


def _dot_nt(a, b, precision=None):
    return lax.dot_general(a, b, (((1,), (1,)), ((), ())), preferred_element_type=F32, precision=precision)


def _dot_tn(a, b, precision=None):
    return lax.dot_general(a, b, (((0,), (0,)), ((), ())), preferred_element_type=F32, precision=precision)


def _sigmoid(x):
    return 1.0 / (1.0 + jnp.exp(-x))


def _softplus(x):
    return jnp.maximum(x, 0.0) + jnp.log(1.0 + jnp.exp(-jnp.abs(x)))


def _modulated_norm(x, g, shift, scale):
    y = x * lax.rsqrt(jnp.mean(x * x, axis=-1, keepdims=True) + NORM_EPS)
    return (y * g) * (1.0 + scale) + shift


def _ada_kernel(s_ref, w_ref, b_ref, o_ref):
    s = s_ref[...]
    s = (s * _sigmoid(s)).astype(BF16)
    o_ref[0] = _dot(s, w_ref[0].astype(BF16)) + b_ref[0]


def _ada_vectors(c_rows, ada_w, ada_b):
    L, D, N = ada_w.shape
    R = c_rows.shape[0]
    tn = 1024
    return pl.pallas_call(
        _ada_kernel,
        grid=(L, N // tn),
        in_specs=[pl.BlockSpec((R, D), lambda l, j: (0, 0)),
                  pl.BlockSpec((1, D, tn), lambda l, j: (l, 0, j)),
                  pl.BlockSpec((1, 1, tn), lambda l, j: (l, 0, j))],
        out_specs=pl.BlockSpec((1, R, tn), lambda l, j: (l, 0, j)),
        out_shape=jax.ShapeDtypeStruct((L, R, N), F32),
        compiler_params=_params("parallel", "parallel"),
        name="ada_vectors",
    )(c_rows, ada_w, ada_b.reshape(L, 1, N))


def _ffn_kernel(x_ref, mod_ref, g_ref, wg_ref, wu_ref, wd_ref, o_ref):
    x = x_ref[0]
    mod = mod_ref[0, 0]
    h = _modulated_norm(x, g_ref[...], mod[0:1], mod[1:2]).astype(BF16)
    gt = _dot(h, wg_ref[...])
    up = _dot(h, wu_ref[...])
    act = (gt * _sigmoid(gt) * up).astype(BF16)
    y = _dot(act, wd_ref[...])
    o_ref[0] = x + 0.5 * mod[2:3] * y


def _ffn(x, mod3, g, wg, wu, wd, n_ctx_blocks):
    B, TT, D = x.shape
    F = wg.shape[1]
    tm = TOKEN_BLOCK
    kind = lambda i: jnp.where(i < n_ctx_blocks, 0, 1)
    return pl.pallas_call(
        _ffn_kernel,
        grid=(B, TT // tm),
        in_specs=[pl.BlockSpec((1, tm, D), lambda b, i: (b, i, 0)),
                  pl.BlockSpec((1, 1, 3, D), lambda b, i: (b, kind(i), 0, 0)),
                  _resident((1, D)), _resident((D, F)), _resident((D, F)), _resident((F, D))],
        out_specs=pl.BlockSpec((1, tm, D), lambda b, i: (b, i, 0)),
        out_shape=jax.ShapeDtypeStruct((B, TT, D), F32),
        compiler_params=_params("parallel", "parallel"),
        name="ffn",
    )(x, mod3, g, wg, wu, wd)


def _head_rms_rope(xs, gain, cos, sin, lane):
    sq = xs * xs
    lo = lane < HEAD_DIM
    s_lo = jnp.sum(jnp.where(lo, sq, 0.0), axis=-1, keepdims=True)
    s_hi = jnp.sum(jnp.where(lo, 0.0, sq), axis=-1, keepdims=True)
    ms = jnp.where(lo, s_lo, s_hi) * (1.0 / HEAD_DIM)
    y = xs * lax.rsqrt(ms + NORM_EPS) * gain
    first = (lane % 32) < 16
    partner = jnp.where(first, pltpu.roll(y, LANES - 16, axis=1), pltpu.roll(y, 16, axis=1))
    return y * cos + partner * sin


def _inproj_kernel(x_ref, mod_ref, g_ref, w_ref, qg_ref, kg_ref, cos_ref, sin_ref,
                   q_ref, k_ref, v_ref, ux_ref, ug_ref, zr_ref, zg_ref):
    x = x_ref[0]
    mod = mod_ref[0, 0]
    h = _modulated_norm(x, g_ref[...], mod[0:1], mod[1:2]).astype(BF16)
    nq = ATTN_HEADS * HEAD_DIM
    nkv = ATTN_KV_HEADS * HEAD_DIM
    lane = lax.broadcasted_iota(jnp.int32, (1, LANES), 1)
    cos = cos_ref[...]
    sin = sin_ref[...]
    za = _dot(h, w_ref[:, 0:nq + 2 * nkv])
    for s in range(nq // LANES):
        qs = _head_rms_rope(za[:, s * LANES:(s + 1) * LANES], qg_ref[...], cos, sin, lane)
        q_ref[0, :, s * LANES:(s + 1) * LANES] = (qs * HEAD_DIM ** -0.5).astype(BF16)
    k_ref[0] = _head_rms_rope(za[:, nq:nq + nkv], kg_ref[...], cos, sin, lane).astype(BF16)
    v_ref[0] = za[:, nq + nkv:nq + 2 * nkv].astype(BF16)
    c0 = nq + 2 * nkv
    zl = _dot(h, w_ref[:, c0:c0 + 2 * MIX])
    ux_ref[...] = zl[:, 0:MIX]
    ug_ref[...] = zl[:, MIX:2 * MIX]
    c1 = c0 + 2 * MIX
    nr = zr_ref.shape[2]
    zr_ref[0] = _dot(h, w_ref[:, c1:c1 + nr])
    c2 = c1 + nr
    zg_ref[0] = _dot(h, w_ref[:, c2:c2 + zg_ref.shape[2]])


def _inproj(x, mod3, g, w_in, q_gain, k_gain, cos_t, sin_t, n_ctx_blocks, n_rwkv, n_gate):
    B, TT, D = x.shape
    tm = TOKEN_BLOCK
    nq = ATTN_HEADS * HEAD_DIM
    nkv = ATTN_KV_HEADS * HEAD_DIM
    kind = lambda i: jnp.where(i < n_ctx_blocks, 0, 1)
    tok = lambda w: pl.BlockSpec((1, tm, w), lambda b, i: (b, i, 0))
    tmajor = pl.BlockSpec((tm, MIX), lambda b, i: (i, b))
    return pl.pallas_call(
        _inproj_kernel,
        grid=(B, TT // tm),
        in_specs=[tok(D),
                  pl.BlockSpec((1, 1, 3, D), lambda b, i: (b, kind(i), 0, 0)),
                  _resident((1, D)), _resident(w_in.shape), _resident((1, LANES)), _resident((1, LANES)),
                  pl.BlockSpec((tm, LANES), lambda b, i: (i, 0)),
                  pl.BlockSpec((tm, LANES), lambda b, i: (i, 0))],
        out_specs=[tok(nq), tok(nkv), tok(nkv), tmajor, tmajor, tok(n_rwkv), tok(n_gate)],
        out_shape=[jax.ShapeDtypeStruct((B, TT, nq), BF16),
                   jax.ShapeDtypeStruct((B, TT, nkv), BF16),
                   jax.ShapeDtypeStruct((B, TT, nkv), BF16),
                   jax.ShapeDtypeStruct((TT, B * MIX), F32),
                   jax.ShapeDtypeStruct((TT, B * MIX), F32),
                   jax.ShapeDtypeStruct((B, TT, n_rwkv), F32),
                   jax.ShapeDtypeStruct((B, TT, n_gate), F32)],
        compiler_params=_params("parallel", "parallel"),
        name="inproj",
    )(x, mod3, g, w_in, q_gain, k_gain, cos_t, sin_t)


def _attn_kernel(sink_ref, q_ref, kp_ref, kc_ref, kn_ref, kx_ref, vp_ref, vc_ref, vn_ref, vx_ref, o_ref,
                 *, n_ctx_blocks, n_blocks):
    i = pl.program_id(1)
    blk = ATTN_BLOCK
    n_cx = kx_ref.shape[1]
    far = 4 * blk
    off_prev = jnp.where(i >= n_ctx_blocks + 1, 0, far)
    off_cur = jnp.where(i >= n_ctx_blocks, 0, far)
    off_next = jnp.where((i >= n_ctx_blocks) & (i + 1 <= n_blocks - 1), 0, far)
    qi = lax.broadcasted_iota(jnp.int32, (blk, 3 * blk + n_cx), 0)
    col = lax.broadcasted_iota(jnp.int32, (blk, 3 * blk + n_cx), 1)
    ki = col & (blk - 1)
    mask = (((col < blk) & (ki >= qi + off_prev))
            | ((col >= blk) & (col < 2 * blk) & (ki >= off_cur))
            | ((col >= 2 * blk) & (col < 3 * blk) & (ki + off_next <= qi))
            | (col >= 3 * blk))
    kcat = jnp.concatenate([kp_ref[0], kc_ref[0], kn_ref[0], kx_ref[0]], axis=0)
    vcat = jnp.concatenate([vp_ref[0], vc_ref[0], vn_ref[0], vx_ref[0]], axis=0)
    rep = ATTN_HEADS // ATTN_KV_HEADS
    for h in range(ATTN_HEADS):
        g = h // rep
        qh = q_ref[0, :, h * HEAD_DIM:(h + 1) * HEAD_DIM]
        kg = kcat[:, g * HEAD_DIM:(g + 1) * HEAD_DIM]
        vg = vcat[:, g * HEAD_DIM:(g + 1) * HEAD_DIM]
        s = jnp.where(mask, _dot_nt(qh, kg), NEG_INF)
        sink = sink_ref[h]
        m = jnp.maximum(jnp.max(s, axis=-1, keepdims=True), sink)
        p = jnp.exp(s - m)
        denom = jnp.sum(p, axis=-1, keepdims=True) + jnp.exp(sink - m)
        o = _dot(p.astype(BF16), vg) / denom
        o_ref[0, :, h * HEAD_DIM:(h + 1) * HEAD_DIM] = o.astype(BF16)


def _attention(q, k, v, sink, n_ctx):
    B, TT, nq = q.shape
    nkv = k.shape[2]
    blk = ATTN_BLOCK
    nb = TT // blk
    ncb = n_ctx // blk
    kv = lambda f: pl.BlockSpec((1, blk, nkv), lambda b, i: (b, f(i), 0))
    prev = kv(lambda i: jnp.maximum(i - 1, 0))
    cur = kv(lambda i: i)
    nxt = kv(lambda i: jnp.minimum(i + 1, nb - 1))
    cx = pl.BlockSpec((1, n_ctx, nkv), lambda b, i: (b, 0, 0))
    return pl.pallas_call(
        functools.partial(_attn_kernel, n_ctx_blocks=ncb, n_blocks=nb),
        grid=(B, nb),
        in_specs=[pl.BlockSpec(memory_space=pltpu.SMEM),
                  pl.BlockSpec((1, blk, nq), lambda b, i: (b, i, 0)),
                  prev, cur, nxt, cx, prev, cur, nxt, cx],
        out_specs=pl.BlockSpec((1, blk, nq), lambda b, i: (b, i, 0)),
        out_shape=jax.ShapeDtypeStruct((B, TT, nq), BF16),
        compiler_params=_params("parallel", "parallel"),
        name="attention",
    )(sink, q, k, k, k, k, v, v, v, v)


def _lru_chunk_index(s, n_ctx_chunks, n_chunks, reverse):
    if not reverse:
        return s
    return jnp.where(s < n_ctx_chunks, n_ctx_chunks - 1 - s, n_chunks - 1 - (s - n_ctx_chunks))


def _lru_kernel(up_ref, u_ref, un_ref, cw_ref, cb_ref, gw_ref, gb_ref, lam_ref, *rest,
                n_ctx_chunks, n_chunks, reverse):
    if reverse:
        hf_ref, ug_ref, y_ref, a_scr, b_scr, h_scr = rest
    else:
        h_out_ref, a_scr, b_scr, h_scr = rest
    s = pl.program_id(0)
    c = _lru_chunk_index(s, n_ctx_chunks, n_chunks, reverse)
    tc, nb, w = u_ref.shape

    @pl.when(s == 0)
    def _():
        h_scr[...] = jnp.zeros_like(h_scr)

    has_prev = (c != 0) & (c != n_ctx_chunks)
    has_next = (c != n_ctx_chunks - 1) & (c != n_chunks - 1)
    u = u_ref[...]
    ext = jnp.concatenate([jnp.where(has_prev, up_ref[...], 0.0), u, jnp.where(has_next, un_ref[...], 0.0)], axis=0)
    cw = cw_ref[...]
    xc = cb_ref[...] + cw[0:1] * ext[0:tc] + cw[1:2] * ext[1:tc + 1] + cw[2:3] * ext[2:tc + 2] + cw[3:4] * ext[3:tc + 3]

    gates = _dot(xc.reshape(tc * nb, w).astype(BF16), gw_ref[...]).reshape(tc, nb, 2 * w) + gb_ref[...]
    r = _sigmoid(gates[:, :, 0:w])
    ig = _sigmoid(gates[:, :, w:2 * w])
    log_a = (-LRU_C * _softplus(-lam_ref[...])) * r
    a2 = jnp.exp(2.0 * log_a)
    a_scr[...] = jnp.exp(log_a)
    b_scr[...] = jnp.sqrt(-jnp.tanh(log_a) * (a2 + 1.0)) * (ig * xc)

    def step(t, h):
        tt = tc - 1 - t if reverse else t
        h = a_scr[tt] * h + b_scr[tt]
        b_scr[tt] = h
        return h

    h_scr[...] = lax.fori_loop(0, tc, step, h_scr[...], unroll=8)

    if reverse:
        ug = ug_ref[...]
        gelu = 0.5 * ug * (1.0 + jnp.tanh(0.7978845608028654 * (ug + 0.044715 * ug * ug * ug)))
        y_ref[...] = ((hf_ref[...] + b_scr[...]) * gelu).astype(BF16)
    else:
        h_out_ref[...] = b_scr[...]


def _lru_direction(ux3, conv_w, conv_b, gate_w, gate_b, lam, n_ctx, reverse, h_fwd=None, ug3=None):
    TT, B, W = ux3.shape
    tc = LRU_CHUNK
    nch = TT // tc
    ncc = n_ctx // tc
    cidx = lambda s: _lru_chunk_index(s, ncc, nch, reverse)
    chunk = pl.BlockSpec((tc, B, W), lambda s: (cidx(s), 0, 0))
    in_specs = [pl.BlockSpec((2, B, W), lambda s: (jnp.maximum(cidx(s) * (tc // 2) - 1, 0), 0, 0)),
                chunk,
                pl.BlockSpec((1, B, W), lambda s: (jnp.minimum((cidx(s) + 1) * tc, TT - 1), 0, 0)),
                _resident((4, W)), _resident((1, W)), _resident((W, 2 * W)), _resident((1, 2 * W)),
                _resident((1, W))]
    args = [ux3, ux3, ux3, conv_w, conv_b, gate_w, gate_b, lam]
    if reverse:
        in_specs += [chunk, chunk]
        args += [h_fwd, ug3]
        out_dtype = BF16
    else:
        out_dtype = F32
    return pl.pallas_call(
        functools.partial(_lru_kernel, n_ctx_chunks=ncc, n_chunks=nch, reverse=reverse),
        grid=(nch,),
        in_specs=in_specs,
        out_specs=chunk,
        out_shape=jax.ShapeDtypeStruct((TT, B, W), out_dtype),
        scratch_shapes=[pltpu.VMEM((tc, B, W), F32), pltpu.VMEM((tc, B, W), F32), pltpu.VMEM((B, W), F32)],
        compiler_params=_params("arbitrary"),
        name="lru_rev" if reverse else "lru_fwd",
    )(*args)


def _rwkv_prep_kernel(zp_ref, z_ref, zn_ref, mu_ref, kk_ref, ka_ref, wup_ref, w0_ref, aup_ref, a0_ref, gup_ref,
                      hsum_ref, r_ref, v_ref, kn_ref, gate_ref, lw_ref, kd_ref, a_ref,
                      *, n_ctx_blocks, n_blocks):
    i = pl.program_id(1)
    has_prev = (i != 0) & (i != n_ctx_blocks)
    has_next = (i != n_ctx_blocks - 1) & (i != n_blocks - 1)
    z = z_ref[0]
    tm = z.shape[0]
    row = lax.broadcasted_iota(jnp.int32, (tm, 1), 0)
    z_first = jnp.where(has_prev, zp_ref[0, 7:8, :], 0.0)
    z_last = jnp.where(has_next, zn_ref[0, 0:1, :], 0.0)
    z_prev = jnp.where(row == 0, z_first, pltpu.roll(z, 1, axis=0))
    z_next = jnp.where(row == tm - 1, z_last, pltpu.roll(z, tm - 1, axis=0))
    mu = mu_ref[...]
    zs = z + mu[0:1] * (z_prev - z) + mu[1:2] * (z_next - z)
    w = MIX
    r = zs[:, 0:w]
    k = zs[:, w:2 * w]
    v = zs[:, 2 * w:3 * w]
    wd = jnp.tanh(zs[:, 3 * w:3 * w + LANES]).astype(BF16)
    ad = zs[:, 3 * w + LANES:3 * w + 2 * LANES].astype(BF16)
    gd = _sigmoid(zs[:, 3 * w + 2 * LANES:3 * w + 3 * LANES]).astype(BF16)
    r_ref[0] = r
    v_ref[0] = v
    kk = k * kk_ref[...]
    norm = jnp.sqrt(_dot_f32(kk * kk, hsum_ref[...]))
    kn_ref[0] = kk / jnp.maximum(norm, 1e-12)
    gate_ref[0] = _dot(gd, gup_ref[...])
    for d in range(2):
        logw = w0_ref[d:d + 1] + _dot(wd, wup_ref[d])
        lw_ref[d, 0] = -jnp.exp(-_softplus(-logw) - 0.5)
        a = _sigmoid(a0_ref[d:d + 1] + _dot(ad, aup_ref[d]))
        a_ref[d, 0] = a
        kd_ref[d, 0] = k * (1.0 + (a - 1.0) * ka_ref[...])


def _rwkv_prepare(zr, mu, k_k, k_a, w_up, w0, a_up, a0, g_up, hsum, n_ctx):
    B, TT, NR = zr.shape
    tm = TOKEN_BLOCK
    nb = TT // tm
    ncb = n_ctx // tm
    rows = tm // 8
    tok = pl.BlockSpec((1, tm, MIX), lambda b, i: (b, i, 0))
    tok2 = pl.BlockSpec((2, 1, tm, MIX), lambda b, i: (0, b, i, 0))
    one = jax.ShapeDtypeStruct((B, TT, MIX), F32)
    two = jax.ShapeDtypeStruct((2, B, TT, MIX), F32)
    return pl.pallas_call(
        functools.partial(_rwkv_prep_kernel, n_ctx_blocks=ncb, n_blocks=nb),
        grid=(B, nb),
        in_specs=[pl.BlockSpec((1, 8, NR), lambda b, i: (b, jnp.maximum(i * rows - 1, 0), 0)),
                  pl.BlockSpec((1, tm, NR), lambda b, i: (b, i, 0)),
                  pl.BlockSpec((1, 8, NR), lambda b, i: (b, jnp.minimum((i + 1) * rows, TT // 8 - 1), 0)),
                  _resident((2, NR)), _resident((1, MIX)), _resident((1, MIX)),
                  _resident((2, LANES, MIX)), _resident((2, MIX)), _resident((2, LANES, MIX)), _resident((2, MIX)),
                  _resident((LANES, MIX)), _resident((MIX, MIX))],
        out_specs=[tok, tok, tok, tok, tok2, tok2, tok2],
        out_shape=[one, one, one, one, two, two, two],
        compiler_params=_params("parallel", "parallel"),
        name="rwkv_prepare",
    )(zr, zr, zr, mu, k_k, k_a, w_up, w0, a_up, a0, g_up, hsum)


GROUP = 4 * HEAD_DIM


def _expand(y, lanehead):
    return jnp.concatenate([jnp.where(lanehead == h, y, jnp.zeros_like(y)) for h in range(4)], axis=0)


def _compact(z, lanehead):
    n = HEAD_DIM
    out = jnp.where(lanehead == 0, z[0:n], 0.0)
    for h in range(1, 4):
        out = out + jnp.where(lanehead == h, z[h * n:(h + 1) * n], 0.0)
    return out


def _rwkv_chunk_kernel(r_ref, v_ref, kn_ref, lw_ref, kd_ref, a_ref, rk_ref, q_out, o_out, g_out, h_out, pc_out):
    d = pl.program_id(0)
    C = RWKV_CHUNK
    tm = r_ref.shape[1]
    sgn = 1 - 2 * d
    ri = lax.broadcasted_iota(jnp.int32, (C, C), 0)
    ci = lax.broadcasted_iota(jnp.int32, (C, C), 1)
    tri = ((ri - ci) * sgn >= 0).astype(F32)
    t = lax.broadcasted_iota(jnp.int32, (C, GROUP), 0)
    lane = lax.broadcasted_iota(jnp.int32, (C, GROUP), 1)
    i = lane & (C - 1)
    eye = t == i
    same16 = (t >> 4) == (i >> 4)
    same32 = (t >> 5) == (i >> 5)
    lanehead = lax.broadcasted_iota(jnp.int32, (1, GROUP), 1) >> 6
    t2 = lax.broadcasted_iota(jnp.int32, (2 * C, GROUP), 0)
    i2 = lax.broadcasted_iota(jnp.int32, (2 * C, GROUP), 1) & (C - 1)
    mask2 = ((t2 & (C - 1)) - i2) * sgn > jnp.where(t2 < C, 0, -1)
    rowhead = lax.broadcasted_iota(jnp.int32, (GROUP, GROUP), 0) >> 6
    colhead = lax.broadcasted_iota(jnp.int32, (GROUP, GROUP), 1) >> 6
    head_ones = (rowhead == colhead).astype(BF16)

    def mm(x, y):
        return _dot(x.astype(BF16), _expand(y.astype(BF16), lanehead))

    def each(f, *lists):
        return [f(*xs) for xs in zip(*lists)]

    n_ch = tm // C
    units = [(slice(ch * C, (ch + 1) * C), slice(g * GROUP, (g + 1) * GROUP), ch)
             for ch in range(n_ch) for g in range(RWKV_HEADS // 4)]
    l_in_all, l_tot_all = [], []
    for ch in range(n_ch):
        lw_c = lw_ref[0, 0, ch * C:(ch + 1) * C, :]
        l_in_all.append(_dot_f32(tri, lw_c))
        l_tot_all.append(jnp.sum(lw_c, axis=0, keepdims=True))
        pc_out[0, 0, ch * 8:(ch + 1) * 8, :] = jnp.broadcast_to(jnp.exp(l_tot_all[ch]), (8, lw_c.shape[1]))

    a_bar, r_bar, lhs, b_inv, k_inv, b_end, k_end, vs = [], [], [], [], [], [], [], []
    for rows, ls, ch in units:
        kn = kn_ref[0, rows, ls]
        k = kd_ref[0, 0, rows, ls]
        l_in = l_in_all[ch][:, ls]
        beta = kn * a_ref[0, 0, rows, ls]
        p_inv = jnp.exp(-l_in)
        p_end = jnp.exp(l_tot_all[ch][:, ls] - l_in)
        a_bar.append(-kn * jnp.exp(l_in - lw_ref[0, 0, rows, ls]))
        r_bar.append(r_ref[0, rows, ls] * jnp.exp(l_in))
        lhs.append(jnp.concatenate([a_bar[-1], r_bar[-1]], axis=0).astype(BF16))
        b_inv.append((beta * p_inv).astype(BF16))
        k_inv.append((k * p_inv).astype(BF16))
        b_end.append((beta * p_end).astype(BF16))
        k_end.append((k * p_end).astype(BF16))
        vs.append(v_ref[0, rows, ls])

    mb = each(lambda l, y: jnp.where(mask2, _dot_nt(l, _expand(y, lanehead)), 0.0), lhs, b_inv)
    mk = each(lambda l, y: jnp.where(mask2, _dot_nt(l, _expand(y, lanehead)), 0.0), lhs, k_inv)
    m_ab = [m[0:C] for m in mb]
    m_rb = [m[C:2 * C] for m in mb]
    m_ak = [m[0:C] for m in mk]
    m_rk = [m[C:2 * C] for m in mk]
    d1 = [jnp.where(same16, m, 0.0) for m in m_ab]
    in32 = [jnp.where(same32, m, 0.0) for m in m_ab]
    d2 = each(mm, d1, d1)
    d4 = each(mm, d2, d2)
    tinv = [jnp.where(eye, 1.0, m) for m in d1]
    tinv = each(lambda t, p: t + mm(t, p), tinv, d2)
    d8 = each(mm, d4, d4)
    tinv = each(lambda t, p: t + mm(t, p), tinv, d4)
    x1 = each(mm, m_ak, vs)
    tinv = each(lambda t, p: t + mm(t, p), tinv, d8)
    te = each(lambda t, a, b: mm(t, a - b), tinv, in32, d1)
    tinv = each(lambda t, e: t + mm(e, t), tinv, te)
    te = each(lambda t, a, b: mm(t, a - b), tinv, m_ab, in32)
    tinv = each(lambda t, e: t + mm(e, t), tinv, te)
    w_hat = each(mm, tinv, a_bar)
    u_hat = each(mm, tinv, x1)
    o_loc = each(mm, m_rk, vs)
    for (rows, ls, ch), rb, w, rbar in zip(units, m_rb, w_hat, r_bar):
        q_out[0, 0, rows, ls] = (rbar + mm(rb, w)).astype(BF16)
    for (rows, ls, ch), rb, u, o, v in zip(units, m_rb, u_hat, o_loc, vs):
        x = r_ref[0, rows, ls] * kd_ref[0, 0, rows, ls] * rk_ref[:, ls]
        x_hi = x.astype(BF16)
        x_lo = (x - x_hi.astype(F32)).astype(BF16)
        bonus = (_dot(x_hi, head_ones) + _dot(x_lo, head_ones)) * v
        o_out[0, 0, rows, ls] = o + mm(rb, u) + bonus
    for (rows, ls, ch), w, be in zip(units, w_hat, b_end):
        g_out[0, 0, rows, ls] = _compact(_dot_tn(w.astype(BF16), be), lanehead).astype(BF16)
    for (rows, ls, ch), v, u, ke, be in zip(units, vs, u_hat, k_end, b_end):
        zh = _dot_tn(jnp.concatenate([v, u], axis=0).astype(BF16), jnp.concatenate([ke, be], axis=0))
        h_out[0, 0, rows, ls] = _compact(zh, lanehead)


def _rwkv_chunks(r, v, kn, lw, kd, a, r_k):
    B, TT, W = r.shape
    tm = TOKEN_BLOCK
    C = RWKV_CHUNK
    assert C == HEAD_DIM and tm % C == 0
    one = pl.BlockSpec((1, tm, W), lambda d, b, i: (b, i, 0))
    two = pl.BlockSpec((1, 1, tm, W), lambda d, b, i: (d, b, i, 0))
    pc = pl.BlockSpec((1, 1, 8 * (tm // C), W), lambda d, b, i: (d, b, i, 0))
    shape = lambda dt: jax.ShapeDtypeStruct((2, B, TT, W), dt)
    return pl.pallas_call(
        _rwkv_chunk_kernel,
        grid=(2, B, TT // tm),
        in_specs=[one, one, one, two, two, two, _resident((1, W))],
        out_specs=[two, two, two, two, pc],
        out_shape=[shape(BF16), shape(F32), shape(BF16), shape(F32),
                   jax.ShapeDtypeStruct((2, B, 8 * (TT // C), W), F32)],
        compiler_params=_params("parallel", "parallel", "parallel"),
        name="rwkv_chunks",
    )(r, v, kn, lw, kd, a, r_k)


def _rwkv_state_kernel(qf_ref, of_ref, gf_ref, hf_ref, pf_ref, qr_ref, or_ref, gr_ref, hr_ref, pr_ref,
                       outf_ref, outr_ref, s_scr):
    s = pl.program_id(0)

    @pl.when(s == 0)
    def _():
        s_scr[...] = jnp.zeros_like(s_scr)

    lanehead = lax.broadcasted_iota(jnp.int32, (1, GROUP), 1) >> 6
    nb = outf_ref.shape[0]
    dirs = ((qf_ref, of_ref, gf_ref, hf_ref, pf_ref, outf_ref), (qr_ref, or_ref, gr_ref, hr_ref, pr_ref, outr_ref))
    for d, (q_ref, o_ref, g_ref, h_ref, p_ref, out_ref) in enumerate(dirs):
        for b in range(nb):
            for g in range(RWKV_HEADS // 4):
                ls = slice(g * GROUP, (g + 1) * GROUP)
                st = s_scr[d, b, g]
                sb = st.astype(BF16)
                out_ref[b, :, ls] = _dot_nt(q_ref[0, b, :, ls], _expand(sb, lanehead)) + o_ref[0, b, :, ls]
                s_scr[d, b, g] = (st * p_ref[0, b, 0:1, ls] + _dot(sb, _expand(g_ref[0, b, :, ls], lanehead))
                                  + h_ref[0, b, :, ls])


def _rwkv_state(q, o0, gm, hm, pc, n_ctx):
    _, B, TT, W = q.shape
    C = RWKV_CHUNK
    nch = TT // C
    ncc = n_ctx // C
    fwd = lambda s: s
    rev = lambda s: jnp.where(s < ncc, ncc - 1 - s, nch - 1 - (s - ncc))
    spec = lambda d, f, rows: pl.BlockSpec((1, B, rows, W), lambda s: (d, 0, f(s), 0))
    out = lambda f: pl.BlockSpec((B, C, W), lambda s: (0, f(s), 0))
    shape = jax.ShapeDtypeStruct((B, TT, W), F32)
    return pl.pallas_call(
        _rwkv_state_kernel,
        grid=(nch,),
        in_specs=[spec(0, fwd, C)] * 4 + [spec(0, fwd, 8)] + [spec(1, rev, C)] * 4 + [spec(1, rev, 8)],
        out_specs=[out(fwd), out(rev)],
        out_shape=[shape, shape],
        scratch_shapes=[pltpu.VMEM((2, B, RWKV_HEADS // 4, HEAD_DIM, GROUP), F32)],
        compiler_params=_params("arbitrary"),
        name="rwkv_state",
    )(q, o0, gm, hm, pc, q, o0, gm, hm, pc)


def _merge_kernel(x_ref, mod_ref, ya_ref, yl_ref, of_ref, or_ref, gate_ref, lng_ref, lnb_ref, havg_ref, zg_ref,
                  proj_ref, wo_ref, out_ref):
    D = x_ref.shape[2]
    o = of_ref[0] + or_ref[0]
    mean = _dot_f32(o, havg_ref[...])
    cen = o - mean
    var = _dot_f32(cen * cen, havg_ref[...])
    yr = ((cen * lax.rsqrt(var + RWKV_LN_EPS)) * lng_ref[...] + lnb_ref[...]) * gate_ref[0]
    ys = (ya_ref[0], yl_ref[...], yr.astype(BF16))
    acc = None
    for b in range(3):
        term = _sigmoid(zg_ref[0, :, b * D:(b + 1) * D]) * _dot(ys[b], proj_ref[b])
        acc = term if acc is None else acc + term
    out_ref[0] = x_ref[0] + mod_ref[0, 0][2:3] * _dot(acc.astype(BF16), wo_ref[...])


def _merge(x, mod3, ya, yl, o_fwd, o_rev, gate, ln_g, ln_b, havg, zg, proj, w_o, n_ctx_blocks):
    B, TT, D = x.shape
    tm = TOKEN_BLOCK
    kind = lambda i: jnp.where(i < n_ctx_blocks, 0, 1)
    tok = lambda w: pl.BlockSpec((1, tm, w), lambda b, i: (b, i, 0))
    return pl.pallas_call(
        _merge_kernel,
        grid=(B, TT // tm),
        in_specs=[tok(D),
                  pl.BlockSpec((1, 1, 3, D), lambda b, i: (b, kind(i), 0, 0)),
                  tok(MIX),
                  pl.BlockSpec((tm, MIX), lambda b, i: (i, b)),
                  tok(MIX), tok(MIX), tok(MIX),
                  _resident((1, MIX)), _resident((1, MIX)), _resident((MIX, MIX)),
                  tok(3 * D),
                  _resident((3, MIX, D)), _resident((D, D))],
        out_specs=tok(D),
        out_shape=jax.ShapeDtypeStruct((B, TT, D), F32),
        compiler_params=_params("parallel", "parallel"),
        name="merge",
    )(x, mod3, ya, yl, o_fwd, o_rev, gate, ln_g, ln_b, havg, zg, proj, w_o)


def _rope_tables(n_ctx, n_tok):
    n_freq = HEAD_DIM // 4
    t = jnp.arange(n_tok)
    row = (t // GRID_W).astype(F32)
    col = (t % GRID_W).astype(F32)
    inv = ROPE_BASE ** (-jnp.arange(n_freq, dtype=F32) / n_freq)
    ang_r = row[:, None] * inv
    ang_c = col[:, None] * inv
    cos = jnp.concatenate([jnp.cos(ang_r)] * 2 + [jnp.cos(ang_c)] * 2, axis=-1)
    sin = jnp.concatenate([-jnp.sin(ang_r), jnp.sin(ang_r), -jnp.sin(ang_c), jnp.sin(ang_c)], axis=-1)
    cos = jnp.concatenate([jnp.ones((n_ctx, HEAD_DIM), F32), cos], axis=0)
    sin = jnp.concatenate([jnp.zeros((n_ctx, HEAD_DIM), F32), sin], axis=0)
    return jnp.tile(cos, (1, LANES // HEAD_DIM)), jnp.tile(sin, (1, LANES // HEAD_DIM))


def _block_diag(w):
    n, r, c = w.shape
    eye = jnp.eye(n, dtype=w.dtype)
    return (eye[:, None, :, None] * w[:, :, None, :]).reshape(n * r, n * c)


def _pad_rows(w, rank):
    z = jnp.zeros_like(w[0])
    return jnp.stack([jnp.concatenate([w[0], z], axis=0), jnp.concatenate([z, w[1]], axis=0)])


def kernel(x, c, ctx, c_ctx, ada_w, ada_b, norm_g, ffn_w_gu, ffn_w_d, w_in, attn_q_gain, attn_k_gain, attn_sink, lru_conv_w, lru_conv_b, lru_gate_w, lru_gate_b, lru_lambda, rwkv_mu, rwkv_w_up, rwkv_w0, rwkv_a_up, rwkv_a0, rwkv_g_up, rwkv_k_k, rwkv_k_a, rwkv_r_k, rwkv_ln_g, rwkv_ln_b, branch_proj, w_out):
    B, T, D = x.shape
    n_ctx = ctx.shape[1]
    L = ada_w.shape[0]
    TT = n_ctx + T
    assert n_ctx % TOKEN_BLOCK == 0 and T % TOKEN_BLOCK == 0 and T % GRID_W == 0
    ncb = n_ctx // TOKEN_BLOCK
    d_ff = ffn_w_d.shape[2]
    n_rwkv = rwkv_mu.shape[2]
    n_gate = 3 * D

    n_rows = -(-(B + 1) // 8) * 8
    c_rows = jnp.zeros((n_rows, D), F32).at[:B].set(c).at[B].set(c_ctx)
    ada = _ada_vectors(c_rows, ada_w, ada_b)
    mod_lat = ada[:, :B].reshape(L, B, 1, N_ADA, D)
    mod_ctx = jnp.broadcast_to(ada[:, B].reshape(L, 1, 1, N_ADA, D), (L, B, 1, N_ADA, D))
    mods = jnp.concatenate([mod_ctx, mod_lat], axis=2)

    cos_t, sin_t = _rope_tables(n_ctx, T)
    hsum = jnp.kron(jnp.eye(RWKV_HEADS, dtype=F32), jnp.ones((HEAD_DIM, HEAD_DIM), F32))
    havg = hsum / HEAD_DIM
    tile2 = lambda g: jnp.tile(g, LANES // HEAD_DIM).reshape(1, LANES)

    xs = jnp.concatenate([ctx, x], axis=1)
    for l in range(L):
        wg = ffn_w_gu[l, :, :, :d_ff].astype(BF16)
        wu = ffn_w_gu[l, :, :, d_ff:].astype(BF16)
        wd = ffn_w_d[l].astype(BF16)
        xs = _ffn(xs, mods[l, :, :, 0:3], norm_g[l, 0].reshape(1, D), wg[0], wu[0], wd[0], ncb)

        q, k, v, ux, ug, zr, zg = _inproj(
            xs, mods[l, :, :, 3:6], norm_g[l, 1].reshape(1, D), w_in[l].astype(BF16),
            tile2(attn_q_gain[l]), tile2(attn_k_gain[l]), cos_t, sin_t, ncb, n_rwkv, n_gate)

        ya = _attention(q, k, v, attn_sink[l], n_ctx)

        ux3 = ux.reshape(TT, B, MIX)
        ug3 = ug.reshape(TT, B, MIX)
        lru_args = lambda d: (
            lru_conv_w[l], lru_conv_b[l].reshape(1, MIX),
            jnp.concatenate([_block_diag(lru_gate_w[l, d, 0]), _block_diag(lru_gate_w[l, d, 1])], axis=1).astype(BF16),
            lru_gate_b[l, d].reshape(1, 2 * MIX), lru_lambda[l, d].reshape(1, MIX))
        h_fwd = _lru_direction(ux3, *lru_args(0), n_ctx, False)
        yl = _lru_direction(ux3, *lru_args(1), n_ctx, True, h_fwd, ug3).reshape(TT, B * MIX)

        r, vv, kn, gate, lw, kd, aa = _rwkv_prepare(
            zr, rwkv_mu[l], rwkv_k_k[l].reshape(1, MIX), rwkv_k_a[l].reshape(1, MIX),
            _pad_rows(rwkv_w_up[l], rwkv_w_up.shape[2]).astype(BF16), rwkv_w0[l],
            _pad_rows(rwkv_a_up[l], rwkv_a_up.shape[2]).astype(BF16), rwkv_a0[l],
            rwkv_g_up[l].astype(BF16), hsum, n_ctx)
        o_fwd, o_rev = _rwkv_state(*_rwkv_chunks(r, vv, kn, lw, kd, aa, rwkv_r_k[l].reshape(1, MIX)), n_ctx)

        xs = _merge(xs, mods[l, :, :, 3:6], ya, yl, o_fwd, o_rev, gate,
                    rwkv_ln_g[l].reshape(1, MIX), rwkv_ln_b[l].reshape(1, MIX), havg, zg,
                    branch_proj[l].astype(BF16), w_out[l].astype(BF16), ncb)

        xs = _ffn(xs, mods[l, :, :, 6:9], norm_g[l, 2].reshape(1, D), wg[1], wu[1], wd[1], ncb)
    return xs[:, n_ctx:]
```

```python
import functools

import jax
import jax.numpy as jnp
from jax import lax
from jax.experimental import pallas as pl
from jax.experimental.pallas import tpu as pltpu

F32 = jnp.float32
BF16 = jnp.bfloat16
HIGHEST = lax.Precision.HIGHEST

N_ADA = 9
NORM_EPS = 1e-6
GRID_W = 64
HEAD_DIM = 64
ATTN_HEADS = 8
ATTN_KV_HEADS = 2
ATTN_BLOCK = 128
ROPE_BASE = 10000.0
NEG_INF = -1e30
LRU_C = 8.0
LRU_BLOCKS = 8
RWKV_HEADS = 8
RWKV_LN_EPS = 64e-5
MIX = 512
LANES = 128
TOKEN_BLOCK = 256
LRU_CHUNK = 128
RWKV_CHUNK = 64
VMEM_LIMIT = 56 * 1024 * 1024


def _params(*sem):
    return pltpu.CompilerParams(dimension_semantics=sem, vmem_limit_bytes=VMEM_LIMIT)


def _resident(shape):
    nd = len(shape)
    return pl.BlockSpec(shape, lambda *_: (0,) * nd, pipeline_mode=pl.Buffered(1))


def _dot(a, b):
    return jnp.dot(a, b, preferred_element_type=F32)


def _dot_f32(a, b):
    return jnp.dot(a, b, preferred_element_type=F32, precision=HIGHEST)


def _dot_split(x, w):
    x_hi = x.astype(BF16)
    x_lo = (x - x_hi.astype(F32)).astype(BF16)
    return _dot(x_hi, w) + _dot(x_lo, w)


def _dot_nt(a, b, precision=None):
    return lax.dot_general(a, b, (((1,), (1,)), ((), ())), preferred_element_type=F32, precision=precision)


def _dot_tn(a, b, precision=None):
    return lax.dot_general(a, b, (((0,), (0,)), ((), ())), preferred_element_type=F32, precision=precision)


def _sigmoid(x):
    return 0.5 * jnp.tanh(0.5 * x) + 0.5


def _softplus(x):
    return jnp.maximum(x, 0.0) + jnp.log(1.0 + jnp.exp(-jnp.abs(x)))


def _modulated_norm(x, g, shift, scale):
    y = x * lax.rsqrt(jnp.mean(x * x, axis=-1, keepdims=True) + NORM_EPS)
    return (y * g) * (1.0 + scale) + shift


def _ada_kernel(s_ref, w_ref, b_ref, o_ref):
    s = s_ref[...]
    s = (s * _sigmoid(s)).astype(BF16)
    o_ref[0] = _dot(s, w_ref[0].astype(BF16)) + b_ref[0]


def _ada_vectors(c_rows, ada_w, ada_b):
    L, D, N = ada_w.shape
    R = c_rows.shape[0]
    tn = 1024
    return pl.pallas_call(
        _ada_kernel,
        grid=(L, N // tn),
        in_specs=[pl.BlockSpec((R, D), lambda l, j: (0, 0)),
                  pl.BlockSpec((1, D, tn), lambda l, j: (l, 0, j)),
                  pl.BlockSpec((1, 1, tn), lambda l, j: (l, 0, j))],
        out_specs=pl.BlockSpec((1, R, tn), lambda l, j: (l, 0, j)),
        out_shape=jax.ShapeDtypeStruct((L, R, N), F32),
        compiler_params=_params("parallel", "parallel"),
        name="ada_vectors",
    )(c_rows, ada_w, ada_b.reshape(L, 1, N))


def _ffn_kernel(x_ref, mod_ref, g_ref, wg_ref, wu_ref, wd_ref, o_ref):
    x = x_ref[0]
    mod = mod_ref[0, 0]
    h = _modulated_norm(x, g_ref[...], mod[0:1], mod[1:2]).astype(BF16)
    gt = _dot(h, wg_ref[...])
    up = _dot(h, wu_ref[...])
    act = (gt * _sigmoid(gt) * up).astype(BF16)
    y = _dot(act, wd_ref[...])
    o_ref[0] = x + 0.5 * mod[2:3] * y


def _ffn(x, mod3, g, wg, wu, wd, n_ctx_blocks, latent_only=False):
    B, TT, D = x.shape
    F = wg.shape[1]
    tm = TOKEN_BLOCK
    first = n_ctx_blocks if latent_only else 0
    kind = lambda i: jnp.where(i < n_ctx_blocks, 0, 1)
    return pl.pallas_call(
        _ffn_kernel,
        grid=(B, TT // tm - first),
        in_specs=[pl.BlockSpec((1, tm, D), lambda b, i: (b, i + first, 0)),
                  pl.BlockSpec((1, 1, 3, D), lambda b, i: (b, kind(i + first), 0, 0)),
                  _resident((1, D)), _resident((D, F)), _resident((D, F)), _resident((F, D))],
        out_specs=pl.BlockSpec((1, tm, D), lambda b, i: (b, i, 0)),
        out_shape=jax.ShapeDtypeStruct((B, TT - first * tm, D), F32),
        compiler_params=_params("parallel", "parallel"),
        name="ffn",
    )(x, mod3, g, wg, wu, wd)


def _head_rms_rope(xs, gain, cos, sin, lane):
    sq = xs * xs
    lo = lane < HEAD_DIM
    s_lo = jnp.sum(jnp.where(lo, sq, 0.0), axis=-1, keepdims=True)
    s_hi = jnp.sum(jnp.where(lo, 0.0, sq), axis=-1, keepdims=True)
    ms = jnp.where(lo, s_lo, s_hi) * (1.0 / HEAD_DIM)
    y = xs * lax.rsqrt(ms + NORM_EPS) * gain
    first = (lane % 32) < 16
    partner = jnp.where(first, pltpu.roll(y, LANES - 16, axis=1), pltpu.roll(y, 16, axis=1))
    return y * cos + partner * sin


def _inproj_kernel(x_ref, mod_ref, g_ref, w_ref, qg_ref, kg_ref, cos_ref, sin_ref,
                   q_ref, k_ref, v_ref, ux_ref, ug_ref, zr_ref, zg_ref):
    x = x_ref[0]
    mod = mod_ref[0, 0]
    h = _modulated_norm(x, g_ref[...], mod[0:1], mod[1:2]).astype(BF16)
    nq = ATTN_HEADS * HEAD_DIM
    nkv = ATTN_KV_HEADS * HEAD_DIM
    lane = lax.broadcasted_iota(jnp.int32, (1, LANES), 1)
    cos = cos_ref[...]
    sin = sin_ref[...]
    za = _dot(h, w_ref[:, 0:nq + 2 * nkv])
    for s in range(nq // LANES):
        qs = _head_rms_rope(za[:, s * LANES:(s + 1) * LANES], qg_ref[...], cos, sin, lane)
        q_ref[0, :, s * LANES:(s + 1) * LANES] = (qs * HEAD_DIM ** -0.5).astype(BF16)
    k_ref[0] = _head_rms_rope(za[:, nq:nq + nkv], kg_ref[...], cos, sin, lane).astype(BF16)
    v_ref[0] = za[:, nq + nkv:nq + 2 * nkv].astype(BF16)
    c0 = nq + 2 * nkv
    zl = _dot(h, w_ref[:, c0:c0 + 2 * MIX])
    ux_ref[...] = zl[:, 0:MIX]
    ug_ref[...] = zl[:, MIX:2 * MIX]
    c1 = c0 + 2 * MIX
    nr = zr_ref.shape[2]
    zr_ref[0] = _dot(h, w_ref[:, c1:c1 + nr])
    c2 = c1 + nr
    zg_ref[0] = _dot(h, w_ref[:, c2:c2 + zg_ref.shape[2]])


def _inproj(x, mod3, g, w_in, q_gain, k_gain, cos_t, sin_t, n_ctx_blocks, n_rwkv, n_gate):
    B, TT, D = x.shape
    tm = TOKEN_BLOCK
    nq = ATTN_HEADS * HEAD_DIM
    nkv = ATTN_KV_HEADS * HEAD_DIM
    kind = lambda i: jnp.where(i < n_ctx_blocks, 0, 1)
    tok = lambda w: pl.BlockSpec((1, tm, w), lambda b, i: (b, i, 0))
    tmajor = pl.BlockSpec((tm, MIX), lambda b, i: (i, b))
    return pl.pallas_call(
        _inproj_kernel,
        grid=(B, TT // tm),
        in_specs=[tok(D),
                  pl.BlockSpec((1, 1, 3, D), lambda b, i: (b, kind(i), 0, 0)),
                  _resident((1, D)), _resident(w_in.shape), _resident((1, LANES)), _resident((1, LANES)),
                  pl.BlockSpec((tm, LANES), lambda b, i: (i, 0)),
                  pl.BlockSpec((tm, LANES), lambda b, i: (i, 0))],
        out_specs=[tok(nq), tok(nkv), tok(nkv), tmajor, tmajor, tok(n_rwkv), tok(n_gate)],
        out_shape=[jax.ShapeDtypeStruct((B, TT, nq), BF16),
                   jax.ShapeDtypeStruct((B, TT, nkv), BF16),
                   jax.ShapeDtypeStruct((B, TT, nkv), BF16),
                   jax.ShapeDtypeStruct((TT, B * MIX), F32),
                   jax.ShapeDtypeStruct((TT, B * MIX), F32),
                   jax.ShapeDtypeStruct((B, TT, n_rwkv), F32),
                   jax.ShapeDtypeStruct((B, TT, n_gate), F32)],
        compiler_params=_params("parallel", "parallel"),
        name="inproj",
    )(x, mod3, g, w_in, q_gain, k_gain, cos_t, sin_t)


def _attn_kernel(sink_ref, q_ref, kp_ref, kc_ref, kn_ref, kx_ref, vp_ref, vc_ref, vn_ref, vx_ref, o_ref,
                 *, n_ctx_blocks, n_blocks):
    i = pl.program_id(1)
    blk = ATTN_BLOCK
    n = HEAD_DIM
    rep = ATTN_HEADS // ATTN_KV_HEADS
    rows = rep * blk
    row = lax.broadcasted_iota(jnp.int32, (rows, 1), 0)

    def sink_column(g):
        col = jnp.full((rows, 1), sink_ref[g * rep], F32)
        for r in range(1, rep):
            col = jnp.where(row >= r * blk, sink_ref[g * rep + r], col)
        return col

    def attend(latent):
        if latent:
            kcat = jnp.concatenate([kp_ref[0], kc_ref[0], kn_ref[0], kx_ref[0]], axis=0)
            vcat = jnp.concatenate([vp_ref[0], vc_ref[0], vn_ref[0], vx_ref[0]], axis=0)
            far = 4 * blk
            off_prev = jnp.where(i >= n_ctx_blocks + 1, 0, far)
            off_next = jnp.where(i + 1 <= n_blocks - 1, 0, far)
            qi = lax.broadcasted_iota(jnp.int32, (rows, blk), 0) & (blk - 1)
            ki = lax.broadcasted_iota(jnp.int32, (rows, blk), 1)
            m_prev = ki >= qi + off_prev
            m_next = ki + off_next <= qi
        else:
            kcat = kx_ref[0]
            vcat = vx_ref[0]
        groups = range(ATTN_KV_HEADS)
        ss = []
        for g in groups:
            qg = jnp.concatenate([q_ref[0, :, (g * rep + r) * n:(g * rep + r + 1) * n] for r in range(rep)], axis=0)
            s = _dot_nt(qg, kcat[:, g * n:(g + 1) * n])
            if latent:
                s = jnp.concatenate([jnp.where(m_prev, s[:, 0:blk], NEG_INF), s[:, blk:2 * blk],
                                     jnp.where(m_next, s[:, 2 * blk:3 * blk], NEG_INF), s[:, 3 * blk:]], axis=1)
            ss.append(s)
        sinks = [sink_column(g) for g in groups]
        ms = [jnp.maximum(jnp.max(s, axis=-1, keepdims=True), sk) for s, sk in zip(ss, sinks)]
        ps = [jnp.exp(s - m) for s, m in zip(ss, ms)]
        dens = [jnp.sum(p, axis=-1, keepdims=True) + jnp.exp(sk - m) for p, sk, m in zip(ps, sinks, ms)]
        outs = [_dot(p.astype(BF16), vcat[:, g * n:(g + 1) * n]) / den for g, p, den in zip(groups, ps, dens)]
        for g in groups:
            for r in range(rep):
                h = g * rep + r
                o_ref[0, :, h * n:(h + 1) * n] = outs[g][r * blk:(r + 1) * blk].astype(BF16)

    @pl.when(i >= n_ctx_blocks)
    def _():
        attend(True)

    @pl.when(i < n_ctx_blocks)
    def _():
        attend(False)


def _attention(q, k, v, sink, n_ctx):
    B, TT, nq = q.shape
    nkv = k.shape[2]
    blk = ATTN_BLOCK
    nb = TT // blk
    ncb = n_ctx // blk
    kv = lambda f: pl.BlockSpec((1, blk, nkv), lambda b, i: (b, f(i), 0))
    prev = kv(lambda i: jnp.maximum(i - 1, 0))
    cur = kv(lambda i: i)
    nxt = kv(lambda i: jnp.minimum(i + 1, nb - 1))
    cx = pl.BlockSpec((1, n_ctx, nkv), lambda b, i: (b, 0, 0))
    return pl.pallas_call(
        functools.partial(_attn_kernel, n_ctx_blocks=ncb, n_blocks=nb),
        grid=(B, nb),
        in_specs=[pl.BlockSpec(memory_space=pltpu.SMEM),
                  pl.BlockSpec((1, blk, nq), lambda b, i: (b, i, 0)),
                  prev, cur, nxt, cx, prev, cur, nxt, cx],
        out_specs=pl.BlockSpec((1, blk, nq), lambda b, i: (b, i, 0)),
        out_shape=jax.ShapeDtypeStruct((B, TT, nq), BF16),
        compiler_params=_params("parallel", "parallel"),
        name="attention",
    )(sink, q, k, k, k, k, v, v, v, v)


def _lru_chunk_index(s, n_ctx_chunks, n_chunks, reverse):
    if not reverse:
        return s
    return jnp.where(s < n_ctx_chunks, n_ctx_chunks - 1 - s, n_chunks - 1 - (s - n_ctx_chunks))


def _lru_kernel(up_ref, u_ref, un_ref, cw_ref, cb_ref, gw_ref, gb_ref, lam_ref, *rest,
                n_ctx_chunks, n_chunks, reverse):
    if reverse:
        hf_ref, ug_ref, y_ref, a_scr, b_scr, h_scr = rest
    else:
        h_out_ref, a_scr, b_scr, h_scr = rest
    s = pl.program_id(0)
    c = _lru_chunk_index(s, n_ctx_chunks, n_chunks, reverse)
    tc, nb, w = u_ref.shape

    @pl.when(s == 0)
    def _():
        h_scr[...] = jnp.zeros_like(h_scr)

    has_prev = (c != 0) & (c != n_ctx_chunks)
    has_next = (c != n_ctx_chunks - 1) & (c != n_chunks - 1)
    u = u_ref[...]
    ext = jnp.concatenate([jnp.where(has_prev, up_ref[...], 0.0), u, jnp.where(has_next, un_ref[...], 0.0)], axis=0)
    cw = cw_ref[...]
    xc = cb_ref[...] + cw[0:1] * ext[0:tc] + cw[1:2] * ext[1:tc + 1] + cw[2:3] * ext[2:tc + 2] + cw[3:4] * ext[3:tc + 3]

    gates = _dot(xc.reshape(tc * nb, w).astype(BF16), gw_ref[...]).reshape(tc, nb, 2 * w) + gb_ref[...]
    r = _sigmoid(gates[:, :, 0:w])
    ig = _sigmoid(gates[:, :, w:2 * w])
    log_a = (-LRU_C * _softplus(-lam_ref[...])) * r
    a = jnp.exp(log_a)
    a_scr[...] = a
    b_scr[...] = jnp.sqrt(-jnp.tanh(log_a) * (a * a + 1.0)) * (ig * xc)

    def step(t, h):
        tt = tc - 1 - t if reverse else t
        h = a_scr[tt] * h + b_scr[tt]
        b_scr[tt] = h
        return h

    h_scr[...] = lax.fori_loop(0, tc, step, h_scr[...], unroll=8)

    if reverse:
        ug = ug_ref[...]
        gelu = 0.5 * ug * (1.0 + jnp.tanh(0.7978845608028654 * (ug + 0.044715 * ug * ug * ug)))
        y_ref[...] = ((hf_ref[...] + b_scr[...]) * gelu).astype(BF16)
    else:
        h_out_ref[...] = b_scr[...]


def _lru_direction(ux3, conv_w, conv_b, gate_w, gate_b, lam, n_ctx, reverse, h_fwd=None, ug3=None):
    TT, B, W = ux3.shape
    tc = LRU_CHUNK
    nch = TT // tc
    ncc = n_ctx // tc
    cidx = lambda s: _lru_chunk_index(s, ncc, nch, reverse)
    chunk = pl.BlockSpec((tc, B, W), lambda s: (cidx(s), 0, 0))
    in_specs = [pl.BlockSpec((2, B, W), lambda s: (jnp.maximum(cidx(s) * (tc // 2) - 1, 0), 0, 0)),
                chunk,
                pl.BlockSpec((1, B, W), lambda s: (jnp.minimum((cidx(s) + 1) * tc, TT - 1), 0, 0)),
                _resident((4, W)), _resident((1, W)), _resident((W, 2 * W)), _resident((1, 2 * W)),
                _resident((1, W))]
    args = [ux3, ux3, ux3, conv_w, conv_b, gate_w, gate_b, lam]
    if reverse:
        in_specs += [chunk, chunk]
        args += [h_fwd, ug3]
        out_dtype = BF16
    else:
        out_dtype = F32
    return pl.pallas_call(
        functools.partial(_lru_kernel, n_ctx_chunks=ncc, n_chunks=nch, reverse=reverse),
        grid=(nch,),
        in_specs=in_specs,
        out_specs=chunk,
        out_shape=jax.ShapeDtypeStruct((TT, B, W), out_dtype),
        scratch_shapes=[pltpu.VMEM((tc, B, W), F32), pltpu.VMEM((tc, B, W), F32), pltpu.VMEM((B, W), F32)],
        compiler_params=_params("arbitrary"),
        name="lru_rev" if reverse else "lru_fwd",
    )(*args)


def _rwkv_prep_kernel(zp_ref, z_ref, zn_ref, mu_ref, kk_ref, ka_ref, wup_ref, w0_ref, aup_ref, a0_ref, gup_ref,
                      hsum_ref, r_ref, v_ref, kn_ref, gate_ref, lw_ref, kd_ref, a_ref,
                      *, n_ctx_blocks, n_blocks):
    i = pl.program_id(1)
    has_prev = (i != 0) & (i != n_ctx_blocks)
    has_next = (i != n_ctx_blocks - 1) & (i != n_blocks - 1)
    z = z_ref[0]
    tm = z.shape[0]
    row = lax.broadcasted_iota(jnp.int32, (tm, 1), 0)
    z_first = jnp.where(has_prev, zp_ref[0, 7:8, :], 0.0)
    z_last = jnp.where(has_next, zn_ref[0, 0:1, :], 0.0)
    z_prev = jnp.where(row == 0, z_first, pltpu.roll(z, 1, axis=0))
    z_next = jnp.where(row == tm - 1, z_last, pltpu.roll(z, tm - 1, axis=0))
    mu = mu_ref[...]
    zs = z + mu[0:1] * (z_prev - z) + mu[1:2] * (z_next - z)
    w = MIX
    r = zs[:, 0:w]
    k = zs[:, w:2 * w]
    v = zs[:, 2 * w:3 * w]
    wd = jnp.tanh(zs[:, 3 * w:3 * w + LANES]).astype(BF16)
    ad = zs[:, 3 * w + LANES:3 * w + 2 * LANES].astype(BF16)
    gd = _sigmoid(zs[:, 3 * w + 2 * LANES:3 * w + 3 * LANES]).astype(BF16)
    r_ref[0] = r
    v_ref[0] = v
    kk = k * kk_ref[...]
    norm = jnp.sqrt(_dot_split(kk * kk, hsum_ref[...]))
    kn_ref[0] = kk / jnp.maximum(norm, 1e-12)
    gate_ref[0] = _dot(gd, gup_ref[...])
    for d in range(2):
        logw = w0_ref[d:d + 1] + _dot(wd, wup_ref[d])
        lw_ref[d, 0] = -jnp.exp(-_softplus(-logw) - 0.5)
        a = _sigmoid(a0_ref[d:d + 1] + _dot(ad, aup_ref[d]))
        a_ref[d, 0] = a
        kd_ref[d, 0] = k * (1.0 + (a - 1.0) * ka_ref[...])


def _rwkv_prepare(zr, mu, k_k, k_a, w_up, w0, a_up, a0, g_up, hsum, n_ctx):
    B, TT, NR = zr.shape
    tm = TOKEN_BLOCK
    nb = TT // tm
    ncb = n_ctx // tm
    rows = tm // 8
    tok = pl.BlockSpec((1, tm, MIX), lambda b, i: (b, i, 0))
    tok2 = pl.BlockSpec((2, 1, tm, MIX), lambda b, i: (0, b, i, 0))
    one = jax.ShapeDtypeStruct((B, TT, MIX), F32)
    two = jax.ShapeDtypeStruct((2, B, TT, MIX), F32)
    return pl.pallas_call(
        functools.partial(_rwkv_prep_kernel, n_ctx_blocks=ncb, n_blocks=nb),
        grid=(B, nb),
        in_specs=[pl.BlockSpec((1, 8, NR), lambda b, i: (b, jnp.maximum(i * rows - 1, 0), 0)),
                  pl.BlockSpec((1, tm, NR), lambda b, i: (b, i, 0)),
                  pl.BlockSpec((1, 8, NR), lambda b, i: (b, jnp.minimum((i + 1) * rows, TT // 8 - 1), 0)),
                  _resident((2, NR)), _resident((1, MIX)), _resident((1, MIX)),
                  _resident((2, LANES, MIX)), _resident((2, MIX)), _resident((2, LANES, MIX)), _resident((2, MIX)),
                  _resident((LANES, MIX)), _resident((MIX, MIX))],
        out_specs=[tok, tok, tok, tok, tok2, tok2, tok2],
        out_shape=[one, one, one, one, two, two, two],
        compiler_params=_params("parallel", "parallel"),
        name="rwkv_prepare",
    )(zr, zr, zr, mu, k_k, k_a, w_up, w0, a_up, a0, g_up, hsum)


GROUP = 4 * HEAD_DIM


def _expand(y, lanehead):
    return jnp.concatenate([jnp.where(lanehead == h, y, jnp.zeros_like(y)) for h in range(4)], axis=0)


def _compact(z, lanehead):
    n = HEAD_DIM
    out = jnp.where(lanehead == 0, z[0:n], 0.0)
    for h in range(1, 4):
        out = out + jnp.where(lanehead == h, z[h * n:(h + 1) * n], 0.0)
    return out


def _rwkv_chunk_kernel(r_ref, v_ref, kn_ref, lw_ref, kd_ref, a_ref, rk_ref, q_out, o_out, g_out, h_out, pc_out):
    d = pl.program_id(0)
    C = RWKV_CHUNK
    tm = r_ref.shape[1]
    sgn = 1 - 2 * d
    ri = lax.broadcasted_iota(jnp.int32, (C, C), 0)
    ci = lax.broadcasted_iota(jnp.int32, (C, C), 1)
    tri = ((ri - ci) * sgn >= 0).astype(F32)
    t = lax.broadcasted_iota(jnp.int32, (C, GROUP), 0)
    lane = lax.broadcasted_iota(jnp.int32, (C, GROUP), 1)
    i = lane & (C - 1)
    eye = t == i
    same16 = (t >> 4) == (i >> 4)
    same32 = (t >> 5) == (i >> 5)
    lanehead = lax.broadcasted_iota(jnp.int32, (1, GROUP), 1) >> 6
    t2 = lax.broadcasted_iota(jnp.int32, (2 * C, GROUP), 0)
    i2 = lax.broadcasted_iota(jnp.int32, (2 * C, GROUP), 1) & (C - 1)
    mask2 = ((t2 & (C - 1)) - i2) * sgn > jnp.where(t2 < C, 0, -1)
    rowhead = lax.broadcasted_iota(jnp.int32, (GROUP, GROUP), 0) >> 6
    colhead = lax.broadcasted_iota(jnp.int32, (GROUP, GROUP), 1) >> 6
    head_ones = (rowhead == colhead).astype(BF16)

    def mm(x, y):
        return _dot(x.astype(BF16), _expand(y.astype(BF16), lanehead))

    def each(f, *lists):
        return [f(*xs) for xs in zip(*lists)]

    n_ch = tm // C
    units = [(slice(ch * C, (ch + 1) * C), slice(g * GROUP, (g + 1) * GROUP), ch)
             for ch in range(n_ch) for g in range(RWKV_HEADS // 4)]
    l_in_all, l_tot_all = [], []
    for ch in range(n_ch):
        lw_c = lw_ref[0, 0, ch * C:(ch + 1) * C, :]
        l_in_all.append(_dot_f32(tri, lw_c))
        l_tot_all.append(jnp.sum(lw_c, axis=0, keepdims=True))
        pc_out[0, 0, ch * 8:(ch + 1) * 8, :] = jnp.broadcast_to(jnp.exp(l_tot_all[ch]), (8, lw_c.shape[1]))

    a_bar, r_bar, lhs, b_inv, k_inv, b_end, k_end, vs = [], [], [], [], [], [], [], []
    for rows, ls, ch in units:
        kn = kn_ref[0, rows, ls]
        k = kd_ref[0, 0, rows, ls]
        l_in = l_in_all[ch][:, ls]
        beta = kn * a_ref[0, 0, rows, ls]
        p_inv = jnp.exp(-l_in)
        p_end = jnp.exp(l_tot_all[ch][:, ls] - l_in)
        a_bar.append(-kn * jnp.exp(l_in - lw_ref[0, 0, rows, ls]))
        r_bar.append(r_ref[0, rows, ls] * jnp.exp(l_in))
        lhs.append(jnp.concatenate([a_bar[-1], r_bar[-1]], axis=0).astype(BF16))
        b_inv.append((beta * p_inv).astype(BF16))
        k_inv.append((k * p_inv).astype(BF16))
        b_end.append((beta * p_end).astype(BF16))
        k_end.append((k * p_end).astype(BF16))
        vs.append(v_ref[0, rows, ls])

    mb = each(lambda l, y: jnp.where(mask2, _dot_nt(l, _expand(y, lanehead)), 0.0), lhs, b_inv)
    mk = each(lambda l, y: jnp.where(mask2, _dot_nt(l, _expand(y, lanehead)), 0.0), lhs, k_inv)
    m_ab = [m[0:C] for m in mb]
    m_rb = [m[C:2 * C] for m in mb]
    m_ak = [m[0:C] for m in mk]
    m_rk = [m[C:2 * C] for m in mk]
    d1 = [jnp.where(same16, m, 0.0) for m in m_ab]
    in32 = [jnp.where(same32, m, 0.0) for m in m_ab]
    d2 = each(mm, d1, d1)
    d4 = each(mm, d2, d2)
    tinv = [jnp.where(eye, 1.0, m) for m in d1]
    tinv = each(lambda t, p: t + mm(t, p), tinv, d2)
    d8 = each(mm, d4, d4)
    tinv = each(lambda t, p: t + mm(t, p), tinv, d4)
    mkv = each(mm, mk, vs)
    x1 = [m[0:C] for m in mkv]
    o_loc = [m[C:2 * C] for m in mkv]
    tinv = each(lambda t, p: t + mm(t, p), tinv, d8)
    te = each(lambda t, a, b: mm(t, a - b), tinv, in32, d1)
    tinv = each(lambda t, e: t + mm(e, t), tinv, te)
    te = each(lambda t, a, b: mm(t, a - b), tinv, m_ab, in32)
    tinv = each(lambda t, e: t + mm(e, t), tinv, te)
    w_hat = each(mm, tinv, a_bar)
    u_hat = each(mm, tinv, x1)
    for (rows, ls, ch), rb, w, rbar in zip(units, m_rb, w_hat, r_bar):
        q_out[0, 0, rows, ls] = (rbar + mm(rb, w)).astype(BF16)
    rkr = _dot_split(jnp.concatenate([r_ref[0, rows, ls] * kd_ref[0, 0, rows, ls] * rk_ref[:, ls]
                                      for rows, ls, ch in units], axis=0), head_ones)
    for n, ((rows, ls, ch), rb, u, o, v) in enumerate(zip(units, m_rb, u_hat, o_loc, vs)):
        o_out[0, 0, rows, ls] = o + mm(rb, u) + rkr[n * C:(n + 1) * C] * v
    for (rows, ls, ch), w, be in zip(units, w_hat, b_end):
        g_out[0, 0, rows, ls] = _compact(_dot_tn(w.astype(BF16), be), lanehead).astype(BF16)
    for (rows, ls, ch), v, u, ke, be in zip(units, vs, u_hat, k_end, b_end):
        zh = _dot_tn(jnp.concatenate([v, u], axis=0).astype(BF16), jnp.concatenate([ke, be], axis=0))
        h_out[0, 0, rows, ls] = _compact(zh, lanehead)


def _rwkv_chunks(r, v, kn, lw, kd, a, r_k):
    B, TT, W = r.shape
    tm = TOKEN_BLOCK
    C = RWKV_CHUNK
    assert C == HEAD_DIM and tm % C == 0
    one = pl.BlockSpec((1, tm, W), lambda d, b, i: (b, i, 0))
    two = pl.BlockSpec((1, 1, tm, W), lambda d, b, i: (d, b, i, 0))
    pc = pl.BlockSpec((1, 1, 8 * (tm // C), W), lambda d, b, i: (d, b, i, 0))
    shape = lambda dt: jax.ShapeDtypeStruct((2, B, TT, W), dt)
    return pl.pallas_call(
        _rwkv_chunk_kernel,
        grid=(2, B, TT // tm),
        in_specs=[one, one, one, two, two, two, _resident((1, W))],
        out_specs=[two, two, two, two, pc],
        out_shape=[shape(BF16), shape(F32), shape(BF16), shape(F32),
                   jax.ShapeDtypeStruct((2, B, 8 * (TT // C), W), F32)],
        compiler_params=_params("parallel", "parallel", "parallel"),
        name="rwkv_chunks",
    )(r, v, kn, lw, kd, a, r_k)


def _rwkv_state_kernel(qf_ref, of_ref, gf_ref, hf_ref, pf_ref, qr_ref, or_ref, gr_ref, hr_ref, pr_ref,
                       outf_ref, outr_ref, s_scr):
    s = pl.program_id(0)

    @pl.when(s == 0)
    def _():
        s_scr[...] = jnp.zeros_like(s_scr)

    lanehead = lax.broadcasted_iota(jnp.int32, (1, GROUP), 1) >> 6
    nb = outf_ref.shape[0]
    dirs = ((qf_ref, of_ref, gf_ref, hf_ref, pf_ref, outf_ref), (qr_ref, or_ref, gr_ref, hr_ref, pr_ref, outr_ref))
    for d, (q_ref, o_ref, g_ref, h_ref, p_ref, out_ref) in enumerate(dirs):
        for b in range(nb):
            for g in range(RWKV_HEADS // 4):
                ls = slice(g * GROUP, (g + 1) * GROUP)
                st = s_scr[d, b, g]
                sb = st.astype(BF16)
                out_ref[b, :, ls] = _dot_nt(q_ref[0, b, :, ls], _expand(sb, lanehead)) + o_ref[0, b, :, ls]
                s_scr[d, b, g] = (st * p_ref[0, b, 0:1, ls] + _dot(sb, _expand(g_ref[0, b, :, ls], lanehead))
                                  + h_ref[0, b, :, ls])


def _rwkv_state(q, o0, gm, hm, pc, n_ctx):
    _, B, TT, W = q.shape
    C = RWKV_CHUNK
    nch = TT // C
    ncc = n_ctx // C
    fwd = lambda s: s
    rev = lambda s: jnp.where(s < ncc, ncc - 1 - s, nch - 1 - (s - ncc))
    spec = lambda d, f, rows: pl.BlockSpec((1, B, rows, W), lambda s: (d, 0, f(s), 0))
    out = lambda f: pl.BlockSpec((B, C, W), lambda s: (0, f(s), 0))
    shape = jax.ShapeDtypeStruct((B, TT, W), F32)
    return pl.pallas_call(
        _rwkv_state_kernel,
        grid=(nch,),
        in_specs=[spec(0, fwd, C)] * 4 + [spec(0, fwd, 8)] + [spec(1, rev, C)] * 4 + [spec(1, rev, 8)],
        out_specs=[out(fwd), out(rev)],
        out_shape=[shape, shape],
        scratch_shapes=[pltpu.VMEM((2, B, RWKV_HEADS // 4, HEAD_DIM, GROUP), F32)],
        compiler_params=_params("arbitrary"),
        name="rwkv_state",
    )(q, o0, gm, hm, pc, q, o0, gm, hm, pc)


def _merge_kernel(x_ref, mod_ref, ya_ref, yl_ref, of_ref, or_ref, gate_ref, lng_ref, lnb_ref, havg_ref, zg_ref,
                  proj_ref, wo_ref, out_ref):
    D = x_ref.shape[2]
    o = of_ref[0] + or_ref[0]
    mean = _dot_split(o, havg_ref[...])
    cen = o - mean
    var = _dot_split(cen * cen, havg_ref[...])
    yr = ((cen * lax.rsqrt(var + RWKV_LN_EPS)) * lng_ref[...] + lnb_ref[...]) * gate_ref[0]
    ys = (ya_ref[0], yl_ref[...], yr.astype(BF16))
    acc = None
    for b in range(3):
        term = _sigmoid(zg_ref[0, :, b * D:(b + 1) * D]) * _dot(ys[b], proj_ref[b])
        acc = term if acc is None else acc + term
    out_ref[0] = x_ref[0] + mod_ref[0, 0][2:3] * _dot(acc.astype(BF16), wo_ref[...])


def _merge(x, mod3, ya, yl, o_fwd, o_rev, gate, ln_g, ln_b, havg, zg, proj, w_o, n_ctx_blocks):
    B, TT, D = x.shape
    tm = TOKEN_BLOCK
    kind = lambda i: jnp.where(i < n_ctx_blocks, 0, 1)
    tok = lambda w: pl.BlockSpec((1, tm, w), lambda b, i: (b, i, 0))
    return pl.pallas_call(
        _merge_kernel,
        grid=(B, TT // tm),
        in_specs=[tok(D),
                  pl.BlockSpec((1, 1, 3, D), lambda b, i: (b, kind(i), 0, 0)),
                  tok(MIX),
                  pl.BlockSpec((tm, MIX), lambda b, i: (i, b)),
                  tok(MIX), tok(MIX), tok(MIX),
                  _resident((1, MIX)), _resident((1, MIX)), _resident((MIX, MIX)),
                  tok(3 * D),
                  _resident((3, MIX, D)), _resident((D, D))],
        out_specs=tok(D),
        out_shape=jax.ShapeDtypeStruct((B, TT, D), F32),
        compiler_params=_params("parallel", "parallel"),
        name="merge",
    )(x, mod3, ya, yl, o_fwd, o_rev, gate, ln_g, ln_b, havg, zg, proj, w_o)


def _rope_tables(n_ctx, n_tok):
    n_freq = HEAD_DIM // 4
    t = jnp.arange(n_tok)
    row = (t // GRID_W).astype(F32)
    col = (t % GRID_W).astype(F32)
    inv = ROPE_BASE ** (-jnp.arange(n_freq, dtype=F32) / n_freq)
    ang_r = row[:, None] * inv
    ang_c = col[:, None] * inv
    cos = jnp.concatenate([jnp.cos(ang_r)] * 2 + [jnp.cos(ang_c)] * 2, axis=-1)
    sin = jnp.concatenate([-jnp.sin(ang_r), jnp.sin(ang_r), -jnp.sin(ang_c), jnp.sin(ang_c)], axis=-1)
    cos = jnp.concatenate([jnp.ones((n_ctx, HEAD_DIM), F32), cos], axis=0)
    sin = jnp.concatenate([jnp.zeros((n_ctx, HEAD_DIM), F32), sin], axis=0)
    return jnp.tile(cos, (1, LANES // HEAD_DIM)), jnp.tile(sin, (1, LANES // HEAD_DIM))


def _block_diag(w):
    n, r, c = w.shape
    eye = jnp.eye(n, dtype=w.dtype)
    return (eye[:, None, :, None] * w[:, :, None, :]).reshape(n * r, n * c)


def _pad_rows(w, rank):
    z = jnp.zeros_like(w[0])
    return jnp.stack([jnp.concatenate([w[0], z], axis=0), jnp.concatenate([z, w[1]], axis=0)])


def kernel(x, c, ctx, c_ctx, ada_w, ada_b, norm_g, ffn_w_gu, ffn_w_d, w_in, attn_q_gain, attn_k_gain, attn_sink, lru_conv_w, lru_conv_b, lru_gate_w, lru_gate_b, lru_lambda, rwkv_mu, rwkv_w_up, rwkv_w0, rwkv_a_up, rwkv_a0, rwkv_g_up, rwkv_k_k, rwkv_k_a, rwkv_r_k, rwkv_ln_g, rwkv_ln_b, branch_proj, w_out):
    B, T, D = x.shape
    n_ctx = ctx.shape[1]
    L = ada_w.shape[0]
    TT = n_ctx + T
    assert n_ctx % TOKEN_BLOCK == 0 and T % TOKEN_BLOCK == 0 and T % GRID_W == 0
    ncb = n_ctx // TOKEN_BLOCK
    d_ff = ffn_w_d.shape[2]
    n_rwkv = rwkv_mu.shape[2]
    n_gate = 3 * D

    n_rows = -(-(B + 1) // 8) * 8
    c_rows = jnp.zeros((n_rows, D), F32).at[:B].set(c).at[B].set(c_ctx)
    ada = _ada_vectors(c_rows, ada_w, ada_b)
    mod_lat = ada[:, :B].reshape(L, B, 1, N_ADA, D)
    mod_ctx = jnp.broadcast_to(ada[:, B].reshape(L, 1, 1, N_ADA, D), (L, B, 1, N_ADA, D))
    mods = jnp.concatenate([mod_ctx, mod_lat], axis=2)

    cos_t, sin_t = _rope_tables(n_ctx, T)
    hsum = jnp.kron(jnp.eye(RWKV_HEADS, dtype=BF16), jnp.ones((HEAD_DIM, HEAD_DIM), BF16))
    havg = hsum / HEAD_DIM
    tile2 = lambda g: jnp.tile(g, LANES // HEAD_DIM).reshape(1, LANES)

    xs = jnp.concatenate([ctx, x], axis=1)
    for l in range(L):
        wg = ffn_w_gu[l, :, :, :d_ff].astype(BF16)
        wu = ffn_w_gu[l, :, :, d_ff:].astype(BF16)
        wd = ffn_w_d[l].astype(BF16)
        xs = _ffn(xs, mods[l, :, :, 0:3], norm_g[l, 0].reshape(1, D), wg[0], wu[0], wd[0], ncb)

        q, k, v, ux, ug, zr, zg = _inproj(
            xs, mods[l, :, :, 3:6], norm_g[l, 1].reshape(1, D), w_in[l].astype(BF16),
            tile2(attn_q_gain[l]), tile2(attn_k_gain[l]), cos_t, sin_t, ncb, n_rwkv, n_gate)

        ya = _attention(q, k, v, attn_sink[l], n_ctx)

        ux3 = ux.reshape(TT, B, MIX)
        ug3 = ug.reshape(TT, B, MIX)
        lru_args = lambda d: (
            lru_conv_w[l], lru_conv_b[l].reshape(1, MIX),
            jnp.concatenate([_block_diag(lru_gate_w[l, d, 0]), _block_diag(lru_gate_w[l, d, 1])], axis=1).astype(BF16),
            lru_gate_b[l, d].reshape(1, 2 * MIX), lru_lambda[l, d].reshape(1, MIX))
        h_fwd = _lru_direction(ux3, *lru_args(0), n_ctx, False)
        yl = _lru_direction(ux3, *lru_args(1), n_ctx, True, h_fwd, ug3).reshape(TT, B * MIX)

        r, vv, kn, gate, lw, kd, aa = _rwkv_prepare(
            zr, rwkv_mu[l], rwkv_k_k[l].reshape(1, MIX), rwkv_k_a[l].reshape(1, MIX),
            _pad_rows(rwkv_w_up[l], rwkv_w_up.shape[2]).astype(BF16), rwkv_w0[l],
            _pad_rows(rwkv_a_up[l], rwkv_a_up.shape[2]).astype(BF16), rwkv_a0[l],
            rwkv_g_up[l].astype(BF16), hsum, n_ctx)
        o_fwd, o_rev = _rwkv_state(*_rwkv_chunks(r, vv, kn, lw, kd, aa, rwkv_r_k[l].reshape(1, MIX)), n_ctx)

        xs = _merge(xs, mods[l, :, :, 3:6], ya, yl, o_fwd, o_rev, gate,
                    rwkv_ln_g[l].reshape(1, MIX), rwkv_ln_b[l].reshape(1, MIX), havg, zg,
                    branch_proj[l].astype(BF16), w_out[l].astype(BF16), ncb)

        xs = _ffn(xs, mods[l, :, :, 6:9], norm_g[l, 2].reshape(1, D), wg[1], wu[1], wd[1], ncb,
                  latent_only=(l == L - 1))
    return xs
```

```python
import functools

import jax
import jax.numpy as jnp
from jax import lax
from jax.experimental import pallas as pl
from jax.experimental.pallas import tpu as pltpu

F32 = jnp.float32
BF16 = jnp.bfloat16
HIGHEST = lax.Precision.HIGHEST

N_ADA = 9
NORM_EPS = 1e-6
GRID_W = 64
HEAD_DIM = 64
ATTN_HEADS = 8
ATTN_KV_HEADS = 2
ATTN_BLOCK = 128
ROPE_BASE = 10000.0
NEG_INF = -1e30
LRU_C = 8.0
LRU_BLOCKS = 8
RWKV_HEADS = 8
RWKV_LN_EPS = 64e-5
MIX = 512
LANES = 128
TOKEN_BLOCK = 256
LRU_CHUNK = 128
RWKV_CHUNK = 64
VMEM_LIMIT = 56 * 1024 * 1024


def _params(*sem):
    return pltpu.CompilerParams(dimension_semantics=sem, vmem_limit_bytes=VMEM_LIMIT)


def _resident(shape):
    nd = len(shape)
    return pl.BlockSpec(shape, lambda *_: (0,) * nd, pipeline_mode=pl.Buffered(1))


def _dot(a, b):
    return jnp.dot(a, b, preferred_element_type=F32)


def _dot_f32(a, b):
    return jnp.dot(a, b, preferred_element_type=F32, precision=HIGHEST)


def _dot_split(x, w):
    x_hi = x.astype(BF16)
    x_lo = (x - x_hi.astype(F32)).astype(BF16)
    return _dot(x_hi, w) + _dot(x_lo, w)


def _dot_nt(a, b, precision=None):
    return lax.dot_general(a, b, (((1,), (1,)), ((), ())), preferred_element_type=F32, precision=precision)


def _dot_tn(a, b, precision=None):
    return lax.dot_general(a, b, (((0,), (0,)), ((), ())), preferred_element_type=F32, precision=precision)


def _sigmoid(x):
    return 0.5 * jnp.tanh(0.5 * x) + 0.5


def _softplus(x):
    return jnp.maximum(x, 0.0) + jnp.log(1.0 + jnp.exp(-jnp.abs(x)))


def _modulated_norm(x, g, shift, scale):
    y = x * lax.rsqrt(jnp.mean(x * x, axis=-1, keepdims=True) + NORM_EPS)
    return (y * g) * (1.0 + scale) + shift


def _ada_kernel(s_ref, w_ref, b_ref, o_ref):
    s = s_ref[...]
    s = (s * _sigmoid(s)).astype(BF16)
    o_ref[0] = _dot(s, w_ref[0].astype(BF16)) + b_ref[0]


def _ada_vectors(c_rows, ada_w, ada_b):
    L, D, N = ada_w.shape
    R = c_rows.shape[0]
    tn = 1024
    return pl.pallas_call(
        _ada_kernel,
        grid=(L, N // tn),
        in_specs=[pl.BlockSpec((R, D), lambda l, j: (0, 0)),
                  pl.BlockSpec((1, D, tn), lambda l, j: (l, 0, j)),
                  pl.BlockSpec((1, 1, tn), lambda l, j: (l, 0, j))],
        out_specs=pl.BlockSpec((1, R, tn), lambda l, j: (l, 0, j)),
        out_shape=jax.ShapeDtypeStruct((L, R, N), F32),
        compiler_params=_params("parallel", "parallel"),
        name="ada_vectors",
    )(c_rows, ada_w, ada_b.reshape(L, 1, N))


def _ffn_kernel(*refs, n_ctx_blocks, split_input):
    if split_input:
        c_ref, x_ref, mod_ref, g_ref, wg_ref, wu_ref, wd_ref, o_ref = refs
        x = jnp.where(pl.program_id(1) < n_ctx_blocks, c_ref[0], x_ref[0])
    else:
        x_ref, mod_ref, g_ref, wg_ref, wu_ref, wd_ref, o_ref = refs
        x = x_ref[0]
    mod = mod_ref[0, 0]
    h = _modulated_norm(x, g_ref[...], mod[0:1], mod[1:2]).astype(BF16)
    gt = _dot(h, wg_ref[...])
    up = _dot(h, wu_ref[...])
    act = (gt * _sigmoid(gt) * up).astype(BF16)
    y = _dot(act, wd_ref[...])
    o_ref[0] = x + 0.5 * mod[2:3] * y


def _ffn(x, mod3, g, wg, wu, wd, n_ctx_blocks, ctx=None, latent_only=False):
    B, _, D = x.shape
    F = wg.shape[1]
    tm = TOKEN_BLOCK
    first = n_ctx_blocks if latent_only else 0
    kind = lambda i: jnp.where(i < n_ctx_blocks, 0, 1)
    if ctx is None:
        n_out = x.shape[1] // tm - first
        x_specs = [pl.BlockSpec((1, tm, D), lambda b, i: (b, i + first, 0))]
        xs = (x,)
    else:
        n_out = n_ctx_blocks + x.shape[1] // tm
        x_specs = [pl.BlockSpec((1, tm, D), lambda b, i: (b, jnp.minimum(i, n_ctx_blocks - 1), 0)),
                   pl.BlockSpec((1, tm, D), lambda b, i: (b, jnp.maximum(i - n_ctx_blocks, 0), 0))]
        xs = (ctx, x)
    return pl.pallas_call(
        functools.partial(_ffn_kernel, n_ctx_blocks=n_ctx_blocks, split_input=ctx is not None),
        grid=(B, n_out),
        in_specs=x_specs + [pl.BlockSpec((1, 1, 3, D), lambda b, i: (b, kind(i + first), 0, 0)),
                            _resident((1, D)), _resident((D, F)), _resident((D, F)), _resident((F, D))],
        out_specs=pl.BlockSpec((1, tm, D), lambda b, i: (b, i, 0)),
        out_shape=jax.ShapeDtypeStruct((B, n_out * tm, D), F32),
        compiler_params=_params("parallel", "parallel"),
        name="ffn",
    )(*xs, mod3, g, wg, wu, wd)


def _head_rms_rope(xs, gain, cos, sin, lane):
    sq = xs * xs
    lo = lane < HEAD_DIM
    s_lo = jnp.sum(jnp.where(lo, sq, 0.0), axis=-1, keepdims=True)
    s_hi = jnp.sum(jnp.where(lo, 0.0, sq), axis=-1, keepdims=True)
    ms = jnp.where(lo, s_lo, s_hi) * (1.0 / HEAD_DIM)
    y = xs * lax.rsqrt(ms + NORM_EPS) * gain
    first = (lane % 32) < 16
    partner = jnp.where(first, pltpu.roll(y, LANES - 16, axis=1), pltpu.roll(y, 16, axis=1))
    return y * cos + partner * sin


def _inproj_kernel(x_ref, mod_ref, g_ref, w_ref, qg_ref, kg_ref, cos_ref, sin_ref,
                   q_ref, k_ref, v_ref, ux_ref, ug_ref, zr_ref, zg_ref):
    x = x_ref[0]
    mod = mod_ref[0, 0]
    h = _modulated_norm(x, g_ref[...], mod[0:1], mod[1:2]).astype(BF16)
    nq = ATTN_HEADS * HEAD_DIM
    nkv = ATTN_KV_HEADS * HEAD_DIM
    lane = lax.broadcasted_iota(jnp.int32, (1, LANES), 1)
    cos = cos_ref[...]
    sin = sin_ref[...]
    za = _dot(h, w_ref[:, 0:nq + 2 * nkv])
    for s in range(nq // LANES):
        qs = _head_rms_rope(za[:, s * LANES:(s + 1) * LANES], qg_ref[...], cos, sin, lane)
        q_ref[0, :, s * LANES:(s + 1) * LANES] = (qs * HEAD_DIM ** -0.5).astype(BF16)
    k_ref[0] = _head_rms_rope(za[:, nq:nq + nkv], kg_ref[...], cos, sin, lane).astype(BF16)
    v_ref[0] = za[:, nq + nkv:nq + 2 * nkv].astype(BF16)
    c0 = nq + 2 * nkv
    zl = _dot(h, w_ref[:, c0:c0 + 2 * MIX])
    ux_ref[...] = zl[:, 0:MIX]
    ug_ref[...] = zl[:, MIX:2 * MIX]
    c1 = c0 + 2 * MIX
    nr = zr_ref.shape[2]
    zr_ref[0] = _dot(h, w_ref[:, c1:c1 + nr])
    c2 = c1 + nr
    zg_ref[0] = _dot(h, w_ref[:, c2:c2 + zg_ref.shape[2]])


def _inproj(x, mod3, g, w_in, q_gain, k_gain, cos_t, sin_t, n_ctx_blocks, n_rwkv, n_gate):
    B, TT, D = x.shape
    tm = TOKEN_BLOCK
    nq = ATTN_HEADS * HEAD_DIM
    nkv = ATTN_KV_HEADS * HEAD_DIM
    kind = lambda i: jnp.where(i < n_ctx_blocks, 0, 1)
    tok = lambda w: pl.BlockSpec((1, tm, w), lambda b, i: (b, i, 0))
    tmajor = pl.BlockSpec((tm, MIX), lambda b, i: (i, b))
    return pl.pallas_call(
        _inproj_kernel,
        grid=(B, TT // tm),
        in_specs=[tok(D),
                  pl.BlockSpec((1, 1, 3, D), lambda b, i: (b, kind(i), 0, 0)),
                  _resident((1, D)), _resident(w_in.shape), _resident((1, LANES)), _resident((1, LANES)),
                  pl.BlockSpec((tm, LANES), lambda b, i: (i, 0)),
                  pl.BlockSpec((tm, LANES), lambda b, i: (i, 0))],
        out_specs=[tok(nq), tok(nkv), tok(nkv), tmajor, tmajor, tok(n_rwkv), tok(n_gate)],
        out_shape=[jax.ShapeDtypeStruct((B, TT, nq), BF16),
                   jax.ShapeDtypeStruct((B, TT, nkv), BF16),
                   jax.ShapeDtypeStruct((B, TT, nkv), BF16),
                   jax.ShapeDtypeStruct((TT, B * MIX), F32),
                   jax.ShapeDtypeStruct((TT, B * MIX), F32),
                   jax.ShapeDtypeStruct((B, TT, n_rwkv), F32),
                   jax.ShapeDtypeStruct((B, TT, n_gate), F32)],
        compiler_params=_params("parallel", "parallel"),
        name="inproj",
    )(x, mod3, g, w_in, q_gain, k_gain, cos_t, sin_t)


def _attn_kernel(sink_ref, q_ref, kp_ref, kc_ref, kn_ref, kx_ref, vp_ref, vc_ref, vn_ref, vx_ref, o_ref,
                 *, n_ctx_blocks, n_blocks):
    i = pl.program_id(1)
    blk = ATTN_BLOCK
    n = HEAD_DIM
    rep = ATTN_HEADS // ATTN_KV_HEADS
    rows = rep * blk
    row = lax.broadcasted_iota(jnp.int32, (rows, 1), 0)

    def sink_column(g):
        col = jnp.full((rows, 1), sink_ref[g * rep], F32)
        for r in range(1, rep):
            col = jnp.where(row >= r * blk, sink_ref[g * rep + r], col)
        return col

    def attend(latent):
        if latent:
            kcat = jnp.concatenate([kp_ref[0], kc_ref[0], kn_ref[0], kx_ref[0]], axis=0)
            vcat = jnp.concatenate([vp_ref[0], vc_ref[0], vn_ref[0], vx_ref[0]], axis=0)
            far = 4 * blk
            off_prev = jnp.where(i >= n_ctx_blocks + 1, 0, far)
            off_next = jnp.where(i + 1 <= n_blocks - 1, 0, far)
            qi = lax.broadcasted_iota(jnp.int32, (rows, blk), 0) & (blk - 1)
            ki = lax.broadcasted_iota(jnp.int32, (rows, blk), 1)
            m_prev = ki >= qi + off_prev
            m_next = ki + off_next <= qi
        else:
            kcat = kx_ref[0]
            vcat = vx_ref[0]
        groups = range(ATTN_KV_HEADS)
        ss = []
        for g in groups:
            qg = jnp.concatenate([q_ref[0, :, (g * rep + r) * n:(g * rep + r + 1) * n] for r in range(rep)], axis=0)
            s = _dot_nt(qg, kcat[:, g * n:(g + 1) * n])
            if latent:
                s = jnp.concatenate([jnp.where(m_prev, s[:, 0:blk], NEG_INF), s[:, blk:2 * blk],
                                     jnp.where(m_next, s[:, 2 * blk:3 * blk], NEG_INF), s[:, 3 * blk:]], axis=1)
            ss.append(s)
        sinks = [sink_column(g) for g in groups]
        ms = [jnp.maximum(jnp.max(s, axis=-1, keepdims=True), sk) for s, sk in zip(ss, sinks)]
        ps = [jnp.exp(s - m) for s, m in zip(ss, ms)]
        dens = [jnp.sum(p, axis=-1, keepdims=True) + jnp.exp(sk - m) for p, sk, m in zip(ps, sinks, ms)]
        outs = [_dot(p.astype(BF16), vcat[:, g * n:(g + 1) * n]) / den for g, p, den in zip(groups, ps, dens)]
        for g in groups:
            for r in range(rep):
                h = g * rep + r
                o_ref[0, :, h * n:(h + 1) * n] = outs[g][r * blk:(r + 1) * blk].astype(BF16)

    @pl.when(i >= n_ctx_blocks)
    def _():
        attend(True)

    @pl.when(i < n_ctx_blocks)
    def _():
        attend(False)


def _attention(q, k, v, sink, n_ctx):
    B, TT, nq = q.shape
    nkv = k.shape[2]
    blk = ATTN_BLOCK
    nb = TT // blk
    ncb = n_ctx // blk
    kv = lambda f: pl.BlockSpec((1, blk, nkv), lambda b, i: (b, f(i), 0))
    prev = kv(lambda i: jnp.maximum(i - 1, 0))
    cur = kv(lambda i: i)
    nxt = kv(lambda i: jnp.minimum(i + 1, nb - 1))
    cx = pl.BlockSpec((1, n_ctx, nkv), lambda b, i: (b, 0, 0))
    return pl.pallas_call(
        functools.partial(_attn_kernel, n_ctx_blocks=ncb, n_blocks=nb),
        grid=(B, nb),
        in_specs=[pl.BlockSpec(memory_space=pltpu.SMEM),
                  pl.BlockSpec((1, blk, nq), lambda b, i: (b, i, 0)),
                  prev, cur, nxt, cx, prev, cur, nxt, cx],
        out_specs=pl.BlockSpec((1, blk, nq), lambda b, i: (b, i, 0)),
        out_shape=jax.ShapeDtypeStruct((B, TT, nq), BF16),
        compiler_params=_params("parallel", "parallel"),
        name="attention",
    )(sink, q, k, k, k, k, v, v, v, v)


def _lru_chunk_index(s, n_ctx_chunks, n_chunks, reverse):
    if not reverse:
        return s
    return jnp.where(s < n_ctx_chunks, n_ctx_chunks - 1 - s, n_chunks - 1 - (s - n_ctx_chunks))


def _lru_kernel(up_ref, u_ref, un_ref, cw_ref, cb_ref, gw_ref, gb_ref, lam_ref, *rest,
                n_ctx_chunks, n_chunks, reverse):
    if reverse:
        hf_ref, ug_ref, y_ref, a_scr, b_scr, h_scr = rest
    else:
        h_out_ref, a_scr, b_scr, h_scr = rest
    s = pl.program_id(0)
    c = _lru_chunk_index(s, n_ctx_chunks, n_chunks, reverse)
    tc, nb, w = u_ref.shape

    @pl.when(s == 0)
    def _():
        h_scr[...] = jnp.zeros_like(h_scr)

    has_prev = (c != 0) & (c != n_ctx_chunks)
    has_next = (c != n_ctx_chunks - 1) & (c != n_chunks - 1)
    u = u_ref[...]
    ext = jnp.concatenate([jnp.where(has_prev, up_ref[...], 0.0), u, jnp.where(has_next, un_ref[...], 0.0)], axis=0)
    cw = cw_ref[...]
    xc = cb_ref[...] + cw[0:1] * ext[0:tc] + cw[1:2] * ext[1:tc + 1] + cw[2:3] * ext[2:tc + 2] + cw[3:4] * ext[3:tc + 3]

    gates = _dot(xc.reshape(tc * nb, w).astype(BF16), gw_ref[...]).reshape(tc, nb, 2 * w) + gb_ref[...]
    r = _sigmoid(gates[:, :, 0:w])
    ig = _sigmoid(gates[:, :, w:2 * w])
    log_a = (-LRU_C * _softplus(-lam_ref[...])) * r
    a = jnp.exp(log_a)
    a_scr[...] = a
    b_scr[...] = jnp.sqrt(-jnp.tanh(log_a) * (a * a + 1.0)) * (ig * xc)

    def step(t, h):
        tt = tc - 1 - t if reverse else t
        h = a_scr[tt] * h + b_scr[tt]
        b_scr[tt] = h
        return h

    h_scr[...] = lax.fori_loop(0, tc, step, h_scr[...], unroll=8)

    if reverse:
        ug = ug_ref[...]
        gelu = 0.5 * ug * (1.0 + jnp.tanh(0.7978845608028654 * (ug + 0.044715 * ug * ug * ug)))
        y_ref[...] = ((hf_ref[...] + b_scr[...]) * gelu).astype(BF16)
    else:
        h_out_ref[...] = b_scr[...]


def _lru_direction(ux3, conv_w, conv_b, gate_w, gate_b, lam, n_ctx, reverse, h_fwd=None, ug3=None):
    TT, B, W = ux3.shape
    tc = LRU_CHUNK
    nch = TT // tc
    ncc = n_ctx // tc
    cidx = lambda s: _lru_chunk_index(s, ncc, nch, reverse)
    chunk = pl.BlockSpec((tc, B, W), lambda s: (cidx(s), 0, 0))
    in_specs = [pl.BlockSpec((2, B, W), lambda s: (jnp.maximum(cidx(s) * (tc // 2) - 1, 0), 0, 0)),
                chunk,
                pl.BlockSpec((1, B, W), lambda s: (jnp.minimum((cidx(s) + 1) * tc, TT - 1), 0, 0)),
                _resident((4, W)), _resident((1, W)), _resident((W, 2 * W)), _resident((1, 2 * W)),
                _resident((1, W))]
    args = [ux3, ux3, ux3, conv_w, conv_b, gate_w, gate_b, lam]
    if reverse:
        in_specs += [chunk, chunk]
        args += [h_fwd, ug3]
        out_dtype = BF16
    else:
        out_dtype = F32
    return pl.pallas_call(
        functools.partial(_lru_kernel, n_ctx_chunks=ncc, n_chunks=nch, reverse=reverse),
        grid=(nch,),
        in_specs=in_specs,
        out_specs=chunk,
        out_shape=jax.ShapeDtypeStruct((TT, B, W), out_dtype),
        scratch_shapes=[pltpu.VMEM((tc, B, W), F32), pltpu.VMEM((tc, B, W), F32), pltpu.VMEM((B, W), F32)],
        compiler_params=_params("arbitrary"),
        name="lru_rev" if reverse else "lru_fwd",
    )(*args)


GROUP_HEADS = LANES // HEAD_DIM
GROUP = GROUP_HEADS * HEAD_DIM
N_GROUPS = RWKV_HEADS // GROUP_HEADS


def _expand(y, lanehead):
    return jnp.concatenate([jnp.where(lanehead == h, y, jnp.zeros_like(y)) for h in range(GROUP_HEADS)], axis=0)


def _compact(z, lanehead):
    n = HEAD_DIM
    out = jnp.where(lanehead == 0, z[0:n], 0.0)
    for h in range(1, GROUP_HEADS):
        out = out + jnp.where(lanehead == h, z[h * n:(h + 1) * n], 0.0)
    return out


def _chunk_summaries(d, r, v, kn, lw, kd, a, rk, q_out, o_out, g_out, h_out, pc_out):
    C = RWKV_CHUNK
    tm = r.shape[0]
    sgn = 1 - 2 * d
    ri = lax.broadcasted_iota(jnp.int32, (C, C), 0)
    ci = lax.broadcasted_iota(jnp.int32, (C, C), 1)
    tri = ((ri - ci) * sgn >= 0).astype(BF16)
    t = lax.broadcasted_iota(jnp.int32, (C, GROUP), 0)
    lane = lax.broadcasted_iota(jnp.int32, (C, GROUP), 1)
    i = lane & (C - 1)
    eye = t == i
    same16 = (t >> 4) == (i >> 4)
    same32 = (t >> 5) == (i >> 5)
    lanehead = lax.broadcasted_iota(jnp.int32, (1, GROUP), 1) >> 6
    t2 = lax.broadcasted_iota(jnp.int32, (2 * C, GROUP), 0)
    i2 = lax.broadcasted_iota(jnp.int32, (2 * C, GROUP), 1) & (C - 1)
    mask2 = ((t2 & (C - 1)) - i2) * sgn > jnp.where(t2 < C, 0, -1)
    rowhead = lax.broadcasted_iota(jnp.int32, (GROUP, GROUP), 0) >> 6
    colhead = lax.broadcasted_iota(jnp.int32, (GROUP, GROUP), 1) >> 6
    head_ones = (rowhead == colhead).astype(BF16)

    def mm(x, y):
        return _dot(x.astype(BF16), _expand(y.astype(BF16), lanehead))

    def each(f, *lists):
        return [f(*xs) for xs in zip(*lists)]

    n_ch = tm // C
    units = [(slice(ch * C, (ch + 1) * C), slice(g * GROUP, (g + 1) * GROUP), ch)
             for ch in range(n_ch) for g in range(N_GROUPS)]
    l_in_all, l_tot_all = [], []
    for ch in range(n_ch):
        lw_c = lw[ch * C:(ch + 1) * C]
        lw_hi = lw_c.astype(BF16)
        rest = lw_c - lw_hi.astype(F32)
        lw_mid = rest.astype(BF16)
        lw_lo = (rest - lw_mid.astype(F32)).astype(BF16)
        l_in_all.append(_dot(tri, lw_hi) + _dot(tri, lw_mid) + _dot(tri, lw_lo))
        l_tot_all.append(jnp.sum(lw_c, axis=0, keepdims=True))
        pc_out[d, 0, ch * 8:(ch + 1) * 8, :] = jnp.broadcast_to(jnp.exp(l_tot_all[ch]), (8, lw_c.shape[1]))

    a_bar, r_bar, lhs, b_inv, k_inv, b_end, k_end, vs = [], [], [], [], [], [], [], []
    for rows, ls, ch in units:
        kn_u = kn[rows, ls]
        k_u = kd[rows, ls]
        l_in = l_in_all[ch][:, ls]
        beta = kn_u * a[rows, ls]
        p_inv = jnp.exp(-l_in)
        p_end = jnp.exp(l_tot_all[ch][:, ls] - l_in)
        a_bar.append(-kn_u * jnp.exp(l_in - lw[rows, ls]))
        r_bar.append(r[rows, ls] * jnp.exp(l_in))
        lhs.append(jnp.concatenate([a_bar[-1], r_bar[-1]], axis=0).astype(BF16))
        b_inv.append((beta * p_inv).astype(BF16))
        k_inv.append((k_u * p_inv).astype(BF16))
        b_end.append((beta * p_end).astype(BF16))
        k_end.append((k_u * p_end).astype(BF16))
        vs.append(v[rows, ls])

    mb = each(lambda l, y: jnp.where(mask2, _dot_nt(l, _expand(y, lanehead)), 0.0), lhs, b_inv)
    mk = each(lambda l, y: jnp.where(mask2, _dot_nt(l, _expand(y, lanehead)), 0.0), lhs, k_inv)
    m_ab = [m[0:C] for m in mb]
    m_rb = [m[C:2 * C] for m in mb]
    d1 = [jnp.where(same16, m, 0.0) for m in m_ab]
    in32 = [jnp.where(same32, m, 0.0) for m in m_ab]
    d2 = each(mm, d1, d1)
    d4 = each(mm, d2, d2)
    tinv = [jnp.where(eye, 1.0, m) for m in d1]
    tinv = each(lambda t, p: t + mm(t, p), tinv, d2)
    d8 = each(mm, d4, d4)
    tinv = each(lambda t, p: t + mm(t, p), tinv, d4)
    mkv = each(mm, mk, vs)
    x1 = [m[0:C] for m in mkv]
    o_loc = [m[C:2 * C] for m in mkv]
    tinv = each(lambda t, p: t + mm(t, p), tinv, d8)
    te = each(lambda t, a, b: mm(t, a - b), tinv, in32, d1)
    tinv = each(lambda t, e: t + mm(e, t), tinv, te)
    te = each(lambda t, a, b: mm(t, a - b), tinv, m_ab, in32)
    tinv = each(lambda t, e: t + mm(e, t), tinv, te)
    w_hat = each(mm, tinv, a_bar)
    u_hat = each(mm, tinv, x1)
    for (rows, ls, ch), rb, w, rbar in zip(units, m_rb, w_hat, r_bar):
        q_out[d, 0, rows, ls] = (rbar + mm(rb, w)).astype(BF16)
    rkr = _dot_split(jnp.concatenate([r[rows, ls] * kd[rows, ls] * rk[:, ls] for rows, ls, ch in units], axis=0),
                     head_ones)
    for n, ((rows, ls, ch), rb, u, o, v_u) in enumerate(zip(units, m_rb, u_hat, o_loc, vs)):
        o_out[d, 0, rows, ls] = o + mm(rb, u) + rkr[n * C:(n + 1) * C] * v_u
    for (rows, ls, ch), w, be in zip(units, w_hat, b_end):
        g_out[d, 0, rows, ls] = _compact(_dot_tn(w.astype(BF16), be), lanehead).astype(BF16)
    for (rows, ls, ch), v_u, u, ke, be in zip(units, vs, u_hat, k_end, b_end):
        zh = _dot_tn(jnp.concatenate([v_u, u], axis=0).astype(BF16), jnp.concatenate([ke, be], axis=0))
        h_out[d, 0, rows, ls] = _compact(zh, lanehead)


def _rwkv_chunk_kernel(zp_ref, z_ref, zn_ref, mu_ref, kk_ref, ka_ref, wup_ref, w0_ref, aup_ref, a0_ref, gup_ref,
                       hsum_ref, rk_ref, gate_ref, q_out, o_out, g_out, h_out, pc_out,
                       *, n_ctx_blocks, n_blocks):
    i = pl.program_id(1)
    has_prev = (i != 0) & (i != n_ctx_blocks)
    has_next = (i != n_ctx_blocks - 1) & (i != n_blocks - 1)
    z = z_ref[0]
    tm = z.shape[0]
    row = lax.broadcasted_iota(jnp.int32, (tm, 1), 0)
    z_first = jnp.where(has_prev, zp_ref[0, 7:8, :], 0.0)
    z_last = jnp.where(has_next, zn_ref[0, 0:1, :], 0.0)
    z_prev = jnp.where(row == 0, z_first, pltpu.roll(z, 1, axis=0))
    z_next = jnp.where(row == tm - 1, z_last, pltpu.roll(z, tm - 1, axis=0))
    mu = mu_ref[...]
    zs = z + mu[0:1] * (z_prev - z) + mu[1:2] * (z_next - z)
    w = MIX
    r = zs[:, 0:w]
    k = zs[:, w:2 * w]
    v = zs[:, 2 * w:3 * w]
    wd = jnp.tanh(zs[:, 3 * w:3 * w + LANES]).astype(BF16)
    ad = zs[:, 3 * w + LANES:3 * w + 2 * LANES].astype(BF16)
    gd = _sigmoid(zs[:, 3 * w + 2 * LANES:3 * w + 3 * LANES]).astype(BF16)
    kk = k * kk_ref[...]
    norm = jnp.sqrt(_dot_split(kk * kk, hsum_ref[...]))
    kn = kk / jnp.maximum(norm, 1e-12)
    gate_ref[0] = _dot(gd, gup_ref[...])
    for d in range(2):
        logw = w0_ref[d:d + 1] + _dot(wd, wup_ref[d])
        lw = -jnp.exp(-_softplus(-logw) - 0.5)
        a = _sigmoid(a0_ref[d:d + 1] + _dot(ad, aup_ref[d]))
        kd = k * (1.0 + (a - 1.0) * ka_ref[...])
        _chunk_summaries(d, r, v, kn, lw, kd, a, rk_ref[...], q_out, o_out, g_out, h_out, pc_out)


def _rwkv_chunks(zr, mu, k_k, k_a, w_up, w0, a_up, a0, g_up, hsum, r_k, n_ctx):
    B, TT, NR = zr.shape
    tm = TOKEN_BLOCK
    C = RWKV_CHUNK
    assert C == HEAD_DIM and tm % C == 0
    nb = TT // tm
    ncb = n_ctx // tm
    rows = tm // 8
    tok = pl.BlockSpec((1, tm, MIX), lambda b, i: (b, i, 0))
    two = pl.BlockSpec((2, 1, tm, MIX), lambda b, i: (0, b, i, 0))
    pc = pl.BlockSpec((2, 1, 8 * (tm // C), MIX), lambda b, i: (0, b, i, 0))
    shape = lambda dt: jax.ShapeDtypeStruct((2, B, TT, MIX), dt)
    return pl.pallas_call(
        functools.partial(_rwkv_chunk_kernel, n_ctx_blocks=ncb, n_blocks=nb),
        grid=(B, nb),
        in_specs=[pl.BlockSpec((1, 8, NR), lambda b, i: (b, jnp.maximum(i * rows - 1, 0), 0)),
                  pl.BlockSpec((1, tm, NR), lambda b, i: (b, i, 0)),
                  pl.BlockSpec((1, 8, NR), lambda b, i: (b, jnp.minimum((i + 1) * rows, TT // 8 - 1), 0)),
                  _resident((2, NR)), _resident((1, MIX)), _resident((1, MIX)),
                  _resident((2, LANES, MIX)), _resident((2, MIX)), _resident((2, LANES, MIX)), _resident((2, MIX)),
                  _resident((LANES, MIX)), _resident((MIX, MIX)), _resident((1, MIX))],
        out_specs=[tok, two, two, two, two, pc],
        out_shape=[jax.ShapeDtypeStruct((B, TT, MIX), F32),
                   shape(BF16), shape(F32), shape(BF16), shape(F32),
                   jax.ShapeDtypeStruct((2, B, 8 * (TT // C), MIX), F32)],
        compiler_params=_params("parallel", "parallel"),
        name="rwkv_chunks",
    )(zr, zr, zr, mu, k_k, k_a, w_up, w0, a_up, a0, g_up, hsum, r_k)


def _rwkv_state_kernel(qf_ref, of_ref, gf_ref, hf_ref, pf_ref, qr_ref, or_ref, gr_ref, hr_ref, pr_ref,
                       outf_ref, outr_ref, s_scr):
    s = pl.program_id(0)

    @pl.when(s == 0)
    def _():
        s_scr[...] = jnp.zeros_like(s_scr)

    lanehead = lax.broadcasted_iota(jnp.int32, (1, GROUP), 1) >> 6
    nb = outf_ref.shape[0]
    dirs = ((qf_ref, of_ref, gf_ref, hf_ref, pf_ref, outf_ref), (qr_ref, or_ref, gr_ref, hr_ref, pr_ref, outr_ref))
    for d, (q_ref, o_ref, g_ref, h_ref, p_ref, out_ref) in enumerate(dirs):
        for b in range(nb):
            for g in range(N_GROUPS):
                ls = slice(g * GROUP, (g + 1) * GROUP)
                st = s_scr[d, b, g]
                sb = st.astype(BF16)
                out_ref[b, :, ls] = _dot_nt(q_ref[0, b, :, ls], _expand(sb, lanehead)) + o_ref[0, b, :, ls]
                s_scr[d, b, g] = (st * p_ref[0, b, 0:1, ls] + _dot(sb, _expand(g_ref[0, b, :, ls], lanehead))
                                  + h_ref[0, b, :, ls])


def _rwkv_state(q, o0, gm, hm, pc, n_ctx):
    _, B, TT, W = q.shape
    C = RWKV_CHUNK
    nch = TT // C
    ncc = n_ctx // C
    fwd = lambda s: s
    rev = lambda s: jnp.where(s < ncc, ncc - 1 - s, nch - 1 - (s - ncc))
    spec = lambda d, f, rows: pl.BlockSpec((1, B, rows, W), lambda s: (d, 0, f(s), 0))
    out = lambda f: pl.BlockSpec((B, C, W), lambda s: (0, f(s), 0))
    shape = jax.ShapeDtypeStruct((B, TT, W), F32)
    return pl.pallas_call(
        _rwkv_state_kernel,
        grid=(nch,),
        in_specs=[spec(0, fwd, C)] * 4 + [spec(0, fwd, 8)] + [spec(1, rev, C)] * 4 + [spec(1, rev, 8)],
        out_specs=[out(fwd), out(rev)],
        out_shape=[shape, shape],
        scratch_shapes=[pltpu.VMEM((2, B, N_GROUPS, HEAD_DIM, GROUP), F32)],
        compiler_params=_params("arbitrary"),
        name="rwkv_state",
    )(q, o0, gm, hm, pc, q, o0, gm, hm, pc)


def _merge_kernel(x_ref, mod_ref, ya_ref, yl_ref, of_ref, or_ref, gate_ref, lng_ref, lnb_ref, havg_ref, zg_ref,
                  proj_ref, wo_ref, out_ref):
    D = x_ref.shape[2]
    o = of_ref[0] + or_ref[0]
    mean = _dot_split(o, havg_ref[...])
    cen = o - mean
    var = _dot_split(cen * cen, havg_ref[...])
    yr = ((cen * lax.rsqrt(var + RWKV_LN_EPS)) * lng_ref[...] + lnb_ref[...]) * gate_ref[0]
    ys = (ya_ref[0], yl_ref[...], yr.astype(BF16))
    acc = None
    for b in range(3):
        term = _sigmoid(zg_ref[0, :, b * D:(b + 1) * D]) * _dot(ys[b], proj_ref[b])
        acc = term if acc is None else acc + term
    out_ref[0] = x_ref[0] + mod_ref[0, 0][2:3] * _dot(acc.astype(BF16), wo_ref[...])


def _merge(x, mod3, ya, yl, o_fwd, o_rev, gate, ln_g, ln_b, havg, zg, proj, w_o, n_ctx_blocks):
    B, TT, D = x.shape
    tm = TOKEN_BLOCK
    kind = lambda i: jnp.where(i < n_ctx_blocks, 0, 1)
    tok = lambda w: pl.BlockSpec((1, tm, w), lambda b, i: (b, i, 0))
    return pl.pallas_call(
        _merge_kernel,
        grid=(B, TT // tm),
        in_specs=[tok(D),
                  pl.BlockSpec((1, 1, 3, D), lambda b, i: (b, kind(i), 0, 0)),
                  tok(MIX),
                  pl.BlockSpec((tm, MIX), lambda b, i: (i, b)),
                  tok(MIX), tok(MIX), tok(MIX),
                  _resident((1, MIX)), _resident((1, MIX)), _resident((MIX, MIX)),
                  tok(3 * D),
                  _resident((3, MIX, D)), _resident((D, D))],
        out_specs=tok(D),
        out_shape=jax.ShapeDtypeStruct((B, TT, D), F32),
        compiler_params=_params("parallel", "parallel"),
        name="merge",
    )(x, mod3, ya, yl, o_fwd, o_rev, gate, ln_g, ln_b, havg, zg, proj, w_o)


def _rope_tables(n_ctx, n_tok):
    n_freq = HEAD_DIM // 4
    t = jnp.arange(n_tok)
    row = (t // GRID_W).astype(F32)
    col = (t % GRID_W).astype(F32)
    inv = ROPE_BASE ** (-jnp.arange(n_freq, dtype=F32) / n_freq)
    ang_r = row[:, None] * inv
    ang_c = col[:, None] * inv
    cos = jnp.concatenate([jnp.cos(ang_r)] * 2 + [jnp.cos(ang_c)] * 2, axis=-1)
    sin = jnp.concatenate([-jnp.sin(ang_r), jnp.sin(ang_r), -jnp.sin(ang_c), jnp.sin(ang_c)], axis=-1)
    cos = jnp.concatenate([jnp.ones((n_ctx, HEAD_DIM), F32), cos], axis=0)
    sin = jnp.concatenate([jnp.zeros((n_ctx, HEAD_DIM), F32), sin], axis=0)
    return jnp.tile(cos, (1, LANES // HEAD_DIM)), jnp.tile(sin, (1, LANES // HEAD_DIM))


def _block_diag(w):
    n, r, c = w.shape
    eye = jnp.eye(n, dtype=w.dtype)
    return (eye[:, None, :, None] * w[:, :, None, :]).reshape(n * r, n * c)


def _pad_rows(w, rank):
    z = jnp.zeros_like(w[0])
    return jnp.stack([jnp.concatenate([w[0], z], axis=0), jnp.concatenate([z, w[1]], axis=0)])


def kernel(x, c, ctx, c_ctx, ada_w, ada_b, norm_g, ffn_w_gu, ffn_w_d, w_in, attn_q_gain, attn_k_gain, attn_sink, lru_conv_w, lru_conv_b, lru_gate_w, lru_gate_b, lru_lambda, rwkv_mu, rwkv_w_up, rwkv_w0, rwkv_a_up, rwkv_a0, rwkv_g_up, rwkv_k_k, rwkv_k_a, rwkv_r_k, rwkv_ln_g, rwkv_ln_b, branch_proj, w_out):
    B, T, D = x.shape
    n_ctx = ctx.shape[1]
    L = ada_w.shape[0]
    TT = n_ctx + T
    assert n_ctx % TOKEN_BLOCK == 0 and T % TOKEN_BLOCK == 0 and T % GRID_W == 0
    ncb = n_ctx // TOKEN_BLOCK
    d_ff = ffn_w_d.shape[2]
    n_rwkv = rwkv_mu.shape[2]
    n_gate = 3 * D

    n_rows = -(-(B + 1) // 8) * 8
    c_rows = jnp.zeros((n_rows, D), F32).at[:B].set(c).at[B].set(c_ctx)
    ada = _ada_vectors(c_rows, ada_w, ada_b)
    mod_lat = ada[:, :B].reshape(L, B, 1, N_ADA, D)
    mod_ctx = jnp.broadcast_to(ada[:, B].reshape(L, 1, 1, N_ADA, D), (L, B, 1, N_ADA, D))
    mods = jnp.concatenate([mod_ctx, mod_lat], axis=2)

    cos_t, sin_t = _rope_tables(n_ctx, T)
    hsum = jnp.kron(jnp.eye(RWKV_HEADS, dtype=BF16), jnp.ones((HEAD_DIM, HEAD_DIM), BF16))
    havg = hsum / HEAD_DIM
    tile2 = lambda g: jnp.tile(g, LANES // HEAD_DIM).reshape(1, LANES)

    xs = x
    for l in range(L):
        wg = ffn_w_gu[l, :, :, :d_ff].astype(BF16)
        wu = ffn_w_gu[l, :, :, d_ff:].astype(BF16)
        wd = ffn_w_d[l].astype(BF16)
        xs = _ffn(xs, mods[l, :, :, 0:3], norm_g[l, 0].reshape(1, D), wg[0], wu[0], wd[0], ncb,
                  ctx=ctx if l == 0 else None)

        q, k, v, ux, ug, zr, zg = _inproj(
            xs, mods[l, :, :, 3:6], norm_g[l, 1].reshape(1, D), w_in[l].astype(BF16),
            tile2(attn_q_gain[l]), tile2(attn_k_gain[l]), cos_t, sin_t, ncb, n_rwkv, n_gate)

        ya = _attention(q, k, v, attn_sink[l], n_ctx)

        ux3 = ux.reshape(TT, B, MIX)
        ug3 = ug.reshape(TT, B, MIX)
        lru_args = lambda d: (
            lru_conv_w[l], lru_conv_b[l].reshape(1, MIX),
            jnp.concatenate([_block_diag(lru_gate_w[l, d, 0]), _block_diag(lru_gate_w[l, d, 1])], axis=1).astype(BF16),
            lru_gate_b[l, d].reshape(1, 2 * MIX), lru_lambda[l, d].reshape(1, MIX))
        h_fwd = _lru_direction(ux3, *lru_args(0), n_ctx, False)
        yl = _lru_direction(ux3, *lru_args(1), n_ctx, True, h_fwd, ug3).reshape(TT, B * MIX)

        gate, *summaries = _rwkv_chunks(
            zr, rwkv_mu[l], rwkv_k_k[l].reshape(1, MIX), rwkv_k_a[l].reshape(1, MIX),
            _pad_rows(rwkv_w_up[l], rwkv_w_up.shape[2]).astype(BF16), rwkv_w0[l],
            _pad_rows(rwkv_a_up[l], rwkv_a_up.shape[2]).astype(BF16), rwkv_a0[l],
            rwkv_g_up[l].astype(BF16), hsum, rwkv_r_k[l].reshape(1, MIX), n_ctx)
        o_fwd, o_rev = _rwkv_state(*summaries, n_ctx)

        xs = _merge(xs, mods[l, :, :, 3:6], ya, yl, o_fwd, o_rev, gate,
                    rwkv_ln_g[l].reshape(1, MIX), rwkv_ln_b[l].reshape(1, MIX), havg, zg,
                    branch_proj[l].astype(BF16), w_out[l].astype(BF16), ncb)

        xs = _ffn(xs, mods[l, :, :, 6:9], norm_g[l, 2].reshape(1, D), wg[1], wu[1], wd[1], ncb,
                  latent_only=(l == L - 1))
    return xs
```

```python
import functools

import jax
import jax.numpy as jnp
from jax import lax
from jax.experimental import pallas as pl
from jax.experimental.pallas import tpu as pltpu

F32 = jnp.float32
BF16 = jnp.bfloat16
HIGHEST = lax.Precision.HIGHEST

N_ADA = 9
NORM_EPS = 1e-6
GRID_W = 64
HEAD_DIM = 64
ATTN_HEADS = 8
ATTN_KV_HEADS = 2
ATTN_BLOCK = 128
ROPE_BASE = 10000.0
NEG_INF = -1e30
LRU_C = 8.0
LRU_BLOCKS = 8
RWKV_HEADS = 8
RWKV_LN_EPS = 64e-5
MIX = 512
LANES = 128
TOKEN_BLOCK = 256
LRU_CHUNK = 128
RWKV_CHUNK = 64
VMEM_LIMIT = 56 * 1024 * 1024


def _params(*sem):
    return pltpu.CompilerParams(dimension_semantics=sem, vmem_limit_bytes=VMEM_LIMIT)


def _resident(shape):
    nd = len(shape)
    return pl.BlockSpec(shape, lambda *_: (0,) * nd, pipeline_mode=pl.Buffered(1))


def _layer_slice(shape, *index):
    nd = len(shape)
    return pl.BlockSpec((None,) * len(index) + tuple(shape), lambda *_: tuple(index) + (0,) * nd,
                        pipeline_mode=pl.Buffered(1))


def _dot(a, b):
    return jnp.dot(a, b, preferred_element_type=F32)


def _dot_f32(a, b):
    return jnp.dot(a, b, preferred_element_type=F32, precision=HIGHEST)


def _dot_split(x, w):
    x_hi = x.astype(BF16)
    x_lo = (x - x_hi.astype(F32)).astype(BF16)
    return _dot(x_hi, w) + _dot(x_lo, w)


def _dot_nt(a, b, precision=None):
    return lax.dot_general(a, b, (((1,), (1,)), ((), ())), preferred_element_type=F32, precision=precision)


def _dot_tn(a, b, precision=None):
    return lax.dot_general(a, b, (((0,), (0,)), ((), ())), preferred_element_type=F32, precision=precision)


def _sigmoid(x):
    return 0.5 * jnp.tanh(0.5 * x) + 0.5


def _softplus(x):
    return jnp.maximum(x, 0.0) + jnp.log(1.0 + jnp.exp(-jnp.abs(x)))


def _modulated_norm(x, g, shift, scale):
    y = x * lax.rsqrt(jnp.mean(x * x, axis=-1, keepdims=True) + NORM_EPS)
    return (y * g) * (1.0 + scale) + shift


def _ada_kernel(s_ref, w_ref, b_ref, o_ref):
    s = s_ref[...]
    s = (s * _sigmoid(s)).astype(BF16)
    o_ref[0] = _dot(s, w_ref[0].astype(BF16)) + b_ref[0]


def _ada_vectors(c_rows, ada_w, ada_b):
    L, D, N = ada_w.shape
    R = c_rows.shape[0]
    tn = 1024
    return pl.pallas_call(
        _ada_kernel,
        grid=(L, N // tn),
        in_specs=[pl.BlockSpec((R, D), lambda l, j: (0, 0)),
                  pl.BlockSpec((1, D, tn), lambda l, j: (l, 0, j)),
                  pl.BlockSpec((1, 1, tn), lambda l, j: (l, 0, j))],
        out_specs=pl.BlockSpec((1, R, tn), lambda l, j: (l, 0, j)),
        out_shape=jax.ShapeDtypeStruct((L, R, N), F32),
        compiler_params=_params("parallel", "parallel"),
        name="ada_vectors",
    )(c_rows, ada_w, ada_b.reshape(L, 1, N))


def _ffn_half_step(x, mod, g, wgu_ref, wd_ref):
    d_ff = wd_ref.shape[0]
    h = _modulated_norm(x, g, mod[0:1], mod[1:2]).astype(BF16)
    gt = _dot(h, wgu_ref[:, 0:d_ff])
    up = _dot(h, wgu_ref[:, d_ff:2 * d_ff])
    act = (gt * _sigmoid(gt) * up).astype(BF16)
    return x + 0.5 * mod[2:3] * _dot(act, wd_ref[...])


def _ffn_kernel(*refs, n_ctx_blocks, split_input):
    if split_input:
        c_ref, x_ref, mod_ref, g_ref, wgu_ref, wd_ref, o_ref = refs
        x = jnp.where(pl.program_id(1) < n_ctx_blocks, c_ref[0], x_ref[0])
    else:
        x_ref, mod_ref, g_ref, wgu_ref, wd_ref, o_ref = refs
        x = x_ref[0]
    o_ref[0] = _ffn_half_step(x, mod_ref[0, 0], g_ref[...], wgu_ref, wd_ref)


def _ffn(x, mod3, g, w_gu, w_d, layer, n_ctx_blocks, ctx=None):
    B, _, D = x.shape
    F = w_d.shape[2]
    tm = TOKEN_BLOCK
    kind = lambda i: jnp.where(i < n_ctx_blocks, 0, 1)
    if ctx is None:
        n_out = x.shape[1] // tm
        x_specs = [pl.BlockSpec((1, tm, D), lambda b, i: (b, i, 0))]
        xs = (x,)
    else:
        n_out = n_ctx_blocks + x.shape[1] // tm
        x_specs = [pl.BlockSpec((1, tm, D), lambda b, i: (b, jnp.minimum(i, n_ctx_blocks - 1), 0)),
                   pl.BlockSpec((1, tm, D), lambda b, i: (b, jnp.maximum(i - n_ctx_blocks, 0), 0))]
        xs = (ctx, x)
    return pl.pallas_call(
        functools.partial(_ffn_kernel, n_ctx_blocks=n_ctx_blocks, split_input=ctx is not None),
        grid=(B, n_out),
        in_specs=x_specs + [pl.BlockSpec((1, 1, 3, D), lambda b, i: (b, kind(i), 0, 0)),
                            _resident((1, D)), _layer_slice((D, 2 * F), layer, 0), _layer_slice((F, D), layer, 0)],
        out_specs=pl.BlockSpec((1, tm, D), lambda b, i: (b, i, 0)),
        out_shape=jax.ShapeDtypeStruct((B, n_out * tm, D), F32),
        compiler_params=_params("parallel", "parallel"),
        name="ffn",
    )(*xs, mod3, g, w_gu, w_d)


def _head_rms_rope(xs, gain, cos, sin, lane):
    sq = xs * xs
    lo = lane < HEAD_DIM
    s_lo = jnp.sum(jnp.where(lo, sq, 0.0), axis=-1, keepdims=True)
    s_hi = jnp.sum(jnp.where(lo, 0.0, sq), axis=-1, keepdims=True)
    ms = jnp.where(lo, s_lo, s_hi) * (1.0 / HEAD_DIM)
    y = xs * lax.rsqrt(ms + NORM_EPS) * gain
    first = (lane % 32) < 16
    partner = jnp.where(first, pltpu.roll(y, LANES - 16, axis=1), pltpu.roll(y, 16, axis=1))
    return y * cos + partner * sin


def _inproj_kernel(x_ref, mod_ref, g_ref, w_ref, qg_ref, kg_ref, cos_ref, sin_ref,
                   q_ref, k_ref, v_ref, ux_ref, ug_ref, zr_ref, zg_ref):
    x = x_ref[0]
    mod = mod_ref[0, 0]
    h = _modulated_norm(x, g_ref[...], mod[0:1], mod[1:2]).astype(BF16)
    nq = ATTN_HEADS * HEAD_DIM
    nkv = ATTN_KV_HEADS * HEAD_DIM
    lane = lax.broadcasted_iota(jnp.int32, (1, LANES), 1)
    cos = cos_ref[...]
    sin = sin_ref[...]
    za = _dot(h, w_ref[:, 0:nq + 2 * nkv])
    for s in range(nq // LANES):
        qs = _head_rms_rope(za[:, s * LANES:(s + 1) * LANES], qg_ref[...], cos, sin, lane)
        q_ref[0, :, s * LANES:(s + 1) * LANES] = (qs * HEAD_DIM ** -0.5).astype(BF16)
    k_ref[0] = _head_rms_rope(za[:, nq:nq + nkv], kg_ref[...], cos, sin, lane).astype(BF16)
    v_ref[0] = za[:, nq + nkv:nq + 2 * nkv].astype(BF16)
    c0 = nq + 2 * nkv
    zl = _dot(h, w_ref[:, c0:c0 + 2 * MIX])
    ux_ref[...] = zl[:, 0:MIX]
    ug_ref[...] = zl[:, MIX:2 * MIX]
    c1 = c0 + 2 * MIX
    nr = zr_ref.shape[2]
    zr_ref[0] = _dot(h, w_ref[:, c1:c1 + nr])
    c2 = c1 + nr
    zg_ref[0] = _dot(h, w_ref[:, c2:c2 + zg_ref.shape[2]])


def _inproj(x, mod3, g, w_in, layer, q_gain, k_gain, cos_t, sin_t, n_ctx_blocks, n_rwkv, n_gate):
    B, TT, D = x.shape
    tm = TOKEN_BLOCK
    nq = ATTN_HEADS * HEAD_DIM
    nkv = ATTN_KV_HEADS * HEAD_DIM
    kind = lambda i: jnp.where(i < n_ctx_blocks, 0, 1)
    tok = lambda w: pl.BlockSpec((1, tm, w), lambda b, i: (b, i, 0))
    tmajor = pl.BlockSpec((tm, MIX), lambda b, i: (i, b))
    return pl.pallas_call(
        _inproj_kernel,
        grid=(B, TT // tm),
        in_specs=[tok(D),
                  pl.BlockSpec((1, 1, 3, D), lambda b, i: (b, kind(i), 0, 0)),
                  _resident((1, D)), _layer_slice(w_in.shape[1:], layer), _resident((1, LANES)), _resident((1, LANES)),
                  pl.BlockSpec((tm, LANES), lambda b, i: (i, 0)),
                  pl.BlockSpec((tm, LANES), lambda b, i: (i, 0))],
        out_specs=[tok(nq), tok(nkv), tok(nkv), tmajor, tmajor, tok(n_rwkv), tok(n_gate)],
        out_shape=[jax.ShapeDtypeStruct((B, TT, nq), BF16),
                   jax.ShapeDtypeStruct((B, TT, nkv), BF16),
                   jax.ShapeDtypeStruct((B, TT, nkv), BF16),
                   jax.ShapeDtypeStruct((TT, B * MIX), F32),
                   jax.ShapeDtypeStruct((TT, B * MIX), F32),
                   jax.ShapeDtypeStruct((B, TT, n_rwkv), F32),
                   jax.ShapeDtypeStruct((B, TT, n_gate), F32)],
        compiler_params=_params("parallel", "parallel"),
        name="inproj",
    )(x, mod3, g, w_in, q_gain, k_gain, cos_t, sin_t)


def _attn_kernel(sink_ref, q_ref, kp_ref, kc_ref, kn_ref, kx_ref, vp_ref, vc_ref, vn_ref, vx_ref, o_ref,
                 *, n_ctx_blocks, n_blocks):
    i = pl.program_id(1)
    blk = ATTN_BLOCK
    n = HEAD_DIM
    rep = ATTN_HEADS // ATTN_KV_HEADS
    rows = rep * blk
    row = lax.broadcasted_iota(jnp.int32, (rows, 1), 0)

    def sink_column(g):
        col = jnp.full((rows, 1), sink_ref[g * rep], F32)
        for r in range(1, rep):
            col = jnp.where(row >= r * blk, sink_ref[g * rep + r], col)
        return col

    def attend(latent):
        if latent:
            kcat = jnp.concatenate([kp_ref[0], kc_ref[0], kn_ref[0], kx_ref[0]], axis=0)
            vcat = jnp.concatenate([vp_ref[0], vc_ref[0], vn_ref[0], vx_ref[0]], axis=0)
            far = 4 * blk
            off_prev = jnp.where(i >= n_ctx_blocks + 1, 0, far)
            off_next = jnp.where(i + 1 <= n_blocks - 1, 0, far)
            qi = lax.broadcasted_iota(jnp.int32, (rows, blk), 0) & (blk - 1)
            ki = lax.broadcasted_iota(jnp.int32, (rows, blk), 1)
            m_prev = ki >= qi + off_prev
            m_next = ki + off_next <= qi
        else:
            kcat = kx_ref[0]
            vcat = vx_ref[0]
        groups = range(ATTN_KV_HEADS)
        ss = []
        for g in groups:
            qg = jnp.concatenate([q_ref[0, :, (g * rep + r) * n:(g * rep + r + 1) * n] for r in range(rep)], axis=0)
            s = _dot_nt(qg, kcat[:, g * n:(g + 1) * n])
            if latent:
                s = jnp.concatenate([jnp.where(m_prev, s[:, 0:blk], NEG_INF), s[:, blk:2 * blk],
                                     jnp.where(m_next, s[:, 2 * blk:3 * blk], NEG_INF), s[:, 3 * blk:]], axis=1)
            ss.append(s)
        sinks = [sink_column(g) for g in groups]
        ms = [jnp.maximum(jnp.max(s, axis=-1, keepdims=True), sk) for s, sk in zip(ss, sinks)]
        ps = [jnp.exp(s - m) for s, m in zip(ss, ms)]
        dens = [jnp.sum(p, axis=-1, keepdims=True) + jnp.exp(sk - m) for p, sk, m in zip(ps, sinks, ms)]
        outs = [_dot(p.astype(BF16), vcat[:, g * n:(g + 1) * n]) / den for g, p, den in zip(groups, ps, dens)]
        for g in groups:
            for r in range(rep):
                h = g * rep + r
                o_ref[0, :, h * n:(h + 1) * n] = outs[g][r * blk:(r + 1) * blk].astype(BF16)

    @pl.when(i >= n_ctx_blocks)
    def _():
        attend(True)

    @pl.when(i < n_ctx_blocks)
    def _():
        attend(False)


def _attention(q, k, v, sink, n_ctx):
    B, TT, nq = q.shape
    nkv = k.shape[2]
    blk = ATTN_BLOCK
    nb = TT // blk
    ncb = n_ctx // blk
    kv = lambda f: pl.BlockSpec((1, blk, nkv), lambda b, i: (b, f(i), 0))
    prev = kv(lambda i: jnp.maximum(i - 1, 0))
    cur = kv(lambda i: i)
    nxt = kv(lambda i: jnp.minimum(i + 1, nb - 1))
    cx = pl.BlockSpec((1, n_ctx, nkv), lambda b, i: (b, 0, 0))
    return pl.pallas_call(
        functools.partial(_attn_kernel, n_ctx_blocks=ncb, n_blocks=nb),
        grid=(B, nb),
        in_specs=[pl.BlockSpec(memory_space=pltpu.SMEM),
                  pl.BlockSpec((1, blk, nq), lambda b, i: (b, i, 0)),
                  prev, cur, nxt, cx, prev, cur, nxt, cx],
        out_specs=pl.BlockSpec((1, blk, nq), lambda b, i: (b, i, 0)),
        out_shape=jax.ShapeDtypeStruct((B, TT, nq), BF16),
        compiler_params=_params("parallel", "parallel"),
        name="attention",
    )(sink, q, k, k, k, k, v, v, v, v)


def _lru_chunk_index(s, n_ctx_chunks, n_chunks, reverse):
    if not reverse:
        return s
    return jnp.where(s < n_ctx_chunks, n_ctx_chunks - 1 - s, n_chunks - 1 - (s - n_ctx_chunks))


def _lru_kernel(up_ref, u_ref, un_ref, cw_ref, cb_ref, gw_ref, gb_ref, lam_ref, *rest,
                n_ctx_chunks, n_chunks, reverse):
    if reverse:
        hf_ref, ug_ref, y_ref, a_scr, b_scr, h_scr = rest
    else:
        h_out_ref, a_scr, b_scr, h_scr = rest
    s = pl.program_id(0)
    c = _lru_chunk_index(s, n_ctx_chunks, n_chunks, reverse)
    tc, nb, w = u_ref.shape

    @pl.when(s == 0)
    def _():
        h_scr[...] = jnp.zeros_like(h_scr)

    has_prev = (c != 0) & (c != n_ctx_chunks)
    has_next = (c != n_ctx_chunks - 1) & (c != n_chunks - 1)
    u = u_ref[...]
    ext = jnp.concatenate([jnp.where(has_prev, up_ref[...], 0.0), u, jnp.where(has_next, un_ref[...], 0.0)], axis=0)
    cw = cw_ref[...]
    xc = cb_ref[...] + cw[0:1] * ext[0:tc] + cw[1:2] * ext[1:tc + 1] + cw[2:3] * ext[2:tc + 2] + cw[3:4] * ext[3:tc + 3]

    gates = _dot(xc.reshape(tc * nb, w).astype(BF16), gw_ref[...]).reshape(tc, nb, 2 * w) + gb_ref[...]
    r = _sigmoid(gates[:, :, 0:w])
    ig = _sigmoid(gates[:, :, w:2 * w])
    log_a = (-LRU_C * _softplus(-lam_ref[...])) * r
    a = jnp.exp(log_a)
    a_scr[...] = a
    b_scr[...] = jnp.sqrt(-jnp.tanh(log_a) * (a * a + 1.0)) * (ig * xc)

    def step(t, h):
        tt = tc - 1 - t if reverse else t
        h = a_scr[tt] * h + b_scr[tt]
        b_scr[tt] = h
        return h

    h_scr[...] = lax.fori_loop(0, tc, step, h_scr[...], unroll=8)

    if reverse:
        ug = ug_ref[...]
        gelu = 0.5 * ug * (1.0 + jnp.tanh(0.7978845608028654 * (ug + 0.044715 * ug * ug * ug)))
        y_ref[...] = ((hf_ref[...] + b_scr[...]) * gelu).astype(BF16)
    else:
        h_out_ref[...] = b_scr[...]


def _lru_direction(ux3, conv_w, conv_b, gate_w, gate_b, lam, n_ctx, reverse, h_fwd=None, ug3=None):
    TT, B, W = ux3.shape
    tc = LRU_CHUNK
    nch = TT // tc
    ncc = n_ctx // tc
    cidx = lambda s: _lru_chunk_index(s, ncc, nch, reverse)
    chunk = pl.BlockSpec((tc, B, W), lambda s: (cidx(s), 0, 0))
    in_specs = [pl.BlockSpec((2, B, W), lambda s: (jnp.maximum(cidx(s) * (tc // 2) - 1, 0), 0, 0)),
                chunk,
                pl.BlockSpec((1, B, W), lambda s: (jnp.minimum((cidx(s) + 1) * tc, TT - 1), 0, 0)),
                _resident((4, W)), _resident((1, W)), _resident((W, 2 * W)), _resident((1, 2 * W)),
                _resident((1, W))]
    args = [ux3, ux3, ux3, conv_w, conv_b, gate_w, gate_b, lam]
    if reverse:
        in_specs += [chunk, chunk]
        args += [h_fwd, ug3]
        out_dtype = BF16
    else:
        out_dtype = F32
    return pl.pallas_call(
        functools.partial(_lru_kernel, n_ctx_chunks=ncc, n_chunks=nch, reverse=reverse),
        grid=(nch,),
        in_specs=in_specs,
        out_specs=chunk,
        out_shape=jax.ShapeDtypeStruct((TT, B, W), out_dtype),
        scratch_shapes=[pltpu.VMEM((tc, B, W), F32), pltpu.VMEM((tc, B, W), F32), pltpu.VMEM((B, W), F32)],
        compiler_params=_params("arbitrary"),
        name="lru_rev" if reverse else "lru_fwd",
    )(*args)


GROUP_HEADS = LANES // HEAD_DIM
GROUP = GROUP_HEADS * HEAD_DIM
N_GROUPS = RWKV_HEADS // GROUP_HEADS


def _expand(y, lanehead):
    return jnp.concatenate([jnp.where(lanehead == h, y, jnp.zeros_like(y)) for h in range(GROUP_HEADS)], axis=0)


def _compact(z, lanehead):
    n = HEAD_DIM
    out = jnp.where(lanehead == 0, z[0:n], 0.0)
    for h in range(1, GROUP_HEADS):
        out = out + jnp.where(lanehead == h, z[h * n:(h + 1) * n], 0.0)
    return out


def _chunk_summaries(d, r, v, kn, lw, kd, a, rk, q_out, o_out, g_out, h_out, pc_out):
    C = RWKV_CHUNK
    tm = r.shape[0]
    sgn = 1 - 2 * d
    ri = lax.broadcasted_iota(jnp.int32, (C, C), 0)
    ci = lax.broadcasted_iota(jnp.int32, (C, C), 1)
    tri = ((ri - ci) * sgn >= 0).astype(BF16)
    t = lax.broadcasted_iota(jnp.int32, (C, GROUP), 0)
    lane = lax.broadcasted_iota(jnp.int32, (C, GROUP), 1)
    i = lane & (C - 1)
    eye = t == i
    same16 = (t >> 4) == (i >> 4)
    same32 = (t >> 5) == (i >> 5)
    lanehead = lax.broadcasted_iota(jnp.int32, (1, GROUP), 1) >> 6
    t2 = lax.broadcasted_iota(jnp.int32, (2 * C, GROUP), 0)
    i2 = lax.broadcasted_iota(jnp.int32, (2 * C, GROUP), 1) & (C - 1)
    mask2 = ((t2 & (C - 1)) - i2) * sgn > jnp.where(t2 < C, 0, -1)
    rowhead = lax.broadcasted_iota(jnp.int32, (GROUP, GROUP), 0) >> 6
    colhead = lax.broadcasted_iota(jnp.int32, (GROUP, GROUP), 1) >> 6
    head_ones = (rowhead == colhead).astype(BF16)

    def mm(x, y):
        return _dot(x.astype(BF16), _expand(y.astype(BF16), lanehead))

    def each(f, *lists):
        return [f(*xs) for xs in zip(*lists)]

    n_ch = tm // C
    units = [(slice(ch * C, (ch + 1) * C), slice(g * GROUP, (g + 1) * GROUP), ch)
             for ch in range(n_ch) for g in range(N_GROUPS)]
    l_in_all, l_tot_all = [], []
    for ch in range(n_ch):
        lw_c = lw[ch * C:(ch + 1) * C]
        lw_hi = lw_c.astype(BF16)
        rest = lw_c - lw_hi.astype(F32)
        lw_mid = rest.astype(BF16)
        lw_lo = (rest - lw_mid.astype(F32)).astype(BF16)
        l_in_all.append(_dot(tri, lw_hi) + _dot(tri, lw_mid) + _dot(tri, lw_lo))
        l_tot_all.append(jnp.sum(lw_c, axis=0, keepdims=True))
        pc_out[d, 0, ch * 8:(ch + 1) * 8, :] = jnp.broadcast_to(jnp.exp(l_tot_all[ch]), (8, lw_c.shape[1]))

    a_bar, r_bar, lhs, b_inv, k_inv, b_end, k_end, vs = [], [], [], [], [], [], [], []
    for rows, ls, ch in units:
        kn_u = kn[rows, ls]
        k_u = kd[rows, ls]
        l_in = l_in_all[ch][:, ls]
        beta = kn_u * a[rows, ls]
        p_inv = jnp.exp(-l_in)
        p_end = jnp.exp(l_tot_all[ch][:, ls] - l_in)
        a_bar.append(-kn_u * jnp.exp(l_in - lw[rows, ls]))
        r_bar.append(r[rows, ls] * jnp.exp(l_in))
        lhs.append(jnp.concatenate([a_bar[-1], r_bar[-1]], axis=0).astype(BF16))
        b_inv.append((beta * p_inv).astype(BF16))
        k_inv.append((k_u * p_inv).astype(BF16))
        b_end.append((beta * p_end).astype(BF16))
        k_end.append((k_u * p_end).astype(BF16))
        vs.append(v[rows, ls])

    mb = each(lambda l, y: jnp.where(mask2, _dot_nt(l, _expand(y, lanehead)), 0.0), lhs, b_inv)
    mk = each(lambda l, y: jnp.where(mask2, _dot_nt(l, _expand(y, lanehead)), 0.0), lhs, k_inv)
    m_ab = [m[0:C] for m in mb]
    m_rb = [m[C:2 * C] for m in mb]
    d1 = [jnp.where(same16, m, 0.0) for m in m_ab]
    in32 = [jnp.where(same32, m, 0.0) for m in m_ab]
    d2 = each(mm, d1, d1)
    d4 = each(mm, d2, d2)
    tinv = [jnp.where(eye, 1.0, m) for m in d1]
    tinv = each(lambda t, p: t + mm(t, p), tinv, d2)
    d8 = each(mm, d4, d4)
    tinv = each(lambda t, p: t + mm(t, p), tinv, d4)
    mkv = each(mm, mk, vs)
    x1 = [m[0:C] for m in mkv]
    o_loc = [m[C:2 * C] for m in mkv]
    tinv = each(lambda t, p: t + mm(t, p), tinv, d8)
    te = each(lambda t, a, b: mm(t, a - b), tinv, in32, d1)
    tinv = each(lambda t, e: t + mm(e, t), tinv, te)
    te = each(lambda t, a, b: mm(t, a - b), tinv, m_ab, in32)
    tinv = each(lambda t, e: t + mm(e, t), tinv, te)
    w_hat = each(mm, tinv, a_bar)
    u_hat = each(mm, tinv, x1)
    for (rows, ls, ch), rb, w, rbar in zip(units, m_rb, w_hat, r_bar):
        q_out[d, 0, rows, ls] = (rbar + mm(rb, w)).astype(BF16)
    rkr = _dot_split(jnp.concatenate([r[rows, ls] * kd[rows, ls] * rk[:, ls] for rows, ls, ch in units], axis=0),
                     head_ones)
    for n, ((rows, ls, ch), rb, u, o, v_u) in enumerate(zip(units, m_rb, u_hat, o_loc, vs)):
        o_out[d, 0, rows, ls] = o + mm(rb, u) + rkr[n * C:(n + 1) * C] * v_u
    for (rows, ls, ch), w, be in zip(units, w_hat, b_end):
        g_out[d, 0, rows, ls] = _compact(_dot_tn(w.astype(BF16), be), lanehead).astype(BF16)
    for (rows, ls, ch), v_u, u, ke, be in zip(units, vs, u_hat, k_end, b_end):
        zh = _dot_tn(jnp.concatenate([v_u, u], axis=0).astype(BF16), jnp.concatenate([ke, be], axis=0))
        h_out[d, 0, rows, ls] = _compact(zh, lanehead)


def _rwkv_chunk_kernel(zp_ref, z_ref, zn_ref, mu_ref, kk_ref, ka_ref, wup_ref, w0_ref, aup_ref, a0_ref, gup_ref,
                       hsum_ref, rk_ref, gate_ref, q_out, o_out, g_out, h_out, pc_out,
                       *, n_ctx_blocks, n_blocks):
    i = pl.program_id(1)
    has_prev = (i != 0) & (i != n_ctx_blocks)
    has_next = (i != n_ctx_blocks - 1) & (i != n_blocks - 1)
    z = z_ref[0]
    tm = z.shape[0]
    row = lax.broadcasted_iota(jnp.int32, (tm, 1), 0)
    z_first = jnp.where(has_prev, zp_ref[0, 7:8, :], 0.0)
    z_last = jnp.where(has_next, zn_ref[0, 0:1, :], 0.0)
    z_prev = jnp.where(row == 0, z_first, pltpu.roll(z, 1, axis=0))
    z_next = jnp.where(row == tm - 1, z_last, pltpu.roll(z, tm - 1, axis=0))
    mu = mu_ref[...]
    zs = z + mu[0:1] * (z_prev - z) + mu[1:2] * (z_next - z)
    w = MIX
    r = zs[:, 0:w]
    k = zs[:, w:2 * w]
    v = zs[:, 2 * w:3 * w]
    wd = jnp.tanh(zs[:, 3 * w:3 * w + LANES]).astype(BF16)
    ad = zs[:, 3 * w + LANES:3 * w + 2 * LANES].astype(BF16)
    gd = _sigmoid(zs[:, 3 * w + 2 * LANES:3 * w + 3 * LANES]).astype(BF16)
    kk = k * kk_ref[...]
    norm = jnp.sqrt(_dot_split(kk * kk, hsum_ref[...]))
    kn = kk / jnp.maximum(norm, 1e-12)
    gate_ref[0] = _dot(gd, gup_ref[...])
    for d in range(2):
        logw = w0_ref[d:d + 1] + _dot(wd, wup_ref[d])
        lw = -jnp.exp(-_softplus(-logw) - 0.5)
        a = _sigmoid(a0_ref[d:d + 1] + _dot(ad, aup_ref[d]))
        kd = k * (1.0 + (a - 1.0) * ka_ref[...])
        _chunk_summaries(d, r, v, kn, lw, kd, a, rk_ref[...], q_out, o_out, g_out, h_out, pc_out)


def _rwkv_chunks(zr, mu, k_k, k_a, w_up, w0, a_up, a0, g_up, hsum, r_k, n_ctx):
    B, TT, NR = zr.shape
    tm = TOKEN_BLOCK
    C = RWKV_CHUNK
    assert C == HEAD_DIM and tm % C == 0
    nb = TT // tm
    ncb = n_ctx // tm
    rows = tm // 8
    tok = pl.BlockSpec((1, tm, MIX), lambda b, i: (b, i, 0))
    two = pl.BlockSpec((2, 1, tm, MIX), lambda b, i: (0, b, i, 0))
    pc = pl.BlockSpec((2, 1, 8 * (tm // C), MIX), lambda b, i: (0, b, i, 0))
    shape = lambda dt: jax.ShapeDtypeStruct((2, B, TT, MIX), dt)
    return pl.pallas_call(
        functools.partial(_rwkv_chunk_kernel, n_ctx_blocks=ncb, n_blocks=nb),
        grid=(B, nb),
        in_specs=[pl.BlockSpec((1, 8, NR), lambda b, i: (b, jnp.maximum(i * rows - 1, 0), 0)),
                  pl.BlockSpec((1, tm, NR), lambda b, i: (b, i, 0)),
                  pl.BlockSpec((1, 8, NR), lambda b, i: (b, jnp.minimum((i + 1) * rows, TT // 8 - 1), 0)),
                  _resident((2, NR)), _resident((1, MIX)), _resident((1, MIX)),
                  _resident((2, LANES, MIX)), _resident((2, MIX)), _resident((2, LANES, MIX)), _resident((2, MIX)),
                  _resident((LANES, MIX)), _resident((MIX, MIX)), _resident((1, MIX))],
        out_specs=[tok, two, two, two, two, pc],
        out_shape=[jax.ShapeDtypeStruct((B, TT, MIX), F32),
                   shape(BF16), shape(F32), shape(BF16), shape(F32),
                   jax.ShapeDtypeStruct((2, B, 8 * (TT // C), MIX), F32)],
        compiler_params=_params("parallel", "parallel"),
        name="rwkv_chunks",
    )(zr, zr, zr, mu, k_k, k_a, w_up, w0, a_up, a0, g_up, hsum, r_k)


def _rwkv_state_kernel(qf_ref, of_ref, gf_ref, hf_ref, pf_ref, qr_ref, or_ref, gr_ref, hr_ref, pr_ref,
                       outf_ref, outr_ref, s_scr):
    s = pl.program_id(0)

    @pl.when(s == 0)
    def _():
        s_scr[...] = jnp.zeros_like(s_scr)

    lanehead = lax.broadcasted_iota(jnp.int32, (1, GROUP), 1) >> 6
    nb = outf_ref.shape[0]
    dirs = ((qf_ref, of_ref, gf_ref, hf_ref, pf_ref, outf_ref), (qr_ref, or_ref, gr_ref, hr_ref, pr_ref, outr_ref))
    for d, (q_ref, o_ref, g_ref, h_ref, p_ref, out_ref) in enumerate(dirs):
        for b in range(nb):
            for g in range(N_GROUPS):
                ls = slice(g * GROUP, (g + 1) * GROUP)
                st = s_scr[d, b, g]
                sb = st.astype(BF16)
                out_ref[b, :, ls] = _dot_nt(q_ref[0, b, :, ls], _expand(sb, lanehead)) + o_ref[0, b, :, ls]
                s_scr[d, b, g] = (st * p_ref[0, b, 0:1, ls] + _dot(sb, _expand(g_ref[0, b, :, ls], lanehead))
                                  + h_ref[0, b, :, ls])


def _rwkv_state(q, o0, gm, hm, pc, n_ctx):
    _, B, TT, W = q.shape
    C = RWKV_CHUNK
    nch = TT // C
    ncc = n_ctx // C
    fwd = lambda s: s
    rev = lambda s: jnp.where(s < ncc, ncc - 1 - s, nch - 1 - (s - ncc))
    spec = lambda d, f, rows: pl.BlockSpec((1, B, rows, W), lambda s: (d, 0, f(s), 0))
    out = lambda f: pl.BlockSpec((B, C, W), lambda s: (0, f(s), 0))
    shape = jax.ShapeDtypeStruct((B, TT, W), F32)
    return pl.pallas_call(
        _rwkv_state_kernel,
        grid=(nch,),
        in_specs=[spec(0, fwd, C)] * 4 + [spec(0, fwd, 8)] + [spec(1, rev, C)] * 4 + [spec(1, rev, 8)],
        out_specs=[out(fwd), out(rev)],
        out_shape=[shape, shape],
        scratch_shapes=[pltpu.VMEM((2, B, N_GROUPS, HEAD_DIM, GROUP), F32)],
        compiler_params=_params("arbitrary"),
        name="rwkv_state",
    )(q, o0, gm, hm, pc, q, o0, gm, hm, pc)


def _merge_ffn_kernel(x_ref, mod_ref, ya_ref, yl_ref, of_ref, or_ref, gate_ref, lng_ref, lnb_ref, havg_ref, zg_ref,
                      proj_ref, wo_ref, g_ref, wgu_ref, wd_ref, out_ref):
    D = x_ref.shape[2]
    mod = mod_ref[0, 0]
    o = of_ref[0] + or_ref[0]
    mean = _dot_split(o, havg_ref[...])
    cen = o - mean
    var = _dot_split(cen * cen, havg_ref[...])
    yr = ((cen * lax.rsqrt(var + RWKV_LN_EPS)) * lng_ref[...] + lnb_ref[...]) * gate_ref[0]
    ys = (ya_ref[0], yl_ref[...], yr.astype(BF16))
    acc = None
    for b in range(3):
        term = _sigmoid(zg_ref[0, :, b * D:(b + 1) * D]) * _dot(ys[b], proj_ref[b])
        acc = term if acc is None else acc + term
    x = x_ref[0] + mod[0:1] * _dot(acc.astype(BF16), wo_ref[...])
    out_ref[0] = _ffn_half_step(x, mod[1:4], g_ref[...], wgu_ref, wd_ref)


def _merge_ffn(x, mod4, ya, yl, o_fwd, o_rev, gate, ln_g, ln_b, havg, zg, proj, w_o, g, w_gu, w_d, layer,
               n_ctx_blocks, latent_only):
    B, TT, D = x.shape
    F = w_d.shape[2]
    tm = TOKEN_BLOCK
    first = n_ctx_blocks if latent_only else 0
    n_out = TT // tm - first
    kind = lambda i: jnp.where(i + first < n_ctx_blocks, 0, 1)
    tok = lambda w: pl.BlockSpec((1, tm, w), lambda b, i: (b, i + first, 0))
    return pl.pallas_call(
        _merge_ffn_kernel,
        grid=(B, n_out),
        in_specs=[tok(D),
                  pl.BlockSpec((1, 1, 4, D), lambda b, i: (b, kind(i), 0, 0)),
                  tok(MIX),
                  pl.BlockSpec((tm, MIX), lambda b, i: (i + first, b)),
                  tok(MIX), tok(MIX), tok(MIX),
                  _resident((1, MIX)), _resident((1, MIX)), _resident((MIX, MIX)),
                  tok(3 * D),
                  _layer_slice((3, MIX, D), layer), _layer_slice((D, D), layer),
                  _resident((1, D)), _layer_slice((D, 2 * F), layer, 1), _layer_slice((F, D), layer, 1)],
        out_specs=pl.BlockSpec((1, tm, D), lambda b, i: (b, i, 0)),
        out_shape=jax.ShapeDtypeStruct((B, n_out * tm, D), F32),
        compiler_params=_params("parallel", "parallel"),
        name="merge_ffn",
    )(x, mod4, ya, yl, o_fwd, o_rev, gate, ln_g, ln_b, havg, zg, proj, w_o, g, w_gu, w_d)


def _rope_tables(n_ctx, n_tok):
    n_freq = HEAD_DIM // 4
    t = jnp.arange(n_tok)
    row = (t // GRID_W).astype(F32)
    col = (t % GRID_W).astype(F32)
    inv = ROPE_BASE ** (-jnp.arange(n_freq, dtype=F32) / n_freq)
    ang_r = row[:, None] * inv
    ang_c = col[:, None] * inv
    cos = jnp.concatenate([jnp.cos(ang_r)] * 2 + [jnp.cos(ang_c)] * 2, axis=-1)
    sin = jnp.concatenate([-jnp.sin(ang_r), jnp.sin(ang_r), -jnp.sin(ang_c), jnp.sin(ang_c)], axis=-1)
    cos = jnp.concatenate([jnp.ones((n_ctx, HEAD_DIM), F32), cos], axis=0)
    sin = jnp.concatenate([jnp.zeros((n_ctx, HEAD_DIM), F32), sin], axis=0)
    return jnp.tile(cos, (1, LANES // HEAD_DIM)), jnp.tile(sin, (1, LANES // HEAD_DIM))


def _block_diag(w):
    n, r, c = w.shape
    eye = jnp.eye(n, dtype=w.dtype)
    return (eye[:, None, :, None] * w[:, :, None, :]).reshape(n * r, n * c)


def _pad_rows(w, rank):
    z = jnp.zeros_like(w[0])
    return jnp.stack([jnp.concatenate([w[0], z], axis=0), jnp.concatenate([z, w[1]], axis=0)])


def kernel(x, c, ctx, c_ctx, ada_w, ada_b, norm_g, ffn_w_gu, ffn_w_d, w_in, attn_q_gain, attn_k_gain, attn_sink, lru_conv_w, lru_conv_b, lru_gate_w, lru_gate_b, lru_lambda, rwkv_mu, rwkv_w_up, rwkv_w0, rwkv_a_up, rwkv_a0, rwkv_g_up, rwkv_k_k, rwkv_k_a, rwkv_r_k, rwkv_ln_g, rwkv_ln_b, branch_proj, w_out):
    B, T, D = x.shape
    n_ctx = ctx.shape[1]
    L = ada_w.shape[0]
    TT = n_ctx + T
    assert n_ctx % TOKEN_BLOCK == 0 and T % TOKEN_BLOCK == 0 and T % GRID_W == 0
    ncb = n_ctx // TOKEN_BLOCK
    d_ff = ffn_w_d.shape[2]
    n_rwkv = rwkv_mu.shape[2]
    n_gate = 3 * D

    n_rows = -(-(B + 1) // 8) * 8
    c_rows = jnp.zeros((n_rows, D), F32).at[:B].set(c).at[B].set(c_ctx)
    ada = _ada_vectors(c_rows, ada_w, ada_b)
    mod_lat = ada[:, :B].reshape(L, B, 1, N_ADA, D)
    mod_ctx = jnp.broadcast_to(ada[:, B].reshape(L, 1, 1, N_ADA, D), (L, B, 1, N_ADA, D))
    mods = jnp.concatenate([mod_ctx, mod_lat], axis=2)

    cos_t, sin_t = _rope_tables(n_ctx, T)
    hsum = jnp.kron(jnp.eye(RWKV_HEADS, dtype=BF16), jnp.ones((HEAD_DIM, HEAD_DIM), BF16))
    havg = hsum / HEAD_DIM
    tile2 = lambda g: jnp.tile(g, LANES // HEAD_DIM).reshape(1, LANES)

    w_gu_b = ffn_w_gu.astype(BF16)
    w_d_b = ffn_w_d.astype(BF16)
    w_in_b = w_in.astype(BF16)
    proj_b = branch_proj.astype(BF16)
    w_out_b = w_out.astype(BF16)

    xs = x
    for l in range(L):
        xs = _ffn(xs, mods[l, :, :, 0:3], norm_g[l, 0].reshape(1, D), w_gu_b, w_d_b, l, ncb,
                  ctx=ctx if l == 0 else None)

        q, k, v, ux, ug, zr, zg = _inproj(
            xs, mods[l, :, :, 3:6], norm_g[l, 1].reshape(1, D), w_in_b, l,
            tile2(attn_q_gain[l]), tile2(attn_k_gain[l]), cos_t, sin_t, ncb, n_rwkv, n_gate)

        ya = _attention(q, k, v, attn_sink[l], n_ctx)

        ux3 = ux.reshape(TT, B, MIX)
        ug3 = ug.reshape(TT, B, MIX)
        lru_args = lambda d: (
            lru_conv_w[l], lru_conv_b[l].reshape(1, MIX),
            jnp.concatenate([_block_diag(lru_gate_w[l, d, 0]), _block_diag(lru_gate_w[l, d, 1])], axis=1).astype(BF16),
            lru_gate_b[l, d].reshape(1, 2 * MIX), lru_lambda[l, d].reshape(1, MIX))
        h_fwd = _lru_direction(ux3, *lru_args(0), n_ctx, False)
        yl = _lru_direction(ux3, *lru_args(1), n_ctx, True, h_fwd, ug3).reshape(TT, B * MIX)

        gate, *summaries = _rwkv_chunks(
            zr, rwkv_mu[l], rwkv_k_k[l].reshape(1, MIX), rwkv_k_a[l].reshape(1, MIX),
            _pad_rows(rwkv_w_up[l], rwkv_w_up.shape[2]).astype(BF16), rwkv_w0[l],
            _pad_rows(rwkv_a_up[l], rwkv_a_up.shape[2]).astype(BF16), rwkv_a0[l],
            rwkv_g_up[l].astype(BF16), hsum, rwkv_r_k[l].reshape(1, MIX), n_ctx)
        o_fwd, o_rev = _rwkv_state(*summaries, n_ctx)

        xs = _merge_ffn(xs, mods[l, :, :, 5:9], ya, yl, o_fwd, o_rev, gate,
                        rwkv_ln_g[l].reshape(1, MIX), rwkv_ln_b[l].reshape(1, MIX), havg, zg,
                        proj_b, w_out_b, norm_g[l, 2].reshape(1, D), w_gu_b, w_d_b, l, ncb,
                        latent_only=(l == L - 1))
    return xs
```

```python
import functools

import jax
import jax.numpy as jnp
from jax import lax
from jax.experimental import pallas as pl
from jax.experimental.pallas import tpu as pltpu

F32 = jnp.float32
BF16 = jnp.bfloat16
HIGHEST = lax.Precision.HIGHEST

N_ADA = 9
NORM_EPS = 1e-6
GRID_W = 64
HEAD_DIM = 64
ATTN_HEADS = 8
ATTN_KV_HEADS = 2
ATTN_BLOCK = 128
ROPE_BASE = 10000.0
NEG_INF = -1e30
LRU_C = 8.0
LRU_BLOCKS = 8
RWKV_HEADS = 8
RWKV_LN_EPS = 64e-5
MIX = 512
LANES = 128
TOKEN_BLOCK = 256
LRU_CHUNK = 128
RWKV_CHUNK = 64
VMEM_LIMIT = 56 * 1024 * 1024


def _params(*sem):
    return pltpu.CompilerParams(dimension_semantics=sem, vmem_limit_bytes=VMEM_LIMIT)


def _resident(shape):
    nd = len(shape)
    return pl.BlockSpec(shape, lambda *_: (0,) * nd, pipeline_mode=pl.Buffered(1))


def _layer_slice(shape, *index):
    nd = len(shape)
    return pl.BlockSpec((None,) * len(index) + tuple(shape), lambda *_: tuple(index) + (0,) * nd,
                        pipeline_mode=pl.Buffered(1))


def _dot(a, b):
    return jnp.dot(a, b, preferred_element_type=F32)


def _dot_f32(a, b):
    return jnp.dot(a, b, preferred_element_type=F32, precision=HIGHEST)


def _dot_split(x, w):
    x_hi = x.astype(BF16)
    x_lo = (x - x_hi.astype(F32)).astype(BF16)
    return _dot(x_hi, w) + _dot(x_lo, w)


def _dot_nt(a, b, precision=None):
    return lax.dot_general(a, b, (((1,), (1,)), ((), ())), preferred_element_type=F32, precision=precision)


def _dot_tn(a, b, precision=None):
    return lax.dot_general(a, b, (((0,), (0,)), ((), ())), preferred_element_type=F32, precision=precision)


def _sigmoid(x):
    return 0.5 * jnp.tanh(0.5 * x) + 0.5


def _softplus(x):
    return jnp.maximum(x, 0.0) + jnp.log(1.0 + jnp.exp(-jnp.abs(x)))


def _modulated_norm(x, g, shift, scale):
    y = x * lax.rsqrt(jnp.mean(x * x, axis=-1, keepdims=True) + NORM_EPS)
    return (y * g) * (1.0 + scale) + shift


def _ada_kernel(s_ref, w_ref, b_ref, o_ref):
    s = s_ref[...]
    s = (s * _sigmoid(s)).astype(BF16)
    o_ref[0] = _dot(s, w_ref[0].astype(BF16)) + b_ref[0]


def _ada_vectors(c_rows, ada_w, ada_b):
    L, D, N = ada_w.shape
    R = c_rows.shape[0]
    tn = 1024
    return pl.pallas_call(
        _ada_kernel,
        grid=(L, N // tn),
        in_specs=[pl.BlockSpec((R, D), lambda l, j: (0, 0)),
                  pl.BlockSpec((1, D, tn), lambda l, j: (l, 0, j)),
                  pl.BlockSpec((1, 1, tn), lambda l, j: (l, 0, j))],
        out_specs=pl.BlockSpec((1, R, tn), lambda l, j: (l, 0, j)),
        out_shape=jax.ShapeDtypeStruct((L, R, N), F32),
        compiler_params=_params("parallel", "parallel"),
        name="ada_vectors",
    )(c_rows, ada_w, ada_b.reshape(L, 1, N))


def _ffn_half_step(x, mod, g, wgu_ref, wd_ref):
    d_ff = wd_ref.shape[0]
    h = _modulated_norm(x, g, mod[0:1], mod[1:2]).astype(BF16)
    gt = _dot(h, wgu_ref[:, 0:d_ff])
    up = _dot(h, wgu_ref[:, d_ff:2 * d_ff])
    act = (gt * _sigmoid(gt) * up).astype(BF16)
    return x + 0.5 * mod[2:3] * _dot(act, wd_ref[...])


def _ffn_kernel(*refs, n_ctx_blocks, split_input):
    if split_input:
        c_ref, x_ref, mod_ref, g_ref, wgu_ref, wd_ref, o_ref = refs
        x = jnp.where(pl.program_id(1) < n_ctx_blocks, c_ref[0], x_ref[0])
    else:
        x_ref, mod_ref, g_ref, wgu_ref, wd_ref, o_ref = refs
        x = x_ref[0]
    o_ref[0] = _ffn_half_step(x, mod_ref[0, 0], g_ref[...], wgu_ref, wd_ref)


def _ffn(x, mod3, g, w_gu, w_d, layer, n_ctx_blocks, ctx=None):
    B, _, D = x.shape
    F = w_d.shape[2]
    tm = TOKEN_BLOCK
    kind = lambda i: jnp.where(i < n_ctx_blocks, 0, 1)
    if ctx is None:
        n_out = x.shape[1] // tm
        x_specs = [pl.BlockSpec((1, tm, D), lambda b, i: (b, i, 0))]
        xs = (x,)
    else:
        n_out = n_ctx_blocks + x.shape[1] // tm
        x_specs = [pl.BlockSpec((1, tm, D), lambda b, i: (b, jnp.minimum(i, n_ctx_blocks - 1), 0)),
                   pl.BlockSpec((1, tm, D), lambda b, i: (b, jnp.maximum(i - n_ctx_blocks, 0), 0))]
        xs = (ctx, x)
    return pl.pallas_call(
        functools.partial(_ffn_kernel, n_ctx_blocks=n_ctx_blocks, split_input=ctx is not None),
        grid=(B, n_out),
        in_specs=x_specs + [pl.BlockSpec((1, 1, 3, D), lambda b, i: (b, kind(i), 0, 0)),
                            _resident((1, D)), _layer_slice((D, 2 * F), layer, 0), _layer_slice((F, D), layer, 0)],
        out_specs=pl.BlockSpec((1, tm, D), lambda b, i: (b, i, 0)),
        out_shape=jax.ShapeDtypeStruct((B, n_out * tm, D), F32),
        compiler_params=_params("parallel", "parallel"),
        name="ffn",
    )(*xs, mod3, g, w_gu, w_d)


def _head_rms_rope(xs, gain, cos, sin, lane):
    sq = xs * xs
    lo = lane < HEAD_DIM
    s_lo = jnp.sum(jnp.where(lo, sq, 0.0), axis=-1, keepdims=True)
    s_hi = jnp.sum(jnp.where(lo, 0.0, sq), axis=-1, keepdims=True)
    ms = jnp.where(lo, s_lo, s_hi) * (1.0 / HEAD_DIM)
    y = xs * lax.rsqrt(ms + NORM_EPS) * gain
    first = (lane % 32) < 16
    partner = jnp.where(first, pltpu.roll(y, LANES - 16, axis=1), pltpu.roll(y, 16, axis=1))
    return y * cos + partner * sin


def _inproj_kernel(x_ref, mod_ref, g_ref, w_ref, qg_ref, kg_ref, cos_ref, sin_ref,
                   q_ref, k_ref, v_ref, ux_ref, ug_ref, zr_ref, zg_ref):
    x = x_ref[0]
    mod = mod_ref[0, 0]
    h = _modulated_norm(x, g_ref[...], mod[0:1], mod[1:2]).astype(BF16)
    nq = ATTN_HEADS * HEAD_DIM
    nkv = ATTN_KV_HEADS * HEAD_DIM
    lane = lax.broadcasted_iota(jnp.int32, (1, LANES), 1)
    cos = cos_ref[...]
    sin = sin_ref[...]
    za = _dot(h, w_ref[:, 0:nq + 2 * nkv])
    for s in range(nq // LANES):
        qs = _head_rms_rope(za[:, s * LANES:(s + 1) * LANES], qg_ref[...], cos, sin, lane)
        q_ref[0, :, s * LANES:(s + 1) * LANES] = (qs * HEAD_DIM ** -0.5).astype(BF16)
    k_ref[0] = _head_rms_rope(za[:, nq:nq + nkv], kg_ref[...], cos, sin, lane).astype(BF16)
    v_ref[0] = za[:, nq + nkv:nq + 2 * nkv].astype(BF16)
    c0 = nq + 2 * nkv
    zl = _dot(h, w_ref[:, c0:c0 + 2 * MIX])
    ux_ref[...] = zl[:, 0:MIX]
    ug_ref[...] = zl[:, MIX:2 * MIX]
    c1 = c0 + 2 * MIX
    nr = zr_ref.shape[2]
    zr_ref[0] = _dot(h, w_ref[:, c1:c1 + nr])
    c2 = c1 + nr
    zg_ref[0] = _dot(h, w_ref[:, c2:c2 + zg_ref.shape[2]])


def _inproj(x, mod3, g, w_in, layer, q_gain, k_gain, cos_t, sin_t, n_ctx_blocks, n_rwkv, n_gate):
    B, TT, D = x.shape
    tm = TOKEN_BLOCK
    nq = ATTN_HEADS * HEAD_DIM
    nkv = ATTN_KV_HEADS * HEAD_DIM
    kind = lambda i: jnp.where(i < n_ctx_blocks, 0, 1)
    tok = lambda w: pl.BlockSpec((1, tm, w), lambda b, i: (b, i, 0))
    tmajor = pl.BlockSpec((tm, MIX), lambda b, i: (i, b))
    return pl.pallas_call(
        _inproj_kernel,
        grid=(B, TT // tm),
        in_specs=[tok(D),
                  pl.BlockSpec((1, 1, 3, D), lambda b, i: (b, kind(i), 0, 0)),
                  _resident((1, D)), _layer_slice(w_in.shape[1:], layer), _resident((1, LANES)), _resident((1, LANES)),
                  pl.BlockSpec((tm, LANES), lambda b, i: (i, 0)),
                  pl.BlockSpec((tm, LANES), lambda b, i: (i, 0))],
        out_specs=[tok(nq), tok(nkv), tok(nkv), tmajor, tmajor, tok(n_rwkv), tok(n_gate)],
        out_shape=[jax.ShapeDtypeStruct((B, TT, nq), BF16),
                   jax.ShapeDtypeStruct((B, TT, nkv), BF16),
                   jax.ShapeDtypeStruct((B, TT, nkv), BF16),
                   jax.ShapeDtypeStruct((TT, B * MIX), F32),
                   jax.ShapeDtypeStruct((TT, B * MIX), F32),
                   jax.ShapeDtypeStruct((B, TT, n_rwkv), F32),
                   jax.ShapeDtypeStruct((B, TT, n_gate), F32)],
        compiler_params=_params("parallel", "parallel"),
        name="inproj",
    )(x, mod3, g, w_in, q_gain, k_gain, cos_t, sin_t)


def _attn_kernel(sink_ref, q_ref, kp_ref, kc_ref, kn_ref, kx_ref, vp_ref, vc_ref, vn_ref, vx_ref, o_ref,
                 *, n_ctx_blocks, n_blocks):
    i = pl.program_id(1)
    blk = ATTN_BLOCK
    n = HEAD_DIM
    rep = ATTN_HEADS // ATTN_KV_HEADS
    rows = rep * blk
    row = lax.broadcasted_iota(jnp.int32, (rows, 1), 0)

    def sink_column(g):
        col = jnp.full((rows, 1), sink_ref[g * rep], F32)
        for r in range(1, rep):
            col = jnp.where(row >= r * blk, sink_ref[g * rep + r], col)
        return col

    def attend(latent):
        if latent:
            kcat = jnp.concatenate([kp_ref[0], kc_ref[0], kn_ref[0], kx_ref[0]], axis=0)
            vcat = jnp.concatenate([vp_ref[0], vc_ref[0], vn_ref[0], vx_ref[0]], axis=0)
            far = 4 * blk
            off_prev = jnp.where(i >= n_ctx_blocks + 1, 0, far)
            off_next = jnp.where(i + 1 <= n_blocks - 1, 0, far)
            n_keys = kcat.shape[0]
            qi = lax.broadcasted_iota(jnp.int32, (rows, n_keys), 0) & (blk - 1)
            col = lax.broadcasted_iota(jnp.int32, (rows, n_keys), 1)
            mask = (((col >= blk) | (col >= qi + off_prev))
                    & ((col < 2 * blk) | (col >= 3 * blk) | (col - 2 * blk + off_next <= qi)))
        else:
            kcat = kx_ref[0]
            vcat = vx_ref[0]
            mask = None
        groups = range(ATTN_KV_HEADS)
        ss = []
        for g in groups:
            qg = jnp.concatenate([q_ref[0, :, (g * rep + r) * n:(g * rep + r + 1) * n] for r in range(rep)], axis=0)
            s = _dot_nt(qg, kcat[:, g * n:(g + 1) * n])
            ss.append(s if mask is None else jnp.where(mask, s, NEG_INF))
        sinks = [sink_column(g) for g in groups]
        ms = [jnp.maximum(jnp.max(s, axis=-1, keepdims=True), sk) for s, sk in zip(ss, sinks)]
        ps = [jnp.exp(s - m) for s, m in zip(ss, ms)]
        dens = [jnp.sum(p, axis=-1, keepdims=True) + jnp.exp(sk - m) for p, sk, m in zip(ps, sinks, ms)]
        outs = [_dot(p.astype(BF16), vcat[:, g * n:(g + 1) * n]) / den for g, p, den in zip(groups, ps, dens)]
        for g in groups:
            for r in range(rep):
                h = g * rep + r
                o_ref[0, :, h * n:(h + 1) * n] = outs[g][r * blk:(r + 1) * blk].astype(BF16)

    @pl.when(i >= n_ctx_blocks)
    def _():
        attend(True)

    @pl.when(i < n_ctx_blocks)
    def _():
        attend(False)


def _attention(q, k, v, sink, n_ctx):
    B, TT, nq = q.shape
    nkv = k.shape[2]
    blk = ATTN_BLOCK
    nb = TT // blk
    ncb = n_ctx // blk
    kv = lambda f: pl.BlockSpec((1, blk, nkv), lambda b, i: (b, f(i), 0))
    prev = kv(lambda i: jnp.maximum(i - 1, 0))
    cur = kv(lambda i: i)
    nxt = kv(lambda i: jnp.minimum(i + 1, nb - 1))
    cx = pl.BlockSpec((1, n_ctx, nkv), lambda b, i: (b, 0, 0))
    return pl.pallas_call(
        functools.partial(_attn_kernel, n_ctx_blocks=ncb, n_blocks=nb),
        grid=(B, nb),
        in_specs=[pl.BlockSpec(memory_space=pltpu.SMEM),
                  pl.BlockSpec((1, blk, nq), lambda b, i: (b, i, 0)),
                  prev, cur, nxt, cx, prev, cur, nxt, cx],
        out_specs=pl.BlockSpec((1, blk, nq), lambda b, i: (b, i, 0)),
        out_shape=jax.ShapeDtypeStruct((B, TT, nq), BF16),
        compiler_params=_params("parallel", "parallel"),
        name="attention",
    )(sink, q, k, k, k, k, v, v, v, v)


def _lru_chunk_index(s, n_ctx_chunks, n_chunks, reverse):
    if not reverse:
        return s
    return jnp.where(s < n_ctx_chunks, n_ctx_chunks - 1 - s, n_chunks - 1 - (s - n_ctx_chunks))


def _lru_kernel(up_ref, u_ref, un_ref, cw_ref, cb_ref, gw_ref, gb_ref, lam_ref, *rest,
                n_ctx_chunks, n_chunks, reverse):
    if reverse:
        hf_ref, ug_ref, y_ref, a_scr, b_scr, h_scr = rest
    else:
        h_out_ref, a_scr, b_scr, h_scr = rest
    s = pl.program_id(0)
    c = _lru_chunk_index(s, n_ctx_chunks, n_chunks, reverse)
    tc, nb, w = u_ref.shape

    @pl.when(s == 0)
    def _():
        h_scr[...] = jnp.zeros_like(h_scr)

    has_prev = (c != 0) & (c != n_ctx_chunks)
    has_next = (c != n_ctx_chunks - 1) & (c != n_chunks - 1)
    u = u_ref[...]
    ext = jnp.concatenate([jnp.where(has_prev, up_ref[...], 0.0), u, jnp.where(has_next, un_ref[...], 0.0)], axis=0)
    cw = cw_ref[...]
    xc = cb_ref[...] + cw[0:1] * ext[0:tc] + cw[1:2] * ext[1:tc + 1] + cw[2:3] * ext[2:tc + 2] + cw[3:4] * ext[3:tc + 3]

    gates = _dot(xc.reshape(tc * nb, w).astype(BF16), gw_ref[...]).reshape(tc, nb, 2 * w) + gb_ref[...]
    r = _sigmoid(gates[:, :, 0:w])
    ig = _sigmoid(gates[:, :, w:2 * w])
    log_a = (-LRU_C * _softplus(-lam_ref[...])) * r
    a = jnp.exp(log_a)
    a_scr[...] = a
    b_scr[...] = jnp.sqrt(-jnp.tanh(log_a) * (a * a + 1.0)) * (ig * xc)

    def step(t, h):
        tt = tc - 1 - t if reverse else t
        h = a_scr[tt] * h + b_scr[tt]
        b_scr[tt] = h
        return h

    h_scr[...] = lax.fori_loop(0, tc, step, h_scr[...], unroll=8)

    if reverse:
        ug = ug_ref[...]
        gelu = 0.5 * ug * (1.0 + jnp.tanh(0.7978845608028654 * (ug + 0.044715 * ug * ug * ug)))
        y_ref[...] = ((hf_ref[...] + b_scr[...]) * gelu).astype(BF16)
    else:
        h_out_ref[...] = b_scr[...]


def _lru_direction(ux3, conv_w, conv_b, gate_w, gate_b, lam, n_ctx, reverse, h_fwd=None, ug3=None):
    TT, B, W = ux3.shape
    tc = LRU_CHUNK
    nch = TT // tc
    ncc = n_ctx // tc
    cidx = lambda s: _lru_chunk_index(s, ncc, nch, reverse)
    chunk = pl.BlockSpec((tc, B, W), lambda s: (cidx(s), 0, 0))
    in_specs = [pl.BlockSpec((2, B, W), lambda s: (jnp.maximum(cidx(s) * (tc // 2) - 1, 0), 0, 0)),
                chunk,
                pl.BlockSpec((1, B, W), lambda s: (jnp.minimum((cidx(s) + 1) * tc, TT - 1), 0, 0)),
                _resident((4, W)), _resident((1, W)), _resident((W, 2 * W)), _resident((1, 2 * W)),
                _resident((1, W))]
    args = [ux3, ux3, ux3, conv_w, conv_b, gate_w, gate_b, lam]
    if reverse:
        in_specs += [chunk, chunk]
        args += [h_fwd, ug3]
        out_dtype = BF16
    else:
        out_dtype = F32
    return pl.pallas_call(
        functools.partial(_lru_kernel, n_ctx_chunks=ncc, n_chunks=nch, reverse=reverse),
        grid=(nch,),
        in_specs=in_specs,
        out_specs=chunk,
        out_shape=jax.ShapeDtypeStruct((TT, B, W), out_dtype),
        scratch_shapes=[pltpu.VMEM((tc, B, W), F32), pltpu.VMEM((tc, B, W), F32), pltpu.VMEM((B, W), F32)],
        compiler_params=_params("arbitrary"),
        name="lru_rev" if reverse else "lru_fwd",
    )(*args)


GROUP_HEADS = LANES // HEAD_DIM
GROUP = GROUP_HEADS * HEAD_DIM
N_GROUPS = RWKV_HEADS // GROUP_HEADS


def _expand(y, lanehead):
    return jnp.concatenate([jnp.where(lanehead == h, y, jnp.zeros_like(y)) for h in range(GROUP_HEADS)], axis=0)


def _compact(z, lanehead):
    n = HEAD_DIM
    out = jnp.where(lanehead == 0, z[0:n], 0.0)
    for h in range(1, GROUP_HEADS):
        out = out + jnp.where(lanehead == h, z[h * n:(h + 1) * n], 0.0)
    return out


def _chunk_summaries(d, r, v, kn, lw, kd, a, rk, outs, side=()):
    q_out, o_out, g_out, h_out, pc_out = outs
    side = list(side)

    def run_side():
        if side:
            side.pop(0)()

    C = RWKV_CHUNK
    tm = r.shape[0]
    sgn = 1 - 2 * d
    ri = lax.broadcasted_iota(jnp.int32, (C, C), 0)
    ci = lax.broadcasted_iota(jnp.int32, (C, C), 1)
    tri = ((ri - ci) * sgn >= 0).astype(BF16)
    t = lax.broadcasted_iota(jnp.int32, (C, GROUP), 0)
    lane = lax.broadcasted_iota(jnp.int32, (C, GROUP), 1)
    i = lane & (C - 1)
    eye = t == i
    same16 = (t >> 4) == (i >> 4)
    same32 = (t >> 5) == (i >> 5)
    lanehead = lax.broadcasted_iota(jnp.int32, (1, GROUP), 1) >> 6
    t2 = lax.broadcasted_iota(jnp.int32, (2 * C, GROUP), 0)
    i2 = lax.broadcasted_iota(jnp.int32, (2 * C, GROUP), 1) & (C - 1)
    mask2 = ((t2 & (C - 1)) - i2) * sgn > jnp.where(t2 < C, 0, -1)
    rowhead = lax.broadcasted_iota(jnp.int32, (GROUP, GROUP), 0) >> 6
    colhead = lax.broadcasted_iota(jnp.int32, (GROUP, GROUP), 1) >> 6
    head_ones = (rowhead == colhead).astype(BF16)

    def mm(x, y):
        return _dot(x.astype(BF16), _expand(y.astype(BF16), lanehead))

    def each(f, *lists):
        return [f(*xs) for xs in zip(*lists)]

    n_ch = tm // C
    units = [(slice(ch * C, (ch + 1) * C), slice(g * GROUP, (g + 1) * GROUP), ch)
             for ch in range(n_ch) for g in range(N_GROUPS)]
    l_in_all, l_tot_all = [], []
    for ch in range(n_ch):
        lw_c = lw[ch * C:(ch + 1) * C]
        lw_hi = lw_c.astype(BF16)
        rest = lw_c - lw_hi.astype(F32)
        lw_mid = rest.astype(BF16)
        lw_lo = (rest - lw_mid.astype(F32)).astype(BF16)
        l_in_all.append(_dot(tri, lw_hi) + _dot(tri, lw_mid) + _dot(tri, lw_lo))
        l_tot_all.append(jnp.sum(lw_c, axis=0, keepdims=True))
        pc_out[ch * 8:(ch + 1) * 8, :] =jnp.broadcast_to(jnp.exp(l_tot_all[ch]), (8, lw_c.shape[1]))

    a_bar, r_bar, lhs, b_inv, k_inv, b_end, k_end, vs = [], [], [], [], [], [], [], []
    for rows, ls, ch in units:
        kn_u = kn[rows, ls]
        k_u = kd[rows, ls]
        l_in = l_in_all[ch][:, ls]
        beta = kn_u * a[rows, ls]
        p_inv = jnp.exp(-l_in)
        p_end = jnp.exp(l_tot_all[ch][:, ls] - l_in)
        a_bar.append(-kn_u * jnp.exp(l_in - lw[rows, ls]))
        r_bar.append(r[rows, ls] * jnp.exp(l_in))
        lhs.append(jnp.concatenate([a_bar[-1], r_bar[-1]], axis=0).astype(BF16))
        b_inv.append((beta * p_inv).astype(BF16))
        k_inv.append((k_u * p_inv).astype(BF16))
        b_end.append((beta * p_end).astype(BF16))
        k_end.append((k_u * p_end).astype(BF16))
        vs.append(v[rows, ls])

    mb = each(lambda l, y: jnp.where(mask2, _dot_nt(l, _expand(y, lanehead)), 0.0), lhs, b_inv)
    mk = each(lambda l, y: jnp.where(mask2, _dot_nt(l, _expand(y, lanehead)), 0.0), lhs, k_inv)
    m_ab = [m[0:C] for m in mb]
    m_rb = [m[C:2 * C] for m in mb]
    run_side()
    d1 = [jnp.where(same16, m, 0.0) for m in m_ab]
    in32 = [jnp.where(same32, m, 0.0) for m in m_ab]
    d2 = each(mm, d1, d1)
    d4 = each(mm, d2, d2)
    tinv = [jnp.where(eye, 1.0, m) for m in d1]
    tinv = each(lambda t, p: t + mm(t, p), tinv, d2)
    d8 = each(mm, d4, d4)
    tinv = each(lambda t, p: t + mm(t, p), tinv, d4)
    mkv = each(mm, mk, vs)
    x1 = [m[0:C] for m in mkv]
    o_loc = [m[C:2 * C] for m in mkv]
    tinv = each(lambda t, p: t + mm(t, p), tinv, d8)
    run_side()
    te = each(lambda t, a, b: mm(t, a - b), tinv, in32, d1)
    tinv = each(lambda t, e: t + mm(e, t), tinv, te)
    run_side()
    te = each(lambda t, a, b: mm(t, a - b), tinv, m_ab, in32)
    tinv = each(lambda t, e: t + mm(e, t), tinv, te)
    w_hat = each(mm, tinv, a_bar)
    u_hat = each(mm, tinv, x1)
    run_side()
    for (rows, ls, ch), rb, w, rbar in zip(units, m_rb, w_hat, r_bar):
        q_out[rows, ls] =(rbar + mm(rb, w)).astype(BF16)
    rkr = _dot_split(jnp.concatenate([r[rows, ls] * kd[rows, ls] * rk[:, ls] for rows, ls, ch in units], axis=0),
                     head_ones)
    for n, ((rows, ls, ch), rb, u, o, v_u) in enumerate(zip(units, m_rb, u_hat, o_loc, vs)):
        o_out[rows, ls] =o + mm(rb, u) + rkr[n * C:(n + 1) * C] * v_u
    for (rows, ls, ch), w, be in zip(units, w_hat, b_end):
        g_out[rows, ls] =_compact(_dot_tn(w.astype(BF16), be), lanehead).astype(BF16)
    for (rows, ls, ch), v_u, u, ke, be in zip(units, vs, u_hat, k_end, b_end):
        zh = _dot_tn(jnp.concatenate([v_u, u], axis=0).astype(BF16), jnp.concatenate([ke, be], axis=0))
        h_out[rows, ls] =_compact(zh, lanehead)


def _rwkv_chunk_kernel(zp_ref, z_ref, zn_ref, mu_ref, kk_ref, ka_ref, wup_ref, w0_ref, aup_ref, a0_ref, gup_ref,
                       hsum_ref, rk_ref, gate_ref, ofwd_ref, q_out, o_out, g_out, h_out, pc_out,
                       qf_scr, of_scr, gf_scr, hf_scr, pf_scr, s_scr, *, n_ctx_blocks, n_blocks):
    i = pl.program_id(1)

    @pl.when(i == 0)
    def _():
        s_scr[...] = jnp.zeros_like(s_scr)

    has_prev = (i != 0) & (i != n_ctx_blocks)
    has_next = (i != n_ctx_blocks - 1) & (i != n_blocks - 1)
    z = z_ref[0]
    tm = z.shape[0]
    row = lax.broadcasted_iota(jnp.int32, (tm, 1), 0)
    z_first = jnp.where(has_prev, zp_ref[0, 7:8, :], 0.0)
    z_last = jnp.where(has_next, zn_ref[0, 0:1, :], 0.0)
    z_prev = jnp.where(row == 0, z_first, pltpu.roll(z, 1, axis=0))
    z_next = jnp.where(row == tm - 1, z_last, pltpu.roll(z, tm - 1, axis=0))
    mu = mu_ref[...]
    zs = z + mu[0:1] * (z_prev - z) + mu[1:2] * (z_next - z)
    w = MIX
    r = zs[:, 0:w]
    k = zs[:, w:2 * w]
    v = zs[:, 2 * w:3 * w]
    wd = jnp.tanh(zs[:, 3 * w:3 * w + LANES]).astype(BF16)
    ad = zs[:, 3 * w + LANES:3 * w + 2 * LANES].astype(BF16)
    gd = _sigmoid(zs[:, 3 * w + 2 * LANES:3 * w + 3 * LANES]).astype(BF16)
    kk = k * kk_ref[...]
    norm = jnp.sqrt(_dot_split(kk * kk, hsum_ref[...]))
    kn = kk / jnp.maximum(norm, 1e-12)
    gate_ref[0] = _dot(gd, gup_ref[...])
    def direction_inputs(d):
        logw = w0_ref[d:d + 1] + _dot(wd, wup_ref[d])
        lw = -jnp.exp(-_softplus(-logw) - 0.5)
        a = _sigmoid(a0_ref[d:d + 1] + _dot(ad, aup_ref[d]))
        return lw, k * (1.0 + (a - 1.0) * ka_ref[...]), a

    lw, kd, a = direction_inputs(0)
    _chunk_summaries(0, r, v, kn, lw, kd, a, rk_ref[...], (qf_scr, of_scr, gf_scr, hf_scr, pf_scr))
    lanehead = lax.broadcasted_iota(jnp.int32, (1, GROUP), 1) >> 6
    C = RWKV_CHUNK

    def forward_chunk(ch):
        def run():
            rows = slice(ch * C, (ch + 1) * C)
            for g in range(N_GROUPS):
                ls = slice(g * GROUP, (g + 1) * GROUP)
                ofwd_ref[0, rows, ls], s_scr[g] = _state_step(
                    s_scr[g], qf_scr[rows, ls], of_scr[rows, ls], gf_scr[rows, ls], hf_scr[rows, ls],
                    pf_scr[ch * 8:ch * 8 + 1, ls], lanehead)
        return run

    steps = [forward_chunk(ch) for ch in range(tm // C)]
    lw, kd, a = direction_inputs(1)
    _chunk_summaries(1, r, v, kn, lw, kd, a, rk_ref[...],
                     (q_out.at[0], o_out.at[0], g_out.at[0], h_out.at[0], pc_out.at[0]), side=steps[:4])
    for run in steps[4:]:
        run()


def _rwkv_chunks(zr, mu, k_k, k_a, w_up, w0, a_up, a0, g_up, hsum, r_k, n_ctx):
    B, TT, NR = zr.shape
    tm = TOKEN_BLOCK
    C = RWKV_CHUNK
    assert C == HEAD_DIM and tm % C == 0
    nb = TT // tm
    ncb = n_ctx // tm
    rows = tm // 8
    n_pc = 8 * (tm // C)
    tok = pl.BlockSpec((1, tm, MIX), lambda b, i: (b, i, 0))
    shape = lambda dt: jax.ShapeDtypeStruct((B, TT, MIX), dt)
    vmem = lambda n, dt: pltpu.VMEM((n, MIX), dt)
    return pl.pallas_call(
        functools.partial(_rwkv_chunk_kernel, n_ctx_blocks=ncb, n_blocks=nb),
        grid=(B, nb),
        in_specs=[pl.BlockSpec((1, 8, NR), lambda b, i: (b, jnp.maximum(i * rows - 1, 0), 0)),
                  pl.BlockSpec((1, tm, NR), lambda b, i: (b, i, 0)),
                  pl.BlockSpec((1, 8, NR), lambda b, i: (b, jnp.minimum((i + 1) * rows, TT // 8 - 1), 0)),
                  _resident((2, NR)), _resident((1, MIX)), _resident((1, MIX)),
                  _resident((2, LANES, MIX)), _resident((2, MIX)), _resident((2, LANES, MIX)), _resident((2, MIX)),
                  _resident((LANES, MIX)), _resident((MIX, MIX)), _resident((1, MIX))],
        out_specs=[tok, tok, tok, tok, tok, tok, pl.BlockSpec((1, n_pc, MIX), lambda b, i: (b, i, 0))],
        out_shape=[shape(F32), shape(F32), shape(BF16), shape(F32), shape(BF16), shape(F32),
                   jax.ShapeDtypeStruct((B, 8 * (TT // C), MIX), F32)],
        scratch_shapes=[vmem(tm, BF16), vmem(tm, F32), vmem(tm, BF16), vmem(tm, F32), vmem(n_pc, F32),
                        pltpu.VMEM((N_GROUPS, HEAD_DIM, GROUP), F32)],
        compiler_params=_params("parallel", "arbitrary"),
        name="rwkv_chunks",
    )(zr, zr, zr, mu, k_k, k_a, w_up, w0, a_up, a0, g_up, hsum, r_k)


def _state_step(st, q, o0, gm, hm, pc, lanehead):
    sb = st.astype(BF16)
    out = _dot_nt(q, _expand(sb, lanehead)) + o0
    return out, st * pc + _dot(sb, _expand(gm, lanehead)) + hm


def _rwkv_state_kernel(q_ref, o_ref, g_ref, h_ref, p_ref, out_ref, s_scr):
    @pl.when(pl.program_id(0) == 0)
    def _():
        s_scr[...] = jnp.zeros_like(s_scr)

    lanehead = lax.broadcasted_iota(jnp.int32, (1, GROUP), 1) >> 6
    for b in range(out_ref.shape[0]):
        for g in range(N_GROUPS):
            ls = slice(g * GROUP, (g + 1) * GROUP)
            out_ref[b, :, ls], s_scr[b, g] = _state_step(
                s_scr[b, g], q_ref[b, :, ls], o_ref[b, :, ls], g_ref[b, :, ls], h_ref[b, :, ls],
                p_ref[b, 0:1, ls], lanehead)


def _rwkv_state_reverse(q, o0, gm, hm, pc, n_ctx):
    B, TT, W = q.shape
    C = RWKV_CHUNK
    nch = TT // C
    ncc = n_ctx // C
    rev = lambda s: jnp.where(s < ncc, ncc - 1 - s, nch - 1 - (s - ncc))
    spec = lambda rows: pl.BlockSpec((B, rows, W), lambda s: (0, rev(s), 0))
    return pl.pallas_call(
        _rwkv_state_kernel,
        grid=(nch,),
        in_specs=[spec(C)] * 4 + [spec(8)],
        out_specs=spec(C),
        out_shape=jax.ShapeDtypeStruct((B, TT, W), F32),
        scratch_shapes=[pltpu.VMEM((B, N_GROUPS, HEAD_DIM, GROUP), F32)],
        compiler_params=_params("arbitrary"),
        name="rwkv_state",
    )(q, o0, gm, hm, pc)


def _merge_ffn_kernel(x_ref, mod_ref, ya_ref, yl_ref, of_ref, or_ref, gate_ref, lng_ref, lnb_ref, havg_ref, zg_ref,
                      proj_ref, wo_ref, g_ref, wgu_ref, wd_ref, out_ref):
    D = x_ref.shape[2]
    mod = mod_ref[0, 0]
    o = of_ref[0] + or_ref[0]
    mean = _dot_split(o, havg_ref[...])
    cen = o - mean
    var = _dot_split(cen * cen, havg_ref[...])
    yr = ((cen * lax.rsqrt(var + RWKV_LN_EPS)) * lng_ref[...] + lnb_ref[...]) * gate_ref[0]
    ys = (ya_ref[0], yl_ref[...], yr.astype(BF16))
    acc = None
    for b in range(3):
        term = _sigmoid(zg_ref[0, :, b * D:(b + 1) * D]) * _dot(ys[b], proj_ref[b])
        acc = term if acc is None else acc + term
    x = x_ref[0] + mod[0:1] * _dot(acc.astype(BF16), wo_ref[...])
    out_ref[0] = _ffn_half_step(x, mod[1:4], g_ref[...], wgu_ref, wd_ref)


def _merge_ffn(x, mod4, ya, yl, o_fwd, o_rev, gate, ln_g, ln_b, havg, zg, proj, w_o, g, w_gu, w_d, layer,
               n_ctx_blocks, latent_only):
    B, TT, D = x.shape
    F = w_d.shape[2]
    tm = TOKEN_BLOCK
    first = n_ctx_blocks if latent_only else 0
    n_out = TT // tm - first
    kind = lambda i: jnp.where(i + first < n_ctx_blocks, 0, 1)
    tok = lambda w: pl.BlockSpec((1, tm, w), lambda b, i: (b, i + first, 0))
    return pl.pallas_call(
        _merge_ffn_kernel,
        grid=(B, n_out),
        in_specs=[tok(D),
                  pl.BlockSpec((1, 1, 4, D), lambda b, i: (b, kind(i), 0, 0)),
                  tok(MIX),
                  pl.BlockSpec((tm, MIX), lambda b, i: (i + first, b)),
                  tok(MIX), tok(MIX), tok(MIX),
                  _resident((1, MIX)), _resident((1, MIX)), _resident((MIX, MIX)),
                  tok(3 * D),
                  _layer_slice((3, MIX, D), layer), _layer_slice((D, D), layer),
                  _resident((1, D)), _layer_slice((D, 2 * F), layer, 1), _layer_slice((F, D), layer, 1)],
        out_specs=pl.BlockSpec((1, tm, D), lambda b, i: (b, i, 0)),
        out_shape=jax.ShapeDtypeStruct((B, n_out * tm, D), F32),
        compiler_params=_params("parallel", "parallel"),
        name="merge_ffn",
    )(x, mod4, ya, yl, o_fwd, o_rev, gate, ln_g, ln_b, havg, zg, proj, w_o, g, w_gu, w_d)


def _rope_tables(n_ctx, n_tok):
    n_freq = HEAD_DIM // 4
    t = jnp.arange(n_tok)
    row = (t // GRID_W).astype(F32)
    col = (t % GRID_W).astype(F32)
    inv = ROPE_BASE ** (-jnp.arange(n_freq, dtype=F32) / n_freq)
    ang_r = row[:, None] * inv
    ang_c = col[:, None] * inv
    cos = jnp.concatenate([jnp.cos(ang_r)] * 2 + [jnp.cos(ang_c)] * 2, axis=-1)
    sin = jnp.concatenate([-jnp.sin(ang_r), jnp.sin(ang_r), -jnp.sin(ang_c), jnp.sin(ang_c)], axis=-1)
    cos = jnp.concatenate([jnp.ones((n_ctx, HEAD_DIM), F32), cos], axis=0)
    sin = jnp.concatenate([jnp.zeros((n_ctx, HEAD_DIM), F32), sin], axis=0)
    return jnp.tile(cos, (1, LANES // HEAD_DIM)), jnp.tile(sin, (1, LANES // HEAD_DIM))


def _block_diag(w):
    n, r, c = w.shape
    eye = jnp.eye(n, dtype=w.dtype)
    return (eye[:, None, :, None] * w[:, :, None, :]).reshape(n * r, n * c)


def _pad_rows(w, rank):
    z = jnp.zeros_like(w[0])
    return jnp.stack([jnp.concatenate([w[0], z], axis=0), jnp.concatenate([z, w[1]], axis=0)])


def kernel(x, c, ctx, c_ctx, ada_w, ada_b, norm_g, ffn_w_gu, ffn_w_d, w_in, attn_q_gain, attn_k_gain, attn_sink, lru_conv_w, lru_conv_b, lru_gate_w, lru_gate_b, lru_lambda, rwkv_mu, rwkv_w_up, rwkv_w0, rwkv_a_up, rwkv_a0, rwkv_g_up, rwkv_k_k, rwkv_k_a, rwkv_r_k, rwkv_ln_g, rwkv_ln_b, branch_proj, w_out):
    B, T, D = x.shape
    n_ctx = ctx.shape[1]
    L = ada_w.shape[0]
    TT = n_ctx + T
    assert n_ctx % TOKEN_BLOCK == 0 and T % TOKEN_BLOCK == 0 and T % GRID_W == 0
    ncb = n_ctx // TOKEN_BLOCK
    d_ff = ffn_w_d.shape[2]
    n_rwkv = rwkv_mu.shape[2]
    n_gate = 3 * D

    n_rows = -(-(B + 1) // 8) * 8
    c_rows = jnp.zeros((n_rows, D), F32).at[:B].set(c).at[B].set(c_ctx)
    ada = _ada_vectors(c_rows, ada_w, ada_b)
    mod_lat = ada[:, :B].reshape(L, B, 1, N_ADA, D)
    mod_ctx = jnp.broadcast_to(ada[:, B].reshape(L, 1, 1, N_ADA, D), (L, B, 1, N_ADA, D))
    mods = jnp.concatenate([mod_ctx, mod_lat], axis=2)

    cos_t, sin_t = _rope_tables(n_ctx, T)
    hsum = jnp.kron(jnp.eye(RWKV_HEADS, dtype=BF16), jnp.ones((HEAD_DIM, HEAD_DIM), BF16))
    havg = hsum / HEAD_DIM
    tile2 = lambda g: jnp.tile(g, LANES // HEAD_DIM).reshape(1, LANES)

    w_gu_b = ffn_w_gu.astype(BF16)
    w_d_b = ffn_w_d.astype(BF16)
    w_in_b = w_in.astype(BF16)
    proj_b = branch_proj.astype(BF16)
    w_out_b = w_out.astype(BF16)

    xs = x
    for l in range(L):
        xs = _ffn(xs, mods[l, :, :, 0:3], norm_g[l, 0].reshape(1, D), w_gu_b, w_d_b, l, ncb,
                  ctx=ctx if l == 0 else None)

        q, k, v, ux, ug, zr, zg = _inproj(
            xs, mods[l, :, :, 3:6], norm_g[l, 1].reshape(1, D), w_in_b, l,
            tile2(attn_q_gain[l]), tile2(attn_k_gain[l]), cos_t, sin_t, ncb, n_rwkv, n_gate)

        ya = _attention(q, k, v, attn_sink[l], n_ctx)

        ux3 = ux.reshape(TT, B, MIX)
        ug3 = ug.reshape(TT, B, MIX)
        lru_args = lambda d: (
            lru_conv_w[l], lru_conv_b[l].reshape(1, MIX),
            jnp.concatenate([_block_diag(lru_gate_w[l, d, 0]), _block_diag(lru_gate_w[l, d, 1])], axis=1).astype(BF16),
            lru_gate_b[l, d].reshape(1, 2 * MIX), lru_lambda[l, d].reshape(1, MIX))
        h_fwd = _lru_direction(ux3, *lru_args(0), n_ctx, False)
        yl = _lru_direction(ux3, *lru_args(1), n_ctx, True, h_fwd, ug3).reshape(TT, B * MIX)

        gate, o_fwd, *summaries = _rwkv_chunks(
            zr, rwkv_mu[l], rwkv_k_k[l].reshape(1, MIX), rwkv_k_a[l].reshape(1, MIX),
            _pad_rows(rwkv_w_up[l], rwkv_w_up.shape[2]).astype(BF16), rwkv_w0[l],
            _pad_rows(rwkv_a_up[l], rwkv_a_up.shape[2]).astype(BF16), rwkv_a0[l],
            rwkv_g_up[l].astype(BF16), hsum, rwkv_r_k[l].reshape(1, MIX), n_ctx)
        o_rev = _rwkv_state_reverse(*summaries, n_ctx)

        xs = _merge_ffn(xs, mods[l, :, :, 5:9], ya, yl, o_fwd, o_rev, gate,
                        rwkv_ln_g[l].reshape(1, MIX), rwkv_ln_b[l].reshape(1, MIX), havg, zg,
                        proj_b, w_out_b, norm_g[l, 2].reshape(1, D), w_gu_b, w_d_b, l, ncb,
                        latent_only=(l == L - 1))
    return xs
```

```python
import functools

import jax
import jax.numpy as jnp
from jax import lax
from jax.experimental import pallas as pl
from jax.experimental.pallas import tpu as pltpu

F32 = jnp.float32
BF16 = jnp.bfloat16
HIGHEST = lax.Precision.HIGHEST

N_ADA = 9
NORM_EPS = 1e-6
GRID_W = 64
HEAD_DIM = 64
ATTN_HEADS = 8
ATTN_KV_HEADS = 2
ATTN_BLOCK = 128
ROPE_BASE = 10000.0
NEG_INF = -1e30
LRU_C = 8.0
LRU_BLOCKS = 8
RWKV_HEADS = 8
RWKV_LN_EPS = 64e-5
MIX = 512
LANES = 128
TOKEN_BLOCK = 256
LRU_CHUNK = 128
RWKV_CHUNK = 64
VMEM_LIMIT = 56 * 1024 * 1024


def _params(*sem):
    return pltpu.CompilerParams(dimension_semantics=sem, vmem_limit_bytes=VMEM_LIMIT)


def _resident(shape):
    nd = len(shape)
    return pl.BlockSpec(shape, lambda *_: (0,) * nd, pipeline_mode=pl.Buffered(1))


def _layer_slice(shape, *index):
    nd = len(shape)
    return pl.BlockSpec((None,) * len(index) + tuple(shape), lambda *_: tuple(index) + (0,) * nd,
                        pipeline_mode=pl.Buffered(1))


def _dot(a, b):
    return jnp.dot(a, b, preferred_element_type=F32)


def _dot_f32(a, b):
    return jnp.dot(a, b, preferred_element_type=F32, precision=HIGHEST)


def _dot_split(x, w):
    x_hi = x.astype(BF16)
    x_lo = (x - x_hi.astype(F32)).astype(BF16)
    return _dot(x_hi, w) + _dot(x_lo, w)


def _dot_nt(a, b, precision=None):
    return lax.dot_general(a, b, (((1,), (1,)), ((), ())), preferred_element_type=F32, precision=precision)


def _dot_tn(a, b, precision=None):
    return lax.dot_general(a, b, (((0,), (0,)), ((), ())), preferred_element_type=F32, precision=precision)


def _sigmoid(x):
    return 0.5 * jnp.tanh(0.5 * x) + 0.5


def _softplus(x):
    return jnp.maximum(x, 0.0) + jnp.log(1.0 + jnp.exp(-jnp.abs(x)))


def _modulated_norm(x, g, shift, scale):
    y = x * lax.rsqrt(jnp.mean(x * x, axis=-1, keepdims=True) + NORM_EPS)
    return (y * g) * (1.0 + scale) + shift


def _ada_kernel(s_ref, w_ref, b_ref, o_ref):
    s = s_ref[...]
    s = (s * _sigmoid(s)).astype(BF16)
    o_ref[0] = _dot(s, w_ref[0].astype(BF16)) + b_ref[0]


def _ada_vectors(c_rows, ada_w, ada_b):
    L, D, N = ada_w.shape
    R = c_rows.shape[0]
    tn = 1024
    return pl.pallas_call(
        _ada_kernel,
        grid=(L, N // tn),
        in_specs=[pl.BlockSpec((R, D), lambda l, j: (0, 0)),
                  pl.BlockSpec((1, D, tn), lambda l, j: (l, 0, j)),
                  pl.BlockSpec((1, 1, tn), lambda l, j: (l, 0, j))],
        out_specs=pl.BlockSpec((1, R, tn), lambda l, j: (l, 0, j)),
        out_shape=jax.ShapeDtypeStruct((L, R, N), F32),
        compiler_params=_params("parallel", "parallel"),
        name="ada_vectors",
    )(c_rows, ada_w, ada_b.reshape(L, 1, N))


def _ffn_half_step(x, mod, g, wgu_ref, wd_ref):
    d_ff = wd_ref.shape[0]
    h = _modulated_norm(x, g, mod[0:1], mod[1:2]).astype(BF16)
    gt = _dot(h, wgu_ref[:, 0:d_ff])
    up = _dot(h, wgu_ref[:, d_ff:2 * d_ff])
    act = (gt * _sigmoid(gt) * up).astype(BF16)
    return x + 0.5 * mod[2:3] * _dot(act, wd_ref[...])


def _ffn_kernel(*refs, n_ctx_blocks, split_input):
    if split_input:
        c_ref, x_ref, mod_ref, g_ref, wgu_ref, wd_ref, o_ref = refs
        x = jnp.where(pl.program_id(1) < n_ctx_blocks, c_ref[0], x_ref[0])
    else:
        x_ref, mod_ref, g_ref, wgu_ref, wd_ref, o_ref = refs
        x = x_ref[0]
    o_ref[0] = _ffn_half_step(x, mod_ref[0, 0], g_ref[...], wgu_ref, wd_ref)


def _ffn(x, mod3, g, w_gu, w_d, layer, n_ctx_blocks, ctx=None):
    B, _, D = x.shape
    F = w_d.shape[2]
    tm = TOKEN_BLOCK
    kind = lambda i: jnp.where(i < n_ctx_blocks, 0, 1)
    if ctx is None:
        n_out = x.shape[1] // tm
        x_specs = [pl.BlockSpec((1, tm, D), lambda b, i: (b, i, 0))]
        xs = (x,)
    else:
        n_out = n_ctx_blocks + x.shape[1] // tm
        x_specs = [pl.BlockSpec((1, tm, D), lambda b, i: (b, jnp.minimum(i, n_ctx_blocks - 1), 0)),
                   pl.BlockSpec((1, tm, D), lambda b, i: (b, jnp.maximum(i - n_ctx_blocks, 0), 0))]
        xs = (ctx, x)
    return pl.pallas_call(
        functools.partial(_ffn_kernel, n_ctx_blocks=n_ctx_blocks, split_input=ctx is not None),
        grid=(B, n_out),
        in_specs=x_specs + [pl.BlockSpec((1, 1, 3, D), lambda b, i: (b, kind(i), 0, 0)),
                            _resident((1, D)), _layer_slice((D, 2 * F), layer, 0), _layer_slice((F, D), layer, 0)],
        out_specs=pl.BlockSpec((1, tm, D), lambda b, i: (b, i, 0)),
        out_shape=jax.ShapeDtypeStruct((B, n_out * tm, D), F32),
        compiler_params=_params("parallel", "parallel"),
        name="ffn",
    )(*xs, mod3, g, w_gu, w_d)


def _head_rms_rope(xs, gain, cos, sin, lane):
    sq = xs * xs
    lo = lane < HEAD_DIM
    s_lo = jnp.sum(jnp.where(lo, sq, 0.0), axis=-1, keepdims=True)
    s_hi = jnp.sum(jnp.where(lo, 0.0, sq), axis=-1, keepdims=True)
    ms = jnp.where(lo, s_lo, s_hi) * (1.0 / HEAD_DIM)
    y = xs * lax.rsqrt(ms + NORM_EPS) * gain
    first = (lane % 32) < 16
    partner = jnp.where(first, pltpu.roll(y, LANES - 16, axis=1), pltpu.roll(y, 16, axis=1))
    return y * cos + partner * sin


def _inproj_kernel(x_ref, mod_ref, g_ref, w_ref, qg_ref, kg_ref, cos_ref, sin_ref,
                   q_ref, k_ref, v_ref, ux_ref, ug_ref, zr_ref, zg_ref):
    x = x_ref[0]
    mod = mod_ref[0, 0]
    h = _modulated_norm(x, g_ref[...], mod[0:1], mod[1:2]).astype(BF16)
    nq = ATTN_HEADS * HEAD_DIM
    nkv = ATTN_KV_HEADS * HEAD_DIM
    lane = lax.broadcasted_iota(jnp.int32, (1, LANES), 1)
    cos = cos_ref[...]
    sin = sin_ref[...]
    za = _dot(h, w_ref[:, 0:nq + 2 * nkv])
    for s in range(nq // LANES):
        qs = _head_rms_rope(za[:, s * LANES:(s + 1) * LANES], qg_ref[...], cos, sin, lane)
        q_ref[0, :, s * LANES:(s + 1) * LANES] = (qs * HEAD_DIM ** -0.5).astype(BF16)
    k_ref[0] = _head_rms_rope(za[:, nq:nq + nkv], kg_ref[...], cos, sin, lane).astype(BF16)
    v_ref[0] = za[:, nq + nkv:nq + 2 * nkv].astype(BF16)
    c0 = nq + 2 * nkv
    zl = _dot(h, w_ref[:, c0:c0 + 2 * MIX])
    ux_ref[...] = zl[:, 0:MIX]
    ug_ref[...] = zl[:, MIX:2 * MIX]
    c1 = c0 + 2 * MIX
    nr = zr_ref.shape[2]
    zr_ref[0] = _dot(h, w_ref[:, c1:c1 + nr])
    c2 = c1 + nr
    zg_ref[0] = _dot(h, w_ref[:, c2:c2 + zg_ref.shape[2]])


def _inproj(x, mod3, g, w_in, layer, q_gain, k_gain, cos_t, sin_t, n_ctx_blocks, n_rwkv, n_gate):
    B, TT, D = x.shape
    tm = TOKEN_BLOCK
    nq = ATTN_HEADS * HEAD_DIM
    nkv = ATTN_KV_HEADS * HEAD_DIM
    kind = lambda i: jnp.where(i < n_ctx_blocks, 0, 1)
    tok = lambda w: pl.BlockSpec((1, tm, w), lambda b, i: (b, i, 0))
    tmajor = pl.BlockSpec((tm, MIX), lambda b, i: (i, b))
    return pl.pallas_call(
        _inproj_kernel,
        grid=(B, TT // tm),
        in_specs=[tok(D),
                  pl.BlockSpec((1, 1, 3, D), lambda b, i: (b, kind(i), 0, 0)),
                  _resident((1, D)), _layer_slice(w_in.shape[1:], layer), _resident((1, LANES)), _resident((1, LANES)),
                  pl.BlockSpec((tm, LANES), lambda b, i: (i, 0)),
                  pl.BlockSpec((tm, LANES), lambda b, i: (i, 0))],
        out_specs=[tok(nq), tok(nkv), tok(nkv), tmajor, tmajor, tok(n_rwkv), tok(n_gate)],
        out_shape=[jax.ShapeDtypeStruct((B, TT, nq), BF16),
                   jax.ShapeDtypeStruct((B, TT, nkv), BF16),
                   jax.ShapeDtypeStruct((B, TT, nkv), BF16),
                   jax.ShapeDtypeStruct((TT, B * MIX), F32),
                   jax.ShapeDtypeStruct((TT, B * MIX), F32),
                   jax.ShapeDtypeStruct((B, TT, n_rwkv), F32),
                   jax.ShapeDtypeStruct((B, TT, n_gate), F32)],
        compiler_params=_params("parallel", "parallel"),
        name="inproj",
    )(x, mod3, g, w_in, q_gain, k_gain, cos_t, sin_t)


def _attn_kernel(sink_ref, q_ref, kp_ref, kc_ref, kn_ref, kx_ref, vp_ref, vc_ref, vn_ref, vx_ref, o_ref,
                 *, n_ctx_blocks, n_blocks):
    i = pl.program_id(1)
    blk = ATTN_BLOCK
    n = HEAD_DIM
    rep = ATTN_HEADS // ATTN_KV_HEADS
    rows = rep * blk
    row = lax.broadcasted_iota(jnp.int32, (rows, 1), 0)

    def sink_column(g):
        col = jnp.full((rows, 1), sink_ref[g * rep], F32)
        for r in range(1, rep):
            col = jnp.where(row >= r * blk, sink_ref[g * rep + r], col)
        return col

    def attend(latent):
        if latent:
            kcat = jnp.concatenate([kp_ref[0], kc_ref[0], kn_ref[0], kx_ref[0]], axis=0)
            vcat = jnp.concatenate([vp_ref[0], vc_ref[0], vn_ref[0], vx_ref[0]], axis=0)
            far = 4 * blk
            off_prev = jnp.where(i >= n_ctx_blocks + 1, 0, far)
            off_next = jnp.where(i + 1 <= n_blocks - 1, 0, far)
            n_keys = kcat.shape[0]
            qi = lax.broadcasted_iota(jnp.int32, (rows, n_keys), 0) & (blk - 1)
            col = lax.broadcasted_iota(jnp.int32, (rows, n_keys), 1)
            mask = (((col >= blk) | (col >= qi + off_prev))
                    & ((col < 2 * blk) | (col >= 3 * blk) | (col - 2 * blk + off_next <= qi)))
        else:
            kcat = kx_ref[0]
            vcat = vx_ref[0]
            mask = None
        groups = range(ATTN_KV_HEADS)
        ss = []
        for g in groups:
            qg = jnp.concatenate([q_ref[0, :, (g * rep + r) * n:(g * rep + r + 1) * n] for r in range(rep)], axis=0)
            s = _dot_nt(qg, kcat[:, g * n:(g + 1) * n])
            ss.append(s if mask is None else jnp.where(mask, s, NEG_INF))
        sinks = [sink_column(g) for g in groups]
        ms = [jnp.maximum(jnp.max(s, axis=-1, keepdims=True), sk) for s, sk in zip(ss, sinks)]
        ps = [jnp.exp(s - m) for s, m in zip(ss, ms)]
        dens = [jnp.sum(p, axis=-1, keepdims=True) + jnp.exp(sk - m) for p, sk, m in zip(ps, sinks, ms)]
        outs = [_dot(p.astype(BF16), vcat[:, g * n:(g + 1) * n]) / den for g, p, den in zip(groups, ps, dens)]
        for g in groups:
            for r in range(rep):
                h = g * rep + r
                o_ref[0, :, h * n:(h + 1) * n] = outs[g][r * blk:(r + 1) * blk].astype(BF16)

    @pl.when(i >= n_ctx_blocks)
    def _():
        attend(True)

    @pl.when(i < n_ctx_blocks)
    def _():
        attend(False)


def _attention(q, k, v, sink, n_ctx):
    B, TT, nq = q.shape
    nkv = k.shape[2]
    blk = ATTN_BLOCK
    nb = TT // blk
    ncb = n_ctx // blk
    kv = lambda f: pl.BlockSpec((1, blk, nkv), lambda b, i: (b, f(i), 0))
    prev = kv(lambda i: jnp.maximum(i - 1, 0))
    cur = kv(lambda i: i)
    nxt = kv(lambda i: jnp.minimum(i + 1, nb - 1))
    cx = pl.BlockSpec((1, n_ctx, nkv), lambda b, i: (b, 0, 0))
    return pl.pallas_call(
        functools.partial(_attn_kernel, n_ctx_blocks=ncb, n_blocks=nb),
        grid=(B, nb),
        in_specs=[pl.BlockSpec(memory_space=pltpu.SMEM),
                  pl.BlockSpec((1, blk, nq), lambda b, i: (b, i, 0)),
                  prev, cur, nxt, cx, prev, cur, nxt, cx],
        out_specs=pl.BlockSpec((1, blk, nq), lambda b, i: (b, i, 0)),
        out_shape=jax.ShapeDtypeStruct((B, TT, nq), BF16),
        compiler_params=_params("parallel", "parallel"),
        name="attention",
    )(sink, q, k, k, k, k, v, v, v, v)


def _lru_chunk_index(s, n_ctx_chunks, n_chunks, reverse):
    if not reverse:
        return s
    return jnp.where(s < n_ctx_chunks, n_ctx_chunks - 1 - s, n_chunks - 1 - (s - n_ctx_chunks))


def _lru_kernel(up_ref, u_ref, un_ref, cw_ref, cb_ref, gw_ref, gb_ref, lam_ref, *rest,
                n_ctx_chunks, n_chunks, reverse):
    if reverse:
        hf_ref, ug_ref, y_ref, a_scr, b_scr, h_scr = rest
    else:
        h_out_ref, a_scr, b_scr, h_scr = rest
    s = pl.program_id(0)
    c = _lru_chunk_index(s, n_ctx_chunks, n_chunks, reverse)
    tc, nb, w = u_ref.shape

    @pl.when(s == 0)
    def _():
        h_scr[...] = jnp.zeros_like(h_scr)

    has_prev = (c != 0) & (c != n_ctx_chunks)
    has_next = (c != n_ctx_chunks - 1) & (c != n_chunks - 1)
    u = u_ref[...]
    ext = jnp.concatenate([jnp.where(has_prev, up_ref[...], 0.0), u, jnp.where(has_next, un_ref[...], 0.0)], axis=0)
    cw = cw_ref[...]
    xc = cb_ref[...] + cw[0:1] * ext[0:tc] + cw[1:2] * ext[1:tc + 1] + cw[2:3] * ext[2:tc + 2] + cw[3:4] * ext[3:tc + 3]

    gates = _dot(xc.reshape(tc * nb, w).astype(BF16), gw_ref[...]).reshape(tc, nb, 2 * w) + gb_ref[...]
    r = _sigmoid(gates[:, :, 0:w])
    ig = _sigmoid(gates[:, :, w:2 * w])
    log_a = (-LRU_C * _softplus(-lam_ref[...])) * r
    a = jnp.exp(log_a)
    a_scr[...] = a
    b_scr[...] = jnp.sqrt(-jnp.tanh(log_a) * (a * a + 1.0)) * (ig * xc)

    def step(t, h):
        tt = tc - 1 - t if reverse else t
        h = a_scr[tt] * h + b_scr[tt]
        b_scr[tt] = h
        return h

    h_scr[...] = lax.fori_loop(0, tc, step, h_scr[...], unroll=8)

    if reverse:
        ug = ug_ref[...]
        gelu = 0.5 * ug * (1.0 + jnp.tanh(0.7978845608028654 * (ug + 0.044715 * ug * ug * ug)))
        y_ref[...] = ((hf_ref[...] + b_scr[...]) * gelu).astype(BF16)
    else:
        h_out_ref[...] = b_scr[...]


def _lru_direction(ux3, conv_w, conv_b, gate_w, gate_b, lam, n_ctx, reverse, h_fwd=None, ug3=None):
    TT, B, W = ux3.shape
    tc = LRU_CHUNK
    nch = TT // tc
    ncc = n_ctx // tc
    cidx = lambda s: _lru_chunk_index(s, ncc, nch, reverse)
    chunk = pl.BlockSpec((tc, B, W), lambda s: (cidx(s), 0, 0))
    in_specs = [pl.BlockSpec((2, B, W), lambda s: (jnp.maximum(cidx(s) * (tc // 2) - 1, 0), 0, 0)),
                chunk,
                pl.BlockSpec((1, B, W), lambda s: (jnp.minimum((cidx(s) + 1) * tc, TT - 1), 0, 0)),
                _resident((4, W)), _resident((1, W)), _resident((W, 2 * W)), _resident((1, 2 * W)),
                _resident((1, W))]
    args = [ux3, ux3, ux3, conv_w, conv_b, gate_w, gate_b, lam]
    if reverse:
        in_specs += [chunk, chunk]
        args += [h_fwd, ug3]
        out_dtype = BF16
    else:
        out_dtype = F32
    return pl.pallas_call(
        functools.partial(_lru_kernel, n_ctx_chunks=ncc, n_chunks=nch, reverse=reverse),
        grid=(nch,),
        in_specs=in_specs,
        out_specs=chunk,
        out_shape=jax.ShapeDtypeStruct((TT, B, W), out_dtype),
        scratch_shapes=[pltpu.VMEM((tc, B, W), F32), pltpu.VMEM((tc, B, W), F32), pltpu.VMEM((B, W), F32)],
        compiler_params=_params("arbitrary"),
        name="lru_rev" if reverse else "lru_fwd",
    )(*args)


GROUP_HEADS = LANES // HEAD_DIM
GROUP = GROUP_HEADS * HEAD_DIM
N_GROUPS = RWKV_HEADS // GROUP_HEADS


def _expand(y, lanehead):
    return jnp.concatenate([jnp.where(lanehead == h, y, jnp.zeros_like(y)) for h in range(GROUP_HEADS)], axis=0)


def _compact(z, lanehead):
    n = HEAD_DIM
    out = jnp.where(lanehead == 0, z[0:n], 0.0)
    for h in range(1, GROUP_HEADS):
        out = out + jnp.where(lanehead == h, z[h * n:(h + 1) * n], 0.0)
    return out


def _chunk_algebra(d, r, v, kn, lw, kd, a, rk, outs=None, state=None, side=()):
    side = list(side)

    def run_side():
        if side:
            side.pop(0)()

    C = RWKV_CHUNK
    tm = r.shape[0]
    sgn = 1 - 2 * d
    ri = lax.broadcasted_iota(jnp.int32, (C, C), 0)
    ci = lax.broadcasted_iota(jnp.int32, (C, C), 1)
    tri = ((ri - ci) * sgn >= 0).astype(BF16)
    t = lax.broadcasted_iota(jnp.int32, (C, GROUP), 0)
    lane = lax.broadcasted_iota(jnp.int32, (C, GROUP), 1)
    i = lane & (C - 1)
    eye = t == i
    same16 = (t >> 4) == (i >> 4)
    same32 = (t >> 5) == (i >> 5)
    lanehead = lax.broadcasted_iota(jnp.int32, (1, GROUP), 1) >> 6
    t2 = lax.broadcasted_iota(jnp.int32, (2 * C, GROUP), 0)
    i2 = lax.broadcasted_iota(jnp.int32, (2 * C, GROUP), 1) & (C - 1)
    mask2 = ((t2 & (C - 1)) - i2) * sgn > jnp.where(t2 < C, 0, -1)
    rowhead = lax.broadcasted_iota(jnp.int32, (GROUP, GROUP), 0) >> 6
    colhead = lax.broadcasted_iota(jnp.int32, (GROUP, GROUP), 1) >> 6
    head_ones = (rowhead == colhead).astype(BF16)

    def mm(x, y):
        return _dot(x.astype(BF16), _expand(y.astype(BF16), lanehead))

    def each(f, *lists):
        return [f(*xs) for xs in zip(*lists)]

    n_ch = tm // C
    units = [(slice(ch * C, (ch + 1) * C), slice(g * GROUP, (g + 1) * GROUP), ch)
             for ch in range(n_ch) for g in range(N_GROUPS)]
    l_in_all, l_tot_all = [], []
    for ch in range(n_ch):
        lw_c = lw[ch * C:(ch + 1) * C]
        lw_hi = lw_c.astype(BF16)
        rest = lw_c - lw_hi.astype(F32)
        lw_mid = rest.astype(BF16)
        lw_lo = (rest - lw_mid.astype(F32)).astype(BF16)
        l_in_all.append(_dot(tri, lw_hi) + _dot(tri, lw_mid) + _dot(tri, lw_lo))
        l_tot_all.append(jnp.sum(lw_c, axis=0, keepdims=True))
    p_tot_all = [jnp.exp(l) for l in l_tot_all]

    a_bar, r_bar, lhs, b_inv, k_inv, b_end, k_end, vs = [], [], [], [], [], [], [], []
    for rows, ls, ch in units:
        kn_u = kn[rows, ls]
        k_u = kd[rows, ls]
        l_in = l_in_all[ch][:, ls]
        beta = kn_u * a[rows, ls]
        p_inv = jnp.exp(-l_in)
        p_end = jnp.exp(l_tot_all[ch][:, ls] - l_in)
        a_bar.append(-kn_u * jnp.exp(l_in - lw[rows, ls]))
        r_bar.append(r[rows, ls] * jnp.exp(l_in))
        lhs.append(jnp.concatenate([a_bar[-1], r_bar[-1]], axis=0).astype(BF16))
        b_inv.append((beta * p_inv).astype(BF16))
        k_inv.append((k_u * p_inv).astype(BF16))
        b_end.append((beta * p_end).astype(BF16))
        k_end.append((k_u * p_end).astype(BF16))
        vs.append(v[rows, ls])

    mb = each(lambda l, y: jnp.where(mask2, _dot_nt(l, _expand(y, lanehead)), 0.0), lhs, b_inv)
    mk = each(lambda l, y: jnp.where(mask2, _dot_nt(l, _expand(y, lanehead)), 0.0), lhs, k_inv)
    m_ab = [m[0:C] for m in mb]
    m_rb = [m[C:2 * C] for m in mb]
    run_side()
    d1 = [jnp.where(same16, m, 0.0) for m in m_ab]
    in32 = [jnp.where(same32, m, 0.0) for m in m_ab]
    d2 = each(mm, d1, d1)
    run_side()
    d4 = each(mm, d2, d2)
    tinv = [jnp.where(eye, 1.0, m) for m in d1]
    tinv = each(lambda t, p: t + mm(t, p), tinv, d2)
    run_side()
    d8 = each(mm, d4, d4)
    tinv = each(lambda t, p: t + mm(t, p), tinv, d4)
    run_side()
    mkv = each(mm, mk, vs)
    x1 = [m[0:C] for m in mkv]
    o_loc = [m[C:2 * C] for m in mkv]
    tinv = each(lambda t, p: t + mm(t, p), tinv, d8)
    run_side()
    te = each(lambda t, a, b: mm(t, a - b), tinv, in32, d1)
    run_side()
    tinv = each(lambda t, e: t + mm(e, t), tinv, te)
    run_side()
    te = each(lambda t, a, b: mm(t, a - b), tinv, m_ab, in32)
    run_side()
    tinv = each(lambda t, e: t + mm(e, t), tinv, te)
    run_side()
    rkr = _dot_split(jnp.concatenate([r[rows, ls] * kd[rows, ls] * rk[:, ls] for rows, ls, ch in units], axis=0),
                     head_ones)

    if state is not None:
        o_ref, s_scr = state

        def chunk_scan(ch):
            ns = [ch * N_GROUPS + g for g in range(N_GROUPS)]
            cell = {}

            def from_state():
                cell["st"] = [s_scr[g] for g in range(N_GROUPS)]
                cell["fs"] = [_dot_nt(lhs[n], _expand(st.astype(BF16), lanehead)) for n, st in zip(ns, cell["st"])]

            def solve():
                cell["u"] = [mm(tinv[n], fs[0:C] + x1[n]) for n, fs in zip(ns, cell["fs"])]

            def emit():
                for g, n in enumerate(ns):
                    rows, ls, _ = units[n]
                    u = cell["u"][g]
                    o_ref[rows, ls] = (cell["fs"][g][C:2 * C] + o_loc[n] + mm(m_rb[n], u)
                                       + rkr[n * C:(n + 1) * C] * vs[n])
                    zh = _dot_tn(jnp.concatenate([vs[n], u], axis=0).astype(BF16),
                                 jnp.concatenate([k_end[n], b_end[n]], axis=0))
                    s_scr[g] = cell["st"][g] * p_tot_all[ch][:, ls] + _compact(zh, lanehead)

            return [from_state, solve, emit]

        while side:
            run_side()
        return [step for ch in range(n_ch) for step in chunk_scan(ch)]

    q_out, o_out, g_out, h_out, pc_out = outs
    for ch in range(n_ch):
        pc_out[ch * 8:(ch + 1) * 8, :] = jnp.broadcast_to(p_tot_all[ch], (8, lw.shape[1]))
    w_hat = each(mm, tinv, a_bar)
    run_side()
    u_hat = each(mm, tinv, x1)
    run_side()
    for (rows, ls, ch), rb, w, rbar in zip(units, m_rb, w_hat, r_bar):
        q_out[rows, ls] = (rbar + mm(rb, w)).astype(BF16)
    run_side()
    for n, ((rows, ls, ch), rb, u, o, v_u) in enumerate(zip(units, m_rb, u_hat, o_loc, vs)):
        o_out[rows, ls] = o + mm(rb, u) + rkr[n * C:(n + 1) * C] * v_u
    run_side()
    for (rows, ls, ch), w, be in zip(units, w_hat, b_end):
        g_out[rows, ls] = _compact(_dot_tn(w.astype(BF16), be), lanehead).astype(BF16)
    run_side()
    for (rows, ls, ch), v_u, u, ke, be in zip(units, vs, u_hat, k_end, b_end):
        zh = _dot_tn(jnp.concatenate([v_u, u], axis=0).astype(BF16), jnp.concatenate([ke, be], axis=0))
        h_out[rows, ls] = _compact(zh, lanehead)
    while side:
        run_side()
    return []


def _rwkv_chunk_kernel(zp_ref, z_ref, zn_ref, mu_ref, kk_ref, ka_ref, wup_ref, w0_ref, aup_ref, a0_ref, gup_ref,
                       hsum_ref, rk_ref, gate_ref, ofwd_ref, q_out, o_out, g_out, h_out, pc_out, s_scr,
                       *, n_ctx_blocks, n_blocks):
    i = pl.program_id(1)

    @pl.when(i == 0)
    def _():
        s_scr[...] = jnp.zeros_like(s_scr)

    has_prev = (i != 0) & (i != n_ctx_blocks)
    has_next = (i != n_ctx_blocks - 1) & (i != n_blocks - 1)
    z = z_ref[0]
    tm = z.shape[0]
    row = lax.broadcasted_iota(jnp.int32, (tm, 1), 0)
    z_first = jnp.where(has_prev, zp_ref[0, 7:8, :], 0.0)
    z_last = jnp.where(has_next, zn_ref[0, 0:1, :], 0.0)
    z_prev = jnp.where(row == 0, z_first, pltpu.roll(z, 1, axis=0))
    z_next = jnp.where(row == tm - 1, z_last, pltpu.roll(z, tm - 1, axis=0))
    mu = mu_ref[...]
    zs = (1.0 - mu[0:1] - mu[1:2]) * z + mu[0:1] * z_prev + mu[1:2] * z_next
    w = MIX
    r = zs[:, 0:w]
    k = zs[:, w:2 * w]
    v = zs[:, 2 * w:3 * w]
    wd = jnp.tanh(zs[:, 3 * w:3 * w + LANES]).astype(BF16)
    ad = zs[:, 3 * w + LANES:3 * w + 2 * LANES].astype(BF16)
    gd = _sigmoid(zs[:, 3 * w + 2 * LANES:3 * w + 3 * LANES]).astype(BF16)
    kk = k * kk_ref[...]
    norm = jnp.sqrt(_dot_split(kk * kk, hsum_ref[...]))
    kn = kk / jnp.maximum(norm, 1e-12)
    gate_ref[0] = _dot(gd, gup_ref[...])
    def direction_inputs(d):
        logw = w0_ref[d:d + 1] + _dot(wd, wup_ref[d])
        lw = -jnp.exp(-_softplus(-logw) - 0.5)
        a = _sigmoid(a0_ref[d:d + 1] + _dot(ad, aup_ref[d]))
        return lw, k * (1.0 + (a - 1.0) * ka_ref[...]), a

    lw, kd, a = direction_inputs(0)
    scan_steps = _chunk_algebra(0, r, v, kn, lw, kd, a, rk_ref[...], state=(ofwd_ref.at[0], s_scr))
    lw, kd, a = direction_inputs(1)
    _chunk_algebra(1, r, v, kn, lw, kd, a, rk_ref[...],
                   outs=(q_out.at[0], o_out.at[0], g_out.at[0], h_out.at[0], pc_out.at[0]), side=scan_steps)


def _rwkv_chunks(zr, mu, k_k, k_a, w_up, w0, a_up, a0, g_up, hsum, r_k, n_ctx):
    B, TT, NR = zr.shape
    tm = TOKEN_BLOCK
    C = RWKV_CHUNK
    assert C == HEAD_DIM and tm % C == 0
    nb = TT // tm
    ncb = n_ctx // tm
    rows = tm // 8
    n_pc = 8 * (tm // C)
    tok = pl.BlockSpec((1, tm, MIX), lambda b, i: (b, i, 0))
    shape = lambda dt: jax.ShapeDtypeStruct((B, TT, MIX), dt)
    return pl.pallas_call(
        functools.partial(_rwkv_chunk_kernel, n_ctx_blocks=ncb, n_blocks=nb),
        grid=(B, nb),
        in_specs=[pl.BlockSpec((1, 8, NR), lambda b, i: (b, jnp.maximum(i * rows - 1, 0), 0)),
                  pl.BlockSpec((1, tm, NR), lambda b, i: (b, i, 0)),
                  pl.BlockSpec((1, 8, NR), lambda b, i: (b, jnp.minimum((i + 1) * rows, TT // 8 - 1), 0)),
                  _resident((2, NR)), _resident((1, MIX)), _resident((1, MIX)),
                  _resident((2, LANES, MIX)), _resident((2, MIX)), _resident((2, LANES, MIX)), _resident((2, MIX)),
                  _resident((LANES, MIX)), _resident((MIX, MIX)), _resident((1, MIX))],
        out_specs=[tok, tok, tok, tok, tok, tok, pl.BlockSpec((1, n_pc, MIX), lambda b, i: (b, i, 0))],
        out_shape=[shape(F32), shape(F32), shape(BF16), shape(F32), shape(BF16), shape(F32),
                   jax.ShapeDtypeStruct((B, 8 * (TT // C), MIX), F32)],
        scratch_shapes=[pltpu.VMEM((N_GROUPS, HEAD_DIM, GROUP), F32)],
        compiler_params=_params("parallel", "arbitrary"),
        name="rwkv_chunks",
    )(zr, zr, zr, mu, k_k, k_a, w_up, w0, a_up, a0, g_up, hsum, r_k)


def _state_step(st, q, o0, gm, hm, pc, lanehead):
    sb = st.astype(BF16)
    out = _dot_nt(q, _expand(sb, lanehead)) + o0
    return out, st * pc + _dot(sb, _expand(gm, lanehead)) + hm


def _rwkv_state_kernel(q_ref, o_ref, g_ref, h_ref, p_ref, out_ref, s_scr):
    @pl.when(pl.program_id(0) == 0)
    def _():
        s_scr[...] = jnp.zeros_like(s_scr)

    lanehead = lax.broadcasted_iota(jnp.int32, (1, GROUP), 1) >> 6
    for b in range(out_ref.shape[0]):
        for g in range(N_GROUPS):
            ls = slice(g * GROUP, (g + 1) * GROUP)
            out_ref[b, :, ls], s_scr[b, g] = _state_step(
                s_scr[b, g], q_ref[b, :, ls], o_ref[b, :, ls], g_ref[b, :, ls], h_ref[b, :, ls],
                p_ref[b, 0:1, ls], lanehead)


def _rwkv_state_reverse(q, o0, gm, hm, pc, n_ctx):
    B, TT, W = q.shape
    C = RWKV_CHUNK
    nch = TT // C
    ncc = n_ctx // C
    rev = lambda s: jnp.where(s < ncc, ncc - 1 - s, nch - 1 - (s - ncc))
    spec = lambda rows: pl.BlockSpec((B, rows, W), lambda s: (0, rev(s), 0))
    return pl.pallas_call(
        _rwkv_state_kernel,
        grid=(nch,),
        in_specs=[spec(C)] * 4 + [spec(8)],
        out_specs=spec(C),
        out_shape=jax.ShapeDtypeStruct((B, TT, W), F32),
        scratch_shapes=[pltpu.VMEM((B, N_GROUPS, HEAD_DIM, GROUP), F32)],
        compiler_params=_params("arbitrary"),
        name="rwkv_state",
    )(q, o0, gm, hm, pc)


def _merge_ffn_kernel(x_ref, mod_ref, ya_ref, yl_ref, of_ref, or_ref, gate_ref, lng_ref, lnb_ref, havg_ref, zg_ref,
                      proj_ref, wo_ref, g_ref, wgu_ref, wd_ref, out_ref):
    D = x_ref.shape[2]
    mod = mod_ref[0, 0]
    o = of_ref[0] + or_ref[0]
    mean = _dot_split(o, havg_ref[...])
    cen = o - mean
    var = _dot_split(cen * cen, havg_ref[...])
    yr = ((cen * lax.rsqrt(var + RWKV_LN_EPS)) * lng_ref[...] + lnb_ref[...]) * gate_ref[0]
    ys = (ya_ref[0], yl_ref[...], yr.astype(BF16))
    acc = None
    for b in range(3):
        term = _sigmoid(zg_ref[0, :, b * D:(b + 1) * D]) * _dot(ys[b], proj_ref[b])
        acc = term if acc is None else acc + term
    x = x_ref[0] + mod[0:1] * _dot(acc.astype(BF16), wo_ref[...])
    out_ref[0] = _ffn_half_step(x, mod[1:4], g_ref[...], wgu_ref, wd_ref)


def _merge_ffn(x, mod4, ya, yl, o_fwd, o_rev, gate, ln_g, ln_b, havg, zg, proj, w_o, g, w_gu, w_d, layer,
               n_ctx_blocks, latent_only):
    B, TT, D = x.shape
    F = w_d.shape[2]
    tm = TOKEN_BLOCK
    first = n_ctx_blocks if latent_only else 0
    n_out = TT // tm - first
    kind = lambda i: jnp.where(i + first < n_ctx_blocks, 0, 1)
    tok = lambda w: pl.BlockSpec((1, tm, w), lambda b, i: (b, i + first, 0))
    return pl.pallas_call(
        _merge_ffn_kernel,
        grid=(B, n_out),
        in_specs=[tok(D),
                  pl.BlockSpec((1, 1, 4, D), lambda b, i: (b, kind(i), 0, 0)),
                  tok(MIX),
                  pl.BlockSpec((tm, MIX), lambda b, i: (i + first, b)),
                  tok(MIX), tok(MIX), tok(MIX),
                  _resident((1, MIX)), _resident((1, MIX)), _resident((MIX, MIX)),
                  tok(3 * D),
                  _layer_slice((3, MIX, D), layer), _layer_slice((D, D), layer),
                  _resident((1, D)), _layer_slice((D, 2 * F), layer, 1), _layer_slice((F, D), layer, 1)],
        out_specs=pl.BlockSpec((1, tm, D), lambda b, i: (b, i, 0)),
        out_shape=jax.ShapeDtypeStruct((B, n_out * tm, D), F32),
        compiler_params=_params("parallel", "parallel"),
        name="merge_ffn",
    )(x, mod4, ya, yl, o_fwd, o_rev, gate, ln_g, ln_b, havg, zg, proj, w_o, g, w_gu, w_d)


def _rope_tables(n_ctx, n_tok):
    n_freq = HEAD_DIM // 4
    t = jnp.arange(n_tok)
    row = (t // GRID_W).astype(F32)
    col = (t % GRID_W).astype(F32)
    inv = ROPE_BASE ** (-jnp.arange(n_freq, dtype=F32) / n_freq)
    ang_r = row[:, None] * inv
    ang_c = col[:, None] * inv
    cos = jnp.concatenate([jnp.cos(ang_r)] * 2 + [jnp.cos(ang_c)] * 2, axis=-1)
    sin = jnp.concatenate([-jnp.sin(ang_r), jnp.sin(ang_r), -jnp.sin(ang_c), jnp.sin(ang_c)], axis=-1)
    cos = jnp.concatenate([jnp.ones((n_ctx, HEAD_DIM), F32), cos], axis=0)
    sin = jnp.concatenate([jnp.zeros((n_ctx, HEAD_DIM), F32), sin], axis=0)
    return jnp.tile(cos, (1, LANES // HEAD_DIM)), jnp.tile(sin, (1, LANES // HEAD_DIM))


def _block_diag(w):
    n, r, c = w.shape
    eye = jnp.eye(n, dtype=w.dtype)
    return (eye[:, None, :, None] * w[:, :, None, :]).reshape(n * r, n * c)


def _pad_rows(w, rank):
    z = jnp.zeros_like(w[0])
    return jnp.stack([jnp.concatenate([w[0], z], axis=0), jnp.concatenate([z, w[1]], axis=0)])


def kernel(x, c, ctx, c_ctx, ada_w, ada_b, norm_g, ffn_w_gu, ffn_w_d, w_in, attn_q_gain, attn_k_gain, attn_sink, lru_conv_w, lru_conv_b, lru_gate_w, lru_gate_b, lru_lambda, rwkv_mu, rwkv_w_up, rwkv_w0, rwkv_a_up, rwkv_a0, rwkv_g_up, rwkv_k_k, rwkv_k_a, rwkv_r_k, rwkv_ln_g, rwkv_ln_b, branch_proj, w_out):
    B, T, D = x.shape
    n_ctx = ctx.shape[1]
    L = ada_w.shape[0]
    TT = n_ctx + T
    assert n_ctx % TOKEN_BLOCK == 0 and T % TOKEN_BLOCK == 0 and T % GRID_W == 0
    ncb = n_ctx // TOKEN_BLOCK
    d_ff = ffn_w_d.shape[2]
    n_rwkv = rwkv_mu.shape[2]
    n_gate = 3 * D

    n_rows = -(-(B + 1) // 8) * 8
    c_rows = jnp.zeros((n_rows, D), F32).at[:B].set(c).at[B].set(c_ctx)
    ada = _ada_vectors(c_rows, ada_w, ada_b)
    mod_lat = ada[:, :B].reshape(L, B, 1, N_ADA, D)
    mod_ctx = jnp.broadcast_to(ada[:, B].reshape(L, 1, 1, N_ADA, D), (L, B, 1, N_ADA, D))
    mods = jnp.concatenate([mod_ctx, mod_lat], axis=2)

    cos_t, sin_t = _rope_tables(n_ctx, T)
    hsum = jnp.kron(jnp.eye(RWKV_HEADS, dtype=BF16), jnp.ones((HEAD_DIM, HEAD_DIM), BF16))
    havg = hsum / HEAD_DIM
    tile2 = lambda g: jnp.tile(g, LANES // HEAD_DIM).reshape(1, LANES)

    w_gu_b = ffn_w_gu.astype(BF16)
    w_d_b = ffn_w_d.astype(BF16)
    w_in_b = w_in.astype(BF16)
    proj_b = branch_proj.astype(BF16)
    w_out_b = w_out.astype(BF16)

    xs = x
    for l in range(L):
        xs = _ffn(xs, mods[l, :, :, 0:3], norm_g[l, 0].reshape(1, D), w_gu_b, w_d_b, l, ncb,
                  ctx=ctx if l == 0 else None)

        q, k, v, ux, ug, zr, zg = _inproj(
            xs, mods[l, :, :, 3:6], norm_g[l, 1].reshape(1, D), w_in_b, l,
            tile2(attn_q_gain[l]), tile2(attn_k_gain[l]), cos_t, sin_t, ncb, n_rwkv, n_gate)

        ya = _attention(q, k, v, attn_sink[l], n_ctx)

        ux3 = ux.reshape(TT, B, MIX)
        ug3 = ug.reshape(TT, B, MIX)
        lru_args = lambda d: (
            lru_conv_w[l], lru_conv_b[l].reshape(1, MIX),
            jnp.concatenate([_block_diag(lru_gate_w[l, d, 0]), _block_diag(lru_gate_w[l, d, 1])], axis=1).astype(BF16),
            lru_gate_b[l, d].reshape(1, 2 * MIX), lru_lambda[l, d].reshape(1, MIX))
        h_fwd = _lru_direction(ux3, *lru_args(0), n_ctx, False)
        yl = _lru_direction(ux3, *lru_args(1), n_ctx, True, h_fwd, ug3).reshape(TT, B * MIX)

        gate, o_fwd, *summaries = _rwkv_chunks(
            zr, rwkv_mu[l], rwkv_k_k[l].reshape(1, MIX), rwkv_k_a[l].reshape(1, MIX),
            _pad_rows(rwkv_w_up[l], rwkv_w_up.shape[2]).astype(BF16), rwkv_w0[l],
            _pad_rows(rwkv_a_up[l], rwkv_a_up.shape[2]).astype(BF16), rwkv_a0[l],
            rwkv_g_up[l].astype(BF16), hsum, rwkv_r_k[l].reshape(1, MIX), n_ctx)
        o_rev = _rwkv_state_reverse(*summaries, n_ctx)

        xs = _merge_ffn(xs, mods[l, :, :, 5:9], ya, yl, o_fwd, o_rev, gate,
                        rwkv_ln_g[l].reshape(1, MIX), rwkv_ln_b[l].reshape(1, MIX), havg, zg,
                        proj_b, w_out_b, norm_g[l, 2].reshape(1, D), w_gu_b, w_d_b, l, ncb,
                        latent_only=(l == L - 1))
    return xs
```

```python
import functools

import jax
import jax.numpy as jnp
from jax import lax
from jax.experimental import pallas as pl
from jax.experimental.pallas import tpu as pltpu

F32 = jnp.float32
BF16 = jnp.bfloat16
HIGHEST = lax.Precision.HIGHEST

N_ADA = 9
NORM_EPS = 1e-6
GRID_W = 64
HEAD_DIM = 64
ATTN_HEADS = 8
ATTN_KV_HEADS = 2
ATTN_BLOCK = 128
ROPE_BASE = 10000.0
NEG_INF = -1e30
LRU_C = 8.0
LRU_BLOCKS = 8
RWKV_HEADS = 8
RWKV_LN_EPS = 64e-5
MIX = 512
LANES = 128
TOKEN_BLOCK = 256
LRU_CHUNK = 128
RWKV_CHUNK = 64
VMEM_LIMIT = 56 * 1024 * 1024


def _params(*sem):
    return pltpu.CompilerParams(dimension_semantics=sem, vmem_limit_bytes=VMEM_LIMIT)


def _resident(shape):
    nd = len(shape)
    return pl.BlockSpec(shape, lambda *_: (0,) * nd, pipeline_mode=pl.Buffered(1))


def _layer_slice(shape, *index):
    nd = len(shape)
    return pl.BlockSpec((None,) * len(index) + tuple(shape), lambda *_: tuple(index) + (0,) * nd,
                        pipeline_mode=pl.Buffered(1))


def _dot(a, b):
    return jnp.dot(a, b, preferred_element_type=F32)


def _dot_f32(a, b):
    return jnp.dot(a, b, preferred_element_type=F32, precision=HIGHEST)


def _dot_split(x, w):
    x_hi = x.astype(BF16)
    x_lo = (x - x_hi.astype(F32)).astype(BF16)
    return _dot(x_hi, w) + _dot(x_lo, w)


def _dot_nt(a, b, precision=None):
    return lax.dot_general(a, b, (((1,), (1,)), ((), ())), preferred_element_type=F32, precision=precision)


def _dot_tn(a, b, precision=None):
    return lax.dot_general(a, b, (((0,), (0,)), ((), ())), preferred_element_type=F32, precision=precision)


def _sigmoid(x):
    return 0.5 * jnp.tanh(0.5 * x) + 0.5


def _softplus(x):
    return jnp.maximum(x, 0.0) + jnp.log(1.0 + jnp.exp(-jnp.abs(x)))


def _modulated_norm(x, g, shift, scale):
    y = x * lax.rsqrt(jnp.mean(x * x, axis=-1, keepdims=True) + NORM_EPS)
    return (y * g) * (1.0 + scale) + shift


def _ada_kernel(s_ref, w_ref, b_ref, o_ref):
    s = s_ref[...]
    s = (s * _sigmoid(s)).astype(BF16)
    o_ref[0] = _dot(s, w_ref[0].astype(BF16)) + b_ref[0]


def _ada_vectors(c_rows, ada_w, ada_b):
    L, D, N = ada_w.shape
    R = c_rows.shape[0]
    tn = 1024
    return pl.pallas_call(
        _ada_kernel,
        grid=(L, N // tn),
        in_specs=[pl.BlockSpec((R, D), lambda l, j: (0, 0)),
                  pl.BlockSpec((1, D, tn), lambda l, j: (l, 0, j)),
                  pl.BlockSpec((1, 1, tn), lambda l, j: (l, 0, j))],
        out_specs=pl.BlockSpec((1, R, tn), lambda l, j: (l, 0, j)),
        out_shape=jax.ShapeDtypeStruct((L, R, N), F32),
        compiler_params=_params("parallel", "parallel"),
        name="ada_vectors",
    )(c_rows, ada_w, ada_b.reshape(L, 1, N))


def _ffn_half_step(x, mod, g, wgu_ref, wd_ref):
    d_ff = wd_ref.shape[0]
    h = _modulated_norm(x, g, mod[0:1], mod[1:2]).astype(BF16)
    gt = _dot(h, wgu_ref[:, 0:d_ff])
    up = _dot(h, wgu_ref[:, d_ff:2 * d_ff])
    act = (gt * _sigmoid(gt) * up).astype(BF16)
    return x + 0.5 * mod[2:3] * _dot(act, wd_ref[...])


def _ffn_kernel(*refs, n_ctx_blocks, split_input):
    if split_input:
        c_ref, x_ref, mod_ref, g_ref, wgu_ref, wd_ref, o_ref = refs
        x = jnp.where(pl.program_id(1) < n_ctx_blocks, c_ref[0], x_ref[0])
    else:
        x_ref, mod_ref, g_ref, wgu_ref, wd_ref, o_ref = refs
        x = x_ref[0]
    o_ref[0] = _ffn_half_step(x, mod_ref[0, 0], g_ref[...], wgu_ref, wd_ref)


def _ffn(x, mod3, g, w_gu, w_d, layer, n_ctx_blocks, ctx=None):
    B, _, D = x.shape
    F = w_d.shape[2]
    tm = TOKEN_BLOCK
    kind = lambda i: jnp.where(i < n_ctx_blocks, 0, 1)
    if ctx is None:
        n_out = x.shape[1] // tm
        x_specs = [pl.BlockSpec((1, tm, D), lambda b, i: (b, i, 0))]
        xs = (x,)
    else:
        n_out = n_ctx_blocks + x.shape[1] // tm
        x_specs = [pl.BlockSpec((1, tm, D), lambda b, i: (b, jnp.minimum(i, n_ctx_blocks - 1), 0)),
                   pl.BlockSpec((1, tm, D), lambda b, i: (b, jnp.maximum(i - n_ctx_blocks, 0), 0))]
        xs = (ctx, x)
    return pl.pallas_call(
        functools.partial(_ffn_kernel, n_ctx_blocks=n_ctx_blocks, split_input=ctx is not None),
        grid=(B, n_out),
        in_specs=x_specs + [pl.BlockSpec((1, 1, 3, D), lambda b, i: (b, kind(i), 0, 0)),
                            _resident((1, D)), _layer_slice((D, 2 * F), layer, 0), _layer_slice((F, D), layer, 0)],
        out_specs=pl.BlockSpec((1, tm, D), lambda b, i: (b, i, 0)),
        out_shape=jax.ShapeDtypeStruct((B, n_out * tm, D), F32),
        compiler_params=_params("parallel", "parallel"),
        name="ffn",
    )(*xs, mod3, g, w_gu, w_d)


def _head_rms_rope(xs, gain, cos, sin, lane):
    sq = xs * xs
    lo = lane < HEAD_DIM
    s_lo = jnp.sum(jnp.where(lo, sq, 0.0), axis=-1, keepdims=True)
    s_hi = jnp.sum(jnp.where(lo, 0.0, sq), axis=-1, keepdims=True)
    ms = jnp.where(lo, s_lo, s_hi) * (1.0 / HEAD_DIM)
    y = xs * lax.rsqrt(ms + NORM_EPS) * gain
    first = (lane % 32) < 16
    partner = jnp.where(first, pltpu.roll(y, LANES - 16, axis=1), pltpu.roll(y, 16, axis=1))
    return y * cos + partner * sin


def _inproj_kernel(x_ref, mod_ref, g_ref, w_ref, qg_ref, kg_ref, cos_ref, sin_ref,
                   q_ref, k_ref, v_ref, ux_ref, ug_ref, zr_ref, zg_ref):
    x = x_ref[0]
    mod = mod_ref[0, 0]
    h = _modulated_norm(x, g_ref[...], mod[0:1], mod[1:2]).astype(BF16)
    nq = ATTN_HEADS * HEAD_DIM
    nkv = ATTN_KV_HEADS * HEAD_DIM
    lane = lax.broadcasted_iota(jnp.int32, (1, LANES), 1)
    cos = cos_ref[...]
    sin = sin_ref[...]
    za = _dot(h, w_ref[:, 0:nq + 2 * nkv])
    for s in range(nq // LANES):
        qs = _head_rms_rope(za[:, s * LANES:(s + 1) * LANES], qg_ref[...], cos, sin, lane)
        q_ref[0, :, s * LANES:(s + 1) * LANES] = (qs * HEAD_DIM ** -0.5).astype(BF16)
    k_ref[0] = _head_rms_rope(za[:, nq:nq + nkv], kg_ref[...], cos, sin, lane).astype(BF16)
    v_ref[0] = za[:, nq + nkv:nq + 2 * nkv].astype(BF16)
    c0 = nq + 2 * nkv
    zl = _dot(h, w_ref[:, c0:c0 + 2 * MIX])
    ux_ref[...] = zl[:, 0:MIX]
    ug_ref[...] = zl[:, MIX:2 * MIX]
    c1 = c0 + 2 * MIX
    nr = zr_ref.shape[2]
    zr_ref[0] = _dot(h, w_ref[:, c1:c1 + nr])
    c2 = c1 + nr
    zg_ref[0] = _dot(h, w_ref[:, c2:c2 + zg_ref.shape[2]])


def _inproj(x, mod3, g, w_in, layer, q_gain, k_gain, cos_t, sin_t, n_ctx_blocks, n_rwkv, n_gate):
    B, TT, D = x.shape
    tm = TOKEN_BLOCK
    nq = ATTN_HEADS * HEAD_DIM
    nkv = ATTN_KV_HEADS * HEAD_DIM
    kind = lambda i: jnp.where(i < n_ctx_blocks, 0, 1)
    tok = lambda w: pl.BlockSpec((1, tm, w), lambda b, i: (b, i, 0))
    tmajor = pl.BlockSpec((tm, MIX), lambda b, i: (i, b))
    return pl.pallas_call(
        _inproj_kernel,
        grid=(B, TT // tm),
        in_specs=[tok(D),
                  pl.BlockSpec((1, 1, 3, D), lambda b, i: (b, kind(i), 0, 0)),
                  _resident((1, D)), _layer_slice(w_in.shape[1:], layer), _resident((1, LANES)), _resident((1, LANES)),
                  pl.BlockSpec((tm, LANES), lambda b, i: (i, 0)),
                  pl.BlockSpec((tm, LANES), lambda b, i: (i, 0))],
        out_specs=[tok(nq), tok(nkv), tok(nkv), tmajor, tmajor, tok(n_rwkv), tok(n_gate)],
        out_shape=[jax.ShapeDtypeStruct((B, TT, nq), BF16),
                   jax.ShapeDtypeStruct((B, TT, nkv), BF16),
                   jax.ShapeDtypeStruct((B, TT, nkv), BF16),
                   jax.ShapeDtypeStruct((TT, B * MIX), F32),
                   jax.ShapeDtypeStruct((TT, B * MIX), F32),
                   jax.ShapeDtypeStruct((B, TT, n_rwkv), F32),
                   jax.ShapeDtypeStruct((B, TT, n_gate), F32)],
        compiler_params=_params("parallel", "parallel"),
        name="inproj",
    )(x, mod3, g, w_in, q_gain, k_gain, cos_t, sin_t)


def _attn_kernel(sink_ref, q_ref, kp_ref, kc_ref, kn_ref, kx_ref, vp_ref, vc_ref, vn_ref, vx_ref, o_ref,
                 *, n_ctx_blocks, n_blocks):
    i = pl.program_id(1)
    blk = ATTN_BLOCK
    n = HEAD_DIM
    rep = ATTN_HEADS // ATTN_KV_HEADS
    rows = rep * blk
    row = lax.broadcasted_iota(jnp.int32, (rows, 1), 0)

    def sink_column(g):
        col = jnp.full((rows, 1), sink_ref[g * rep], F32)
        for r in range(1, rep):
            col = jnp.where(row >= r * blk, sink_ref[g * rep + r], col)
        return col

    def attend(latent):
        if latent:
            kcat = jnp.concatenate([kp_ref[0], kc_ref[0], kn_ref[0], kx_ref[0]], axis=0)
            vcat = jnp.concatenate([vp_ref[0], vc_ref[0], vn_ref[0], vx_ref[0]], axis=0)
            far = 4 * blk
            off_prev = jnp.where(i >= n_ctx_blocks + 1, 0, far)
            off_next = jnp.where(i + 1 <= n_blocks - 1, 0, far)
            n_keys = kcat.shape[0]
            qi = lax.broadcasted_iota(jnp.int32, (rows, n_keys), 0) & (blk - 1)
            col = lax.broadcasted_iota(jnp.int32, (rows, n_keys), 1)
            mask = (((col >= blk) | (col >= qi + off_prev))
                    & ((col < 2 * blk) | (col >= 3 * blk) | (col - 2 * blk + off_next <= qi)))
        else:
            kcat = kx_ref[0]
            vcat = vx_ref[0]
            mask = None
        groups = range(ATTN_KV_HEADS)
        ss = []
        for g in groups:
            qg = jnp.concatenate([q_ref[0, :, (g * rep + r) * n:(g * rep + r + 1) * n] for r in range(rep)], axis=0)
            s = _dot_nt(qg, kcat[:, g * n:(g + 1) * n])
            ss.append(s if mask is None else jnp.where(mask, s, NEG_INF))
        sinks = [sink_column(g) for g in groups]
        ms = [jnp.maximum(jnp.max(s, axis=-1, keepdims=True), sk) for s, sk in zip(ss, sinks)]
        ps = [jnp.exp(s - m) for s, m in zip(ss, ms)]
        dens = [jnp.sum(p, axis=-1, keepdims=True) + jnp.exp(sk - m) for p, sk, m in zip(ps, sinks, ms)]
        outs = [_dot(p.astype(BF16), vcat[:, g * n:(g + 1) * n]) / den for g, p, den in zip(groups, ps, dens)]
        for g in groups:
            for r in range(rep):
                h = g * rep + r
                o_ref[0, :, h * n:(h + 1) * n] = outs[g][r * blk:(r + 1) * blk].astype(BF16)

    @pl.when(i >= n_ctx_blocks)
    def _():
        attend(True)

    @pl.when(i < n_ctx_blocks)
    def _():
        attend(False)


def _attention(q, k, v, sink, n_ctx):
    B, TT, nq = q.shape
    nkv = k.shape[2]
    blk = ATTN_BLOCK
    nb = TT // blk
    ncb = n_ctx // blk
    kv = lambda f: pl.BlockSpec((1, blk, nkv), lambda b, i: (b, f(i), 0))
    prev = kv(lambda i: jnp.maximum(i - 1, 0))
    cur = kv(lambda i: i)
    nxt = kv(lambda i: jnp.minimum(i + 1, nb - 1))
    cx = pl.BlockSpec((1, n_ctx, nkv), lambda b, i: (b, 0, 0))
    return pl.pallas_call(
        functools.partial(_attn_kernel, n_ctx_blocks=ncb, n_blocks=nb),
        grid=(B, nb),
        in_specs=[pl.BlockSpec(memory_space=pltpu.SMEM),
                  pl.BlockSpec((1, blk, nq), lambda b, i: (b, i, 0)),
                  prev, cur, nxt, cx, prev, cur, nxt, cx],
        out_specs=pl.BlockSpec((1, blk, nq), lambda b, i: (b, i, 0)),
        out_shape=jax.ShapeDtypeStruct((B, TT, nq), BF16),
        compiler_params=_params("parallel", "parallel"),
        name="attention",
    )(sink, q, k, k, k, k, v, v, v, v)


def _lru_chunk_index(s, n_ctx_chunks, n_chunks, reverse):
    if not reverse:
        return s
    return jnp.where(s < n_ctx_chunks, n_ctx_chunks - 1 - s, n_chunks - 1 - (s - n_ctx_chunks))


def _lru_kernel(up_ref, u_ref, un_ref, cw_ref, cb_ref, gw_ref, gb_ref, lam_ref, *rest,
                n_ctx_chunks, n_chunks, reverse):
    if reverse:
        hf_ref, ug_ref, y_ref, a_scr, b_scr, h_scr = rest
    else:
        h_out_ref, a_scr, b_scr, h_scr = rest
    s = pl.program_id(0)
    c = _lru_chunk_index(s, n_ctx_chunks, n_chunks, reverse)
    tc, nb, w = u_ref.shape

    @pl.when(s == 0)
    def _():
        h_scr[...] = jnp.zeros_like(h_scr)

    has_prev = (c != 0) & (c != n_ctx_chunks)
    has_next = (c != n_ctx_chunks - 1) & (c != n_chunks - 1)
    u = u_ref[...]
    ext = jnp.concatenate([jnp.where(has_prev, up_ref[...], 0.0), u, jnp.where(has_next, un_ref[...], 0.0)], axis=0)
    cw = cw_ref[...]
    xc = cb_ref[...] + cw[0:1] * ext[0:tc] + cw[1:2] * ext[1:tc + 1] + cw[2:3] * ext[2:tc + 2] + cw[3:4] * ext[3:tc + 3]

    half_z = _dot(xc.reshape(tc * nb, w).astype(BF16), gw_ref[...]).reshape(tc, nb, 2 * w) + gb_ref[...]
    t_r = jnp.tanh(half_z[:, :, 0:w])
    t_i = jnp.tanh(half_z[:, :, w:2 * w])
    half_c = (-0.5 * LRU_C) * _softplus(-lam_ref[...])
    log_a = half_c * t_r + half_c
    a = jnp.exp(log_a)
    a_scr[...] = a
    y = -jnp.tanh(log_a) * (a * a + 1.0)
    root = jnp.where(y > 0.0, y * lax.rsqrt(y), 0.0)
    half_x = 0.5 * xc
    b_scr[...] = root * (t_i * half_x + half_x)

    def step(t, h):
        tt = tc - 1 - t if reverse else t
        h = a_scr[tt] * h + b_scr[tt]
        b_scr[tt] = h
        return h

    h_scr[...] = lax.fori_loop(0, tc, step, h_scr[...], unroll=8)

    if reverse:
        ug = ug_ref[...]
        gelu = 0.5 * ug * (1.0 + jnp.tanh(0.7978845608028654 * (ug + 0.044715 * ug * ug * ug)))
        y_ref[...] = ((hf_ref[...] + b_scr[...]) * gelu).astype(BF16)
    else:
        h_out_ref[...] = b_scr[...]


def _lru_direction(ux3, conv_w, conv_b, gate_w, gate_b, lam, n_ctx, reverse, h_fwd=None, ug3=None):
    TT, B, W = ux3.shape
    tc = LRU_CHUNK
    nch = TT // tc
    ncc = n_ctx // tc
    cidx = lambda s: _lru_chunk_index(s, ncc, nch, reverse)
    chunk = pl.BlockSpec((tc, B, W), lambda s: (cidx(s), 0, 0))
    in_specs = [pl.BlockSpec((2, B, W), lambda s: (jnp.maximum(cidx(s) * (tc // 2) - 1, 0), 0, 0)),
                chunk,
                pl.BlockSpec((1, B, W), lambda s: (jnp.minimum((cidx(s) + 1) * tc, TT - 1), 0, 0)),
                _resident((4, W)), _resident((1, W)), _resident((W, 2 * W)), _resident((1, 2 * W)),
                _resident((1, W))]
    args = [ux3, ux3, ux3, conv_w, conv_b, gate_w, gate_b, lam]
    if reverse:
        in_specs += [chunk, chunk]
        args += [h_fwd, ug3]
        out_dtype = BF16
    else:
        out_dtype = F32
    return pl.pallas_call(
        functools.partial(_lru_kernel, n_ctx_chunks=ncc, n_chunks=nch, reverse=reverse),
        grid=(nch,),
        in_specs=in_specs,
        out_specs=chunk,
        out_shape=jax.ShapeDtypeStruct((TT, B, W), out_dtype),
        scratch_shapes=[pltpu.VMEM((tc, B, W), F32), pltpu.VMEM((tc, B, W), F32), pltpu.VMEM((B, W), F32)],
        compiler_params=_params("arbitrary"),
        name="lru_rev" if reverse else "lru_fwd",
    )(*args)


GROUP_HEADS = LANES // HEAD_DIM
GROUP = GROUP_HEADS * HEAD_DIM
N_GROUPS = RWKV_HEADS // GROUP_HEADS


def _expand(y, lanehead):
    return jnp.concatenate([jnp.where(lanehead == h, y, jnp.zeros_like(y)) for h in range(GROUP_HEADS)], axis=0)


def _compact(z, lanehead):
    n = HEAD_DIM
    out = jnp.where(lanehead == 0, z[0:n], 0.0)
    for h in range(1, GROUP_HEADS):
        out = out + jnp.where(lanehead == h, z[h * n:(h + 1) * n], 0.0)
    return out


def _chunk_algebra(d, r, v, kn, lw, kd, a, rk, outs=None, state=None, side=()):
    side = list(side)

    def run_side():
        if side:
            side.pop(0)()

    C = RWKV_CHUNK
    tm = r.shape[0]
    sgn = 1 - 2 * d
    ri = lax.broadcasted_iota(jnp.int32, (C, C), 0)
    ci = lax.broadcasted_iota(jnp.int32, (C, C), 1)
    tri = ((ri - ci) * sgn >= 0).astype(BF16)
    t = lax.broadcasted_iota(jnp.int32, (C, GROUP), 0)
    lane = lax.broadcasted_iota(jnp.int32, (C, GROUP), 1)
    i = lane & (C - 1)
    eye = t == i
    same16 = (t >> 4) == (i >> 4)
    same32 = (t >> 5) == (i >> 5)
    lanehead = lax.broadcasted_iota(jnp.int32, (1, GROUP), 1) >> 6
    t2 = lax.broadcasted_iota(jnp.int32, (2 * C, GROUP), 0)
    i2 = lax.broadcasted_iota(jnp.int32, (2 * C, GROUP), 1) & (C - 1)
    mask2 = ((t2 & (C - 1)) - i2) * sgn > jnp.where(t2 < C, 0, -1)
    rowhead = lax.broadcasted_iota(jnp.int32, (GROUP, GROUP), 0) >> 6
    colhead = lax.broadcasted_iota(jnp.int32, (GROUP, GROUP), 1) >> 6
    head_ones = (rowhead == colhead).astype(BF16)

    def mm(x, y):
        return _dot(x.astype(BF16), _expand(y.astype(BF16), lanehead))

    def each(f, *lists):
        return [f(*xs) for xs in zip(*lists)]

    n_ch = tm // C
    units = [(slice(ch * C, (ch + 1) * C), slice(g * GROUP, (g + 1) * GROUP), ch)
             for ch in range(n_ch) for g in range(N_GROUPS)]
    l_in_all, l_tot_all = [], []
    for ch in range(n_ch):
        lw_c = lw[ch * C:(ch + 1) * C]
        lw_hi = lw_c.astype(BF16)
        rest = lw_c - lw_hi.astype(F32)
        lw_mid = rest.astype(BF16)
        lw_lo = (rest - lw_mid.astype(F32)).astype(BF16)
        l_in_all.append(_dot(tri, lw_hi) + _dot(tri, lw_mid) + _dot(tri, lw_lo))
        l_tot_all.append(jnp.sum(lw_c, axis=0, keepdims=True))
    p_tot_all = [jnp.exp(l) for l in l_tot_all]

    a_bar, r_bar, lhs, b_inv, k_inv, b_end, k_end, vs = [], [], [], [], [], [], [], []
    for rows, ls, ch in units:
        kn_u = kn[rows, ls]
        k_u = kd[rows, ls]
        l_in = l_in_all[ch][:, ls]
        beta = kn_u * a[rows, ls]
        p_inv = jnp.exp(-l_in)
        p_end = jnp.exp(l_tot_all[ch][:, ls] - l_in)
        a_bar.append(-kn_u * jnp.exp(l_in - lw[rows, ls]))
        r_bar.append(r[rows, ls] * jnp.exp(l_in))
        lhs.append(jnp.concatenate([a_bar[-1], r_bar[-1]], axis=0).astype(BF16))
        b_inv.append((beta * p_inv).astype(BF16))
        k_inv.append((k_u * p_inv).astype(BF16))
        b_end.append((beta * p_end).astype(BF16))
        k_end.append((k_u * p_end).astype(BF16))
        vs.append(v[rows, ls])

    mb = each(lambda l, y: jnp.where(mask2, _dot_nt(l, _expand(y, lanehead)), 0.0), lhs, b_inv)
    mk = each(lambda l, y: jnp.where(mask2, _dot_nt(l, _expand(y, lanehead)), 0.0), lhs, k_inv)
    m_ab = [m[0:C] for m in mb]
    m_rb = [m[C:2 * C] for m in mb]
    run_side()
    d1 = [jnp.where(same16, m, 0.0) for m in m_ab]
    in32 = [jnp.where(same32, m, 0.0) for m in m_ab]
    d2 = each(mm, d1, d1)
    run_side()
    d4 = each(mm, d2, d2)
    tinv = [jnp.where(eye, 1.0, m) for m in d1]
    tinv = each(lambda t, p: t + mm(t, p), tinv, d2)
    run_side()
    d8 = each(mm, d4, d4)
    tinv = each(lambda t, p: t + mm(t, p), tinv, d4)
    run_side()
    mkv = each(mm, mk, vs)
    x1 = [m[0:C] for m in mkv]
    o_loc = [m[C:2 * C] for m in mkv]
    tinv = each(lambda t, p: t + mm(t, p), tinv, d8)
    run_side()
    te = each(lambda t, a, b: mm(t, a - b), tinv, in32, d1)
    run_side()
    tinv = each(lambda t, e: t + mm(e, t), tinv, te)
    run_side()
    te = each(lambda t, a, b: mm(t, a - b), tinv, m_ab, in32)
    run_side()
    tinv = each(lambda t, e: t + mm(e, t), tinv, te)
    run_side()
    rkr = _dot_split(jnp.concatenate([r[rows, ls] * kd[rows, ls] * rk[:, ls] for rows, ls, ch in units], axis=0),
                     head_ones)

    if state is not None:
        o_ref, s_scr = state

        def chunk_scan(ch):
            ns = [ch * N_GROUPS + g for g in range(N_GROUPS)]
            cell = {}

            def from_state():
                cell["st"] = [s_scr[g] for g in range(N_GROUPS)]
                cell["fs"] = [_dot_nt(lhs[n], _expand(st.astype(BF16), lanehead)) for n, st in zip(ns, cell["st"])]

            def solve():
                cell["u"] = [mm(tinv[n], fs[0:C] + x1[n]) for n, fs in zip(ns, cell["fs"])]

            def emit():
                for g, n in enumerate(ns):
                    rows, ls, _ = units[n]
                    u = cell["u"][g]
                    o_ref[rows, ls] = (cell["fs"][g][C:2 * C] + o_loc[n] + mm(m_rb[n], u)
                                       + rkr[n * C:(n + 1) * C] * vs[n])
                    zh = _dot_tn(jnp.concatenate([vs[n], u], axis=0).astype(BF16),
                                 jnp.concatenate([k_end[n], b_end[n]], axis=0))
                    s_scr[g] = cell["st"][g] * p_tot_all[ch][:, ls] + _compact(zh, lanehead)

            return [from_state, solve, emit]

        while side:
            run_side()
        return [step for ch in range(n_ch) for step in chunk_scan(ch)]

    q_out, o_out, g_out, h_out, pc_out = outs
    for ch in range(n_ch):
        pc_out[ch * 8:(ch + 1) * 8, :] = jnp.broadcast_to(p_tot_all[ch], (8, lw.shape[1]))
    w_hat = each(mm, tinv, a_bar)
    run_side()
    u_hat = each(mm, tinv, x1)
    run_side()
    for (rows, ls, ch), rb, w, rbar in zip(units, m_rb, w_hat, r_bar):
        q_out[rows, ls] = (rbar + mm(rb, w)).astype(BF16)
    run_side()
    for n, ((rows, ls, ch), rb, u, o, v_u) in enumerate(zip(units, m_rb, u_hat, o_loc, vs)):
        o_out[rows, ls] = o + mm(rb, u) + rkr[n * C:(n + 1) * C] * v_u
    run_side()
    for (rows, ls, ch), w, be in zip(units, w_hat, b_end):
        g_out[rows, ls] = _compact(_dot_tn(w.astype(BF16), be), lanehead).astype(BF16)
    run_side()
    for (rows, ls, ch), v_u, u, ke, be in zip(units, vs, u_hat, k_end, b_end):
        zh = _dot_tn(jnp.concatenate([v_u, u], axis=0).astype(BF16), jnp.concatenate([ke, be], axis=0))
        h_out[rows, ls] = _compact(zh, lanehead)
    while side:
        run_side()
    return []


def _rwkv_chunk_kernel(zp_ref, z_ref, zn_ref, mu_ref, kk_ref, ka_ref, wup_ref, w0_ref, aup_ref, a0_ref, gup_ref,
                       hsum_ref, rk_ref, gate_ref, ofwd_ref, q_out, o_out, g_out, h_out, pc_out, s_scr,
                       *, n_ctx_blocks, n_blocks):
    i = pl.program_id(1)

    @pl.when(i == 0)
    def _():
        s_scr[...] = jnp.zeros_like(s_scr)

    has_prev = (i != 0) & (i != n_ctx_blocks)
    has_next = (i != n_ctx_blocks - 1) & (i != n_blocks - 1)
    z = z_ref[0]
    tm = z.shape[0]
    row = lax.broadcasted_iota(jnp.int32, (tm, 1), 0)
    z_first = jnp.where(has_prev, zp_ref[0, 7:8, :], 0.0)
    z_last = jnp.where(has_next, zn_ref[0, 0:1, :], 0.0)
    z_prev = jnp.where(row == 0, z_first, pltpu.roll(z, 1, axis=0))
    z_next = jnp.where(row == tm - 1, z_last, pltpu.roll(z, tm - 1, axis=0))
    mu = mu_ref[...]
    zs = (1.0 - mu[0:1] - mu[1:2]) * z + mu[0:1] * z_prev + mu[1:2] * z_next
    w = MIX
    r = zs[:, 0:w]
    k = zs[:, w:2 * w]
    v = zs[:, 2 * w:3 * w]
    wd = jnp.tanh(zs[:, 3 * w:3 * w + LANES]).astype(BF16)
    ad = zs[:, 3 * w + LANES:3 * w + 2 * LANES].astype(BF16)
    gd = _sigmoid(zs[:, 3 * w + 2 * LANES:3 * w + 3 * LANES]).astype(BF16)
    kk = k * kk_ref[...]
    norm = jnp.sqrt(_dot_split(kk * kk, hsum_ref[...]))
    kn = kk / jnp.maximum(norm, 1e-12)
    gate_ref[0] = _dot(gd, gup_ref[...])
    def direction_inputs(d):
        logw = w0_ref[d:d + 1] + _dot(wd, wup_ref[d])
        lw = -jnp.exp(-_softplus(-logw) - 0.5)
        a = _sigmoid(a0_ref[d:d + 1] + _dot(ad, aup_ref[d]))
        return lw, k * (1.0 + (a - 1.0) * ka_ref[...]), a

    lw, kd, a = direction_inputs(0)
    scan_steps = _chunk_algebra(0, r, v, kn, lw, kd, a, rk_ref[...], state=(ofwd_ref.at[0], s_scr))
    lw, kd, a = direction_inputs(1)
    _chunk_algebra(1, r, v, kn, lw, kd, a, rk_ref[...],
                   outs=(q_out.at[0], o_out.at[0], g_out.at[0], h_out.at[0], pc_out.at[0]), side=scan_steps)


def _rwkv_chunks(zr, mu, k_k, k_a, w_up, w0, a_up, a0, g_up, hsum, r_k, n_ctx):
    B, TT, NR = zr.shape
    tm = TOKEN_BLOCK
    C = RWKV_CHUNK
    assert C == HEAD_DIM and tm % C == 0
    nb = TT // tm
    ncb = n_ctx // tm
    rows = tm // 8
    n_pc = 8 * (tm // C)
    tok = pl.BlockSpec((1, tm, MIX), lambda b, i: (b, i, 0))
    shape = lambda dt: jax.ShapeDtypeStruct((B, TT, MIX), dt)
    return pl.pallas_call(
        functools.partial(_rwkv_chunk_kernel, n_ctx_blocks=ncb, n_blocks=nb),
        grid=(B, nb),
        in_specs=[pl.BlockSpec((1, 8, NR), lambda b, i: (b, jnp.maximum(i * rows - 1, 0), 0)),
                  pl.BlockSpec((1, tm, NR), lambda b, i: (b, i, 0)),
                  pl.BlockSpec((1, 8, NR), lambda b, i: (b, jnp.minimum((i + 1) * rows, TT // 8 - 1), 0)),
                  _resident((2, NR)), _resident((1, MIX)), _resident((1, MIX)),
                  _resident((2, LANES, MIX)), _resident((2, MIX)), _resident((2, LANES, MIX)), _resident((2, MIX)),
                  _resident((LANES, MIX)), _resident((MIX, MIX)), _resident((1, MIX))],
        out_specs=[tok, tok, tok, tok, tok, tok, pl.BlockSpec((1, n_pc, MIX), lambda b, i: (b, i, 0))],
        out_shape=[shape(F32), shape(F32), shape(BF16), shape(F32), shape(BF16), shape(F32),
                   jax.ShapeDtypeStruct((B, 8 * (TT // C), MIX), F32)],
        scratch_shapes=[pltpu.VMEM((N_GROUPS, HEAD_DIM, GROUP), F32)],
        compiler_params=_params("parallel", "arbitrary"),
        name="rwkv_chunks",
    )(zr, zr, zr, mu, k_k, k_a, w_up, w0, a_up, a0, g_up, hsum, r_k)


def _state_step(st, q, o0, gm, hm, pc, lanehead):
    sb = st.astype(BF16)
    out = _dot_nt(q, _expand(sb, lanehead)) + o0
    return out, st * pc + _dot(sb, _expand(gm, lanehead)) + hm


def _rwkv_state_kernel(q_ref, o_ref, g_ref, h_ref, p_ref, out_ref, s_scr):
    @pl.when(pl.program_id(0) == 0)
    def _():
        s_scr[...] = jnp.zeros_like(s_scr)

    lanehead = lax.broadcasted_iota(jnp.int32, (1, GROUP), 1) >> 6
    for b in range(out_ref.shape[0]):
        for g in range(N_GROUPS):
            ls = slice(g * GROUP, (g + 1) * GROUP)
            out_ref[b, :, ls], s_scr[b, g] = _state_step(
                s_scr[b, g], q_ref[b, :, ls], o_ref[b, :, ls], g_ref[b, :, ls], h_ref[b, :, ls],
                p_ref[b, 0:1, ls], lanehead)


def _rwkv_state_reverse(q, o0, gm, hm, pc, n_ctx):
    B, TT, W = q.shape
    C = RWKV_CHUNK
    nch = TT // C
    ncc = n_ctx // C
    rev = lambda s: jnp.where(s < ncc, ncc - 1 - s, nch - 1 - (s - ncc))
    spec = lambda rows: pl.BlockSpec((B, rows, W), lambda s: (0, rev(s), 0))
    return pl.pallas_call(
        _rwkv_state_kernel,
        grid=(nch,),
        in_specs=[spec(C)] * 4 + [spec(8)],
        out_specs=spec(C),
        out_shape=jax.ShapeDtypeStruct((B, TT, W), F32),
        scratch_shapes=[pltpu.VMEM((B, N_GROUPS, HEAD_DIM, GROUP), F32)],
        compiler_params=_params("arbitrary"),
        name="rwkv_state",
    )(q, o0, gm, hm, pc)


def _merge_ffn_kernel(x_ref, mod_ref, ya_ref, yl_ref, of_ref, or_ref, gate_ref, lng_ref, lnb_ref, havg_ref, zg_ref,
                      proj_ref, wo_ref, g_ref, wgu_ref, wd_ref, out_ref):
    D = x_ref.shape[2]
    mod = mod_ref[0, 0]
    o = of_ref[0] + or_ref[0]
    mean = _dot_split(o, havg_ref[...])
    cen = o - mean
    var = _dot_split(cen * cen, havg_ref[...])
    yr = ((cen * lax.rsqrt(var + RWKV_LN_EPS)) * lng_ref[...] + lnb_ref[...]) * gate_ref[0]
    ys = (ya_ref[0], yl_ref[...], yr.astype(BF16))
    acc = None
    for b in range(3):
        term = _sigmoid(zg_ref[0, :, b * D:(b + 1) * D]) * _dot(ys[b], proj_ref[b])
        acc = term if acc is None else acc + term
    x = x_ref[0] + mod[0:1] * _dot(acc.astype(BF16), wo_ref[...])
    out_ref[0] = _ffn_half_step(x, mod[1:4], g_ref[...], wgu_ref, wd_ref)


def _merge_ffn(x, mod4, ya, yl, o_fwd, o_rev, gate, ln_g, ln_b, havg, zg, proj, w_o, g, w_gu, w_d, layer,
               n_ctx_blocks, latent_only):
    B, TT, D = x.shape
    F = w_d.shape[2]
    tm = TOKEN_BLOCK
    first = n_ctx_blocks if latent_only else 0
    n_out = TT // tm - first
    kind = lambda i: jnp.where(i + first < n_ctx_blocks, 0, 1)
    tok = lambda w: pl.BlockSpec((1, tm, w), lambda b, i: (b, i + first, 0))
    return pl.pallas_call(
        _merge_ffn_kernel,
        grid=(B, n_out),
        in_specs=[tok(D),
                  pl.BlockSpec((1, 1, 4, D), lambda b, i: (b, kind(i), 0, 0)),
                  tok(MIX),
                  pl.BlockSpec((tm, MIX), lambda b, i: (i + first, b)),
                  tok(MIX), tok(MIX), tok(MIX),
                  _resident((1, MIX)), _resident((1, MIX)), _resident((MIX, MIX)),
                  tok(3 * D),
                  _layer_slice((3, MIX, D), layer), _layer_slice((D, D), layer),
                  _resident((1, D)), _layer_slice((D, 2 * F), layer, 1), _layer_slice((F, D), layer, 1)],
        out_specs=pl.BlockSpec((1, tm, D), lambda b, i: (b, i, 0)),
        out_shape=jax.ShapeDtypeStruct((B, n_out * tm, D), F32),
        compiler_params=_params("parallel", "parallel"),
        name="merge_ffn",
    )(x, mod4, ya, yl, o_fwd, o_rev, gate, ln_g, ln_b, havg, zg, proj, w_o, g, w_gu, w_d)


def _rope_tables(n_ctx, n_tok):
    n_freq = HEAD_DIM // 4
    t = jnp.arange(n_tok)
    row = (t // GRID_W).astype(F32)
    col = (t % GRID_W).astype(F32)
    inv = ROPE_BASE ** (-jnp.arange(n_freq, dtype=F32) / n_freq)
    ang_r = row[:, None] * inv
    ang_c = col[:, None] * inv
    cos = jnp.concatenate([jnp.cos(ang_r)] * 2 + [jnp.cos(ang_c)] * 2, axis=-1)
    sin = jnp.concatenate([-jnp.sin(ang_r), jnp.sin(ang_r), -jnp.sin(ang_c), jnp.sin(ang_c)], axis=-1)
    cos = jnp.concatenate([jnp.ones((n_ctx, HEAD_DIM), F32), cos], axis=0)
    sin = jnp.concatenate([jnp.zeros((n_ctx, HEAD_DIM), F32), sin], axis=0)
    return jnp.tile(cos, (1, LANES // HEAD_DIM)), jnp.tile(sin, (1, LANES // HEAD_DIM))


def _block_diag(w):
    n, r, c = w.shape
    eye = jnp.eye(n, dtype=w.dtype)
    return (eye[:, None, :, None] * w[:, :, None, :]).reshape(n * r, n * c)


def _pad_rows(w, rank):
    z = jnp.zeros_like(w[0])
    return jnp.stack([jnp.concatenate([w[0], z], axis=0), jnp.concatenate([z, w[1]], axis=0)])


def kernel(x, c, ctx, c_ctx, ada_w, ada_b, norm_g, ffn_w_gu, ffn_w_d, w_in, attn_q_gain, attn_k_gain, attn_sink, lru_conv_w, lru_conv_b, lru_gate_w, lru_gate_b, lru_lambda, rwkv_mu, rwkv_w_up, rwkv_w0, rwkv_a_up, rwkv_a0, rwkv_g_up, rwkv_k_k, rwkv_k_a, rwkv_r_k, rwkv_ln_g, rwkv_ln_b, branch_proj, w_out):
    B, T, D = x.shape
    n_ctx = ctx.shape[1]
    L = ada_w.shape[0]
    TT = n_ctx + T
    assert n_ctx % TOKEN_BLOCK == 0 and T % TOKEN_BLOCK == 0 and T % GRID_W == 0
    ncb = n_ctx // TOKEN_BLOCK
    d_ff = ffn_w_d.shape[2]
    n_rwkv = rwkv_mu.shape[2]
    n_gate = 3 * D

    n_rows = -(-(B + 1) // 8) * 8
    c_rows = jnp.zeros((n_rows, D), F32).at[:B].set(c).at[B].set(c_ctx)
    ada = _ada_vectors(c_rows, ada_w, ada_b)
    mod_lat = ada[:, :B].reshape(L, B, 1, N_ADA, D)
    mod_ctx = jnp.broadcast_to(ada[:, B].reshape(L, 1, 1, N_ADA, D), (L, B, 1, N_ADA, D))
    mods = jnp.concatenate([mod_ctx, mod_lat], axis=2)

    cos_t, sin_t = _rope_tables(n_ctx, T)
    hsum = jnp.kron(jnp.eye(RWKV_HEADS, dtype=BF16), jnp.ones((HEAD_DIM, HEAD_DIM), BF16))
    havg = hsum / HEAD_DIM
    tile2 = lambda g: jnp.tile(g, LANES // HEAD_DIM).reshape(1, LANES)

    w_gu_b = ffn_w_gu.astype(BF16)
    w_d_b = ffn_w_d.astype(BF16)
    w_in_b = w_in.astype(BF16)
    proj_b = branch_proj.astype(BF16)
    w_out_b = w_out.astype(BF16)

    xs = x
    for l in range(L):
        xs = _ffn(xs, mods[l, :, :, 0:3], norm_g[l, 0].reshape(1, D), w_gu_b, w_d_b, l, ncb,
                  ctx=ctx if l == 0 else None)

        q, k, v, ux, ug, zr, zg = _inproj(
            xs, mods[l, :, :, 3:6], norm_g[l, 1].reshape(1, D), w_in_b, l,
            tile2(attn_q_gain[l]), tile2(attn_k_gain[l]), cos_t, sin_t, ncb, n_rwkv, n_gate)

        ya = _attention(q, k, v, attn_sink[l], n_ctx)

        ux3 = ux.reshape(TT, B, MIX)
        ug3 = ug.reshape(TT, B, MIX)
        lru_args = lambda d: (
            lru_conv_w[l], lru_conv_b[l].reshape(1, MIX),
            (0.5 * jnp.concatenate([_block_diag(lru_gate_w[l, d, 0]), _block_diag(lru_gate_w[l, d, 1])],
                                   axis=1)).astype(BF16),
            0.5 * lru_gate_b[l, d].reshape(1, 2 * MIX), lru_lambda[l, d].reshape(1, MIX))
        h_fwd = _lru_direction(ux3, *lru_args(0), n_ctx, False)
        yl = _lru_direction(ux3, *lru_args(1), n_ctx, True, h_fwd, ug3).reshape(TT, B * MIX)

        gate, o_fwd, *summaries = _rwkv_chunks(
            zr, rwkv_mu[l], rwkv_k_k[l].reshape(1, MIX), rwkv_k_a[l].reshape(1, MIX),
            _pad_rows(rwkv_w_up[l], rwkv_w_up.shape[2]).astype(BF16), rwkv_w0[l],
            _pad_rows(rwkv_a_up[l], rwkv_a_up.shape[2]).astype(BF16), rwkv_a0[l],
            rwkv_g_up[l].astype(BF16), hsum, rwkv_r_k[l].reshape(1, MIX), n_ctx)
        o_rev = _rwkv_state_reverse(*summaries, n_ctx)

        xs = _merge_ffn(xs, mods[l, :, :, 5:9], ya, yl, o_fwd, o_rev, gate,
                        rwkv_ln_g[l].reshape(1, MIX), rwkv_ln_b[l].reshape(1, MIX), havg, zg,
                        proj_b, w_out_b, norm_g[l, 2].reshape(1, D), w_gu_b, w_d_b, l, ncb,
                        latent_only=(l == L - 1))
    return xs
```

```python
import functools

import jax
import jax.numpy as jnp
from jax import lax
from jax.experimental import pallas as pl
from jax.experimental.pallas import tpu as pltpu

F32 = jnp.float32
BF16 = jnp.bfloat16

N_ADA = 9
NORM_EPS = 1e-6
GRID_W = 64
HEAD_DIM = 64
ATTN_HEADS = 8
ATTN_KV_HEADS = 2
ATTN_BLOCK = 128
ROPE_BASE = 10000.0
NEG_INF = -1e30
LRU_C = 8.0
RWKV_HEADS = 8
RWKV_LN_EPS = 64e-5
MIX = 512
LANES = 128
TOKEN_BLOCK = 256
LRU_CHUNK = 128
RWKV_CHUNK = 64
VMEM_LIMIT = 56 * 1024 * 1024


def _params(*sem):
    return pltpu.CompilerParams(dimension_semantics=sem, vmem_limit_bytes=VMEM_LIMIT)


def _resident(shape):
    nd = len(shape)
    return pl.BlockSpec(shape, lambda *_: (0,) * nd, pipeline_mode=pl.Buffered(1))


def _layer_slice(shape, *index):
    nd = len(shape)
    return pl.BlockSpec((None,) * len(index) + tuple(shape), lambda *_: tuple(index) + (0,) * nd,
                        pipeline_mode=pl.Buffered(1))


def _dot(a, b):
    return jnp.dot(a, b, preferred_element_type=F32)


def _dot_split(x, w):
    x_hi = x.astype(BF16)
    x_lo = (x - x_hi.astype(F32)).astype(BF16)
    return _dot(x_hi, w) + _dot(x_lo, w)


def _dot_nt(a, b):
    return lax.dot_general(a, b, (((1,), (1,)), ((), ())), preferred_element_type=F32)


def _dot_tn(a, b):
    return lax.dot_general(a, b, (((0,), (0,)), ((), ())), preferred_element_type=F32)


def _sigmoid(x):
    return 0.5 * jnp.tanh(0.5 * x) + 0.5


def _softplus(x):
    return jnp.maximum(x, 0.0) + jnp.log(1.0 + jnp.exp(-jnp.abs(x)))


def _modulated_norm(x, g, shift, scale):
    y = x * lax.rsqrt(jnp.mean(x * x, axis=-1, keepdims=True) + NORM_EPS)
    return (y * g) * (1.0 + scale) + shift


def _ada_kernel(s_ref, w_ref, b_ref, o_ref):
    s = s_ref[...]
    s = (s * _sigmoid(s)).astype(BF16)
    o_ref[0] = _dot(s, w_ref[0].astype(BF16)) + b_ref[0]


def _ada_vectors(c_rows, ada_w, ada_b):
    L, D, N = ada_w.shape
    R = c_rows.shape[0]
    tn = 1024
    return pl.pallas_call(
        _ada_kernel,
        grid=(L, N // tn),
        in_specs=[pl.BlockSpec((R, D), lambda l, j: (0, 0)),
                  pl.BlockSpec((1, D, tn), lambda l, j: (l, 0, j)),
                  pl.BlockSpec((1, 1, tn), lambda l, j: (l, 0, j))],
        out_specs=pl.BlockSpec((1, R, tn), lambda l, j: (l, 0, j)),
        out_shape=jax.ShapeDtypeStruct((L, R, N), F32),
        compiler_params=_params("parallel", "parallel"),
        name="ada_vectors",
    )(c_rows, ada_w, ada_b.reshape(L, 1, N))


def _ffn_half_step(x, mod, g, wgu_ref, wd_ref):
    d_ff = wd_ref.shape[0]
    h = _modulated_norm(x, g, mod[0:1], mod[1:2]).astype(BF16)
    gt = _dot(h, wgu_ref[:, 0:d_ff])
    up = _dot(h, wgu_ref[:, d_ff:2 * d_ff])
    act = (gt * _sigmoid(gt) * up).astype(BF16)
    return x + 0.5 * mod[2:3] * _dot(act, wd_ref[...])


def _ffn_kernel(*refs, n_ctx_blocks, split_input):
    if split_input:
        c_ref, x_ref, mod_ref, g_ref, wgu_ref, wd_ref, o_ref = refs
        x = jnp.where(pl.program_id(1) < n_ctx_blocks, c_ref[0], x_ref[0])
    else:
        x_ref, mod_ref, g_ref, wgu_ref, wd_ref, o_ref = refs
        x = x_ref[0]
    o_ref[0] = _ffn_half_step(x, mod_ref[0, 0], g_ref[...], wgu_ref, wd_ref)


def _ffn(x, mod3, g, w_gu, w_d, layer, n_ctx_blocks, ctx=None):
    B, _, D = x.shape
    F = w_d.shape[2]
    tm = TOKEN_BLOCK
    kind = lambda i: jnp.where(i < n_ctx_blocks, 0, 1)
    if ctx is None:
        n_out = x.shape[1] // tm
        x_specs = [pl.BlockSpec((1, tm, D), lambda b, i: (b, i, 0))]
        xs = (x,)
    else:
        n_out = n_ctx_blocks + x.shape[1] // tm
        x_specs = [pl.BlockSpec((1, tm, D), lambda b, i: (b, jnp.minimum(i, n_ctx_blocks - 1), 0)),
                   pl.BlockSpec((1, tm, D), lambda b, i: (b, jnp.maximum(i - n_ctx_blocks, 0), 0))]
        xs = (ctx, x)
    return pl.pallas_call(
        functools.partial(_ffn_kernel, n_ctx_blocks=n_ctx_blocks, split_input=ctx is not None),
        grid=(B, n_out),
        in_specs=x_specs + [pl.BlockSpec((1, 1, 3, D), lambda b, i: (b, kind(i), 0, 0)),
                            _resident((1, D)), _layer_slice((D, 2 * F), layer, 0), _layer_slice((F, D), layer, 0)],
        out_specs=pl.BlockSpec((1, tm, D), lambda b, i: (b, i, 0)),
        out_shape=jax.ShapeDtypeStruct((B, n_out * tm, D), F32),
        compiler_params=_params("parallel", "parallel"),
        name="ffn",
    )(*xs, mod3, g, w_gu, w_d)


def _head_rms_rope(xs, gain, cos, sin, lane):
    sq = xs * xs
    lo = lane < HEAD_DIM
    s_lo = jnp.sum(jnp.where(lo, sq, 0.0), axis=-1, keepdims=True)
    s_hi = jnp.sum(jnp.where(lo, 0.0, sq), axis=-1, keepdims=True)
    ms = jnp.where(lo, s_lo, s_hi) * (1.0 / HEAD_DIM)
    y = xs * lax.rsqrt(ms + NORM_EPS) * gain
    first = (lane % 32) < 16
    partner = jnp.where(first, pltpu.roll(y, LANES - 16, axis=1), pltpu.roll(y, 16, axis=1))
    return y * cos + partner * sin


def _inproj_kernel(x_ref, mod_ref, g_ref, w_ref, qg_ref, kg_ref, cos_ref, sin_ref,
                   q_ref, k_ref, v_ref, ux_ref, ug_ref, zr_ref, zg_ref):
    x = x_ref[0]
    mod = mod_ref[0, 0]
    h = _modulated_norm(x, g_ref[...], mod[0:1], mod[1:2]).astype(BF16)
    nq = ATTN_HEADS * HEAD_DIM
    nkv = ATTN_KV_HEADS * HEAD_DIM
    lane = lax.broadcasted_iota(jnp.int32, (1, LANES), 1)
    cos = cos_ref[...]
    sin = sin_ref[...]
    za = _dot(h, w_ref[:, 0:nq + 2 * nkv])
    for s in range(nq // LANES):
        qs = _head_rms_rope(za[:, s * LANES:(s + 1) * LANES], qg_ref[...], cos, sin, lane)
        q_ref[0, :, s * LANES:(s + 1) * LANES] = (qs * HEAD_DIM ** -0.5).astype(BF16)
    k_ref[0] = _head_rms_rope(za[:, nq:nq + nkv], kg_ref[...], cos, sin, lane).astype(BF16)
    v_ref[0] = za[:, nq + nkv:nq + 2 * nkv].astype(BF16)
    c0 = nq + 2 * nkv
    zl = _dot(h, w_ref[:, c0:c0 + 2 * MIX])
    ux_ref[...] = zl[:, 0:MIX]
    ug_ref[...] = zl[:, MIX:2 * MIX]
    c1 = c0 + 2 * MIX
    nr = zr_ref.shape[2]
    zr_ref[0] = _dot(h, w_ref[:, c1:c1 + nr])
    c2 = c1 + nr
    zg_ref[0] = _dot(h, w_ref[:, c2:c2 + zg_ref.shape[2]])


def _inproj(x, mod3, g, w_in, layer, q_gain, k_gain, cos_t, sin_t, n_ctx_blocks, n_rwkv, n_gate):
    B, TT, D = x.shape
    tm = TOKEN_BLOCK
    nq = ATTN_HEADS * HEAD_DIM
    nkv = ATTN_KV_HEADS * HEAD_DIM
    kind = lambda i: jnp.where(i < n_ctx_blocks, 0, 1)
    tok = lambda w: pl.BlockSpec((1, tm, w), lambda b, i: (b, i, 0))
    tmajor = pl.BlockSpec((tm, MIX), lambda b, i: (i, b))
    return pl.pallas_call(
        _inproj_kernel,
        grid=(B, TT // tm),
        in_specs=[tok(D),
                  pl.BlockSpec((1, 1, 3, D), lambda b, i: (b, kind(i), 0, 0)),
                  _resident((1, D)), _layer_slice(w_in.shape[1:], layer), _resident((1, LANES)), _resident((1, LANES)),
                  pl.BlockSpec((tm, LANES), lambda b, i: (i, 0)),
                  pl.BlockSpec((tm, LANES), lambda b, i: (i, 0))],
        out_specs=[tok(nq), tok(nkv), tok(nkv), tmajor, tmajor, tok(n_rwkv), tok(n_gate)],
        out_shape=[jax.ShapeDtypeStruct((B, TT, nq), BF16),
                   jax.ShapeDtypeStruct((B, TT, nkv), BF16),
                   jax.ShapeDtypeStruct((B, TT, nkv), BF16),
                   jax.ShapeDtypeStruct((TT, B * MIX), F32),
                   jax.ShapeDtypeStruct((TT, B * MIX), F32),
                   jax.ShapeDtypeStruct((B, TT, n_rwkv), F32),
                   jax.ShapeDtypeStruct((B, TT, n_gate), F32)],
        compiler_params=_params("parallel", "parallel"),
        name="inproj",
    )(x, mod3, g, w_in, q_gain, k_gain, cos_t, sin_t)


def _attn_kernel(sink_ref, q_ref, kp_ref, kc_ref, kn_ref, kx_ref, vp_ref, vc_ref, vn_ref, vx_ref, o_ref,
                 *, n_ctx_blocks, n_blocks):
    i = pl.program_id(1)
    blk = ATTN_BLOCK
    n = HEAD_DIM
    rep = ATTN_HEADS // ATTN_KV_HEADS
    rows = rep * blk
    row = lax.broadcasted_iota(jnp.int32, (rows, 1), 0)

    def sink_column(g):
        col = jnp.full((rows, 1), sink_ref[g * rep], F32)
        for r in range(1, rep):
            col = jnp.where(row >= r * blk, sink_ref[g * rep + r], col)
        return col

    def attend(latent):
        if latent:
            kcat = jnp.concatenate([kp_ref[0], kc_ref[0], kn_ref[0], kx_ref[0]], axis=0)
            vcat = jnp.concatenate([vp_ref[0], vc_ref[0], vn_ref[0], vx_ref[0]], axis=0)
            far = 4 * blk
            off_prev = jnp.where(i >= n_ctx_blocks + 1, 0, far)
            off_next = jnp.where(i + 1 <= n_blocks - 1, 0, far)
            n_keys = kcat.shape[0]
            qi = lax.broadcasted_iota(jnp.int32, (rows, n_keys), 0) & (blk - 1)
            col = lax.broadcasted_iota(jnp.int32, (rows, n_keys), 1)
            mask = (((col >= blk) | (col >= qi + off_prev))
                    & ((col < 2 * blk) | (col >= 3 * blk) | (col - 2 * blk + off_next <= qi)))
        else:
            kcat = kx_ref[0]
            vcat = vx_ref[0]
            mask = None
        groups = range(ATTN_KV_HEADS)
        ss = []
        for g in groups:
            qg = jnp.concatenate([q_ref[0, :, (g * rep + r) * n:(g * rep + r + 1) * n] for r in range(rep)], axis=0)
            s = _dot_nt(qg, kcat[:, g * n:(g + 1) * n])
            ss.append(s if mask is None else jnp.where(mask, s, NEG_INF))
        sinks = [sink_column(g) for g in groups]
        ms = [jnp.maximum(jnp.max(s, axis=-1, keepdims=True), sk) for s, sk in zip(ss, sinks)]
        ps = [jnp.exp(s - m) for s, m in zip(ss, ms)]
        dens = [jnp.sum(p, axis=-1, keepdims=True) + jnp.exp(sk - m) for p, sk, m in zip(ps, sinks, ms)]
        outs = [_dot(p.astype(BF16), vcat[:, g * n:(g + 1) * n]) / den for g, p, den in zip(groups, ps, dens)]
        for g in groups:
            for r in range(rep):
                h = g * rep + r
                o_ref[0, :, h * n:(h + 1) * n] = outs[g][r * blk:(r + 1) * blk].astype(BF16)

    @pl.when(i >= n_ctx_blocks)
    def _():
        attend(True)

    @pl.when(i < n_ctx_blocks)
    def _():
        attend(False)


def _attention(q, k, v, sink, n_ctx):
    B, TT, nq = q.shape
    nkv = k.shape[2]
    blk = ATTN_BLOCK
    nb = TT // blk
    ncb = n_ctx // blk
    kv = lambda f: pl.BlockSpec((1, blk, nkv), lambda b, i: (b, f(i), 0))
    prev = kv(lambda i: jnp.maximum(i - 1, 0))
    cur = kv(lambda i: i)
    nxt = kv(lambda i: jnp.minimum(i + 1, nb - 1))
    cx = pl.BlockSpec((1, n_ctx, nkv), lambda b, i: (b, 0, 0))
    return pl.pallas_call(
        functools.partial(_attn_kernel, n_ctx_blocks=ncb, n_blocks=nb),
        grid=(B, nb),
        in_specs=[pl.BlockSpec(memory_space=pltpu.SMEM),
                  pl.BlockSpec((1, blk, nq), lambda b, i: (b, i, 0)),
                  prev, cur, nxt, cx, prev, cur, nxt, cx],
        out_specs=pl.BlockSpec((1, blk, nq), lambda b, i: (b, i, 0)),
        out_shape=jax.ShapeDtypeStruct((B, TT, nq), BF16),
        compiler_params=_params("parallel", "parallel"),
        name="attention",
    )(sink, q, k, k, k, k, v, v, v, v)


def _lru_chunk_index(s, n_ctx_chunks, n_chunks, reverse):
    if not reverse:
        return s
    return jnp.where(s < n_ctx_chunks, n_ctx_chunks - 1 - s, n_chunks - 1 - (s - n_ctx_chunks))


def _lru_kernel(up_ref, u_ref, un_ref, cw_ref, cb_ref, gw_ref, gb_ref, lam_ref, *rest,
                n_ctx_chunks, n_chunks, reverse):
    if reverse:
        hf_ref, ug_ref, y_ref, a_scr, b_scr, h_scr = rest
    else:
        h_out_ref, a_scr, b_scr, h_scr = rest
    s = pl.program_id(0)
    c = _lru_chunk_index(s, n_ctx_chunks, n_chunks, reverse)
    tc, nb, w = u_ref.shape

    @pl.when(s == 0)
    def _():
        h_scr[...] = jnp.zeros_like(h_scr)

    has_prev = (c != 0) & (c != n_ctx_chunks)
    has_next = (c != n_ctx_chunks - 1) & (c != n_chunks - 1)
    u = u_ref[...]
    ext = jnp.concatenate([jnp.where(has_prev, up_ref[...], 0.0), u, jnp.where(has_next, un_ref[...], 0.0)], axis=0)
    cw = cw_ref[...]
    xc = cb_ref[...] + cw[0:1] * ext[0:tc] + cw[1:2] * ext[1:tc + 1] + cw[2:3] * ext[2:tc + 2] + cw[3:4] * ext[3:tc + 3]

    half_z = _dot(xc.reshape(tc * nb, w).astype(BF16), gw_ref[...]).reshape(tc, nb, 2 * w) + gb_ref[...]
    t_r = jnp.tanh(half_z[:, :, 0:w])
    t_i = jnp.tanh(half_z[:, :, w:2 * w])
    half_c = (-0.5 * LRU_C) * _softplus(-lam_ref[...])
    log_a = half_c * t_r + half_c
    a = jnp.exp(log_a)
    a_scr[...] = a
    y = -jnp.tanh(log_a) * (a * a + 1.0)
    root = jnp.where(y > 0.0, y * lax.rsqrt(y), 0.0)
    half_x = 0.5 * xc
    b_scr[...] = root * (t_i * half_x + half_x)

    def step(t, h):
        tt = tc - 1 - t if reverse else t
        h = a_scr[tt] * h + b_scr[tt]
        b_scr[tt] = h
        return h

    h_scr[...] = lax.fori_loop(0, tc, step, h_scr[...], unroll=8)

    if reverse:
        ug = ug_ref[...]
        gelu = 0.5 * ug * (1.0 + jnp.tanh(0.7978845608028654 * (ug + 0.044715 * ug * ug * ug)))
        y_ref[...] = ((hf_ref[...] + b_scr[...]) * gelu).astype(BF16)
    else:
        h_out_ref[...] = b_scr[...]


def _lru_direction(ux3, conv_w, conv_b, gate_w, gate_b, lam, n_ctx, reverse, h_fwd=None, ug3=None):
    TT, B, W = ux3.shape
    tc = LRU_CHUNK
    nch = TT // tc
    ncc = n_ctx // tc
    cidx = lambda s: _lru_chunk_index(s, ncc, nch, reverse)
    chunk = pl.BlockSpec((tc, B, W), lambda s: (cidx(s), 0, 0))
    in_specs = [pl.BlockSpec((2, B, W), lambda s: (jnp.maximum(cidx(s) * (tc // 2) - 1, 0), 0, 0)),
                chunk,
                pl.BlockSpec((1, B, W), lambda s: (jnp.minimum((cidx(s) + 1) * tc, TT - 1), 0, 0)),
                _resident((4, W)), _resident((1, W)), _resident((W, 2 * W)), _resident((1, 2 * W)),
                _resident((1, W))]
    args = [ux3, ux3, ux3, conv_w, conv_b, gate_w, gate_b, lam]
    if reverse:
        in_specs += [chunk, chunk]
        args += [h_fwd, ug3]
        out_dtype = BF16
    else:
        out_dtype = F32
    return pl.pallas_call(
        functools.partial(_lru_kernel, n_ctx_chunks=ncc, n_chunks=nch, reverse=reverse),
        grid=(nch,),
        in_specs=in_specs,
        out_specs=chunk,
        out_shape=jax.ShapeDtypeStruct((TT, B, W), out_dtype),
        scratch_shapes=[pltpu.VMEM((tc, B, W), F32), pltpu.VMEM((tc, B, W), F32), pltpu.VMEM((B, W), F32)],
        compiler_params=_params("arbitrary"),
        name="lru_rev" if reverse else "lru_fwd",
    )(*args)


GROUP_HEADS = LANES // HEAD_DIM
GROUP = GROUP_HEADS * HEAD_DIM
N_GROUPS = RWKV_HEADS // GROUP_HEADS


def _expand(y, lanehead):
    return jnp.concatenate([jnp.where(lanehead == h, y, jnp.zeros_like(y)) for h in range(GROUP_HEADS)], axis=0)


def _compact(z, lanehead):
    n = HEAD_DIM
    out = jnp.where(lanehead == 0, z[0:n], 0.0)
    for h in range(1, GROUP_HEADS):
        out = out + jnp.where(lanehead == h, z[h * n:(h + 1) * n], 0.0)
    return out


def _chunk_algebra(d, r, v, kn, lw, kd, a, rk, outs=None, state=None, side=()):
    side = list(side)

    def run_side():
        if side:
            side.pop(0)()

    C = RWKV_CHUNK
    tm = r.shape[0]
    sgn = 1 - 2 * d
    ri = lax.broadcasted_iota(jnp.int32, (C, C), 0)
    ci = lax.broadcasted_iota(jnp.int32, (C, C), 1)
    tri = ((ri - ci) * sgn >= 0).astype(BF16)
    t = lax.broadcasted_iota(jnp.int32, (C, GROUP), 0)
    lane = lax.broadcasted_iota(jnp.int32, (C, GROUP), 1)
    i = lane & (C - 1)
    eye = t == i
    same16 = (t >> 4) == (i >> 4)
    same32 = (t >> 5) == (i >> 5)
    lanehead = lax.broadcasted_iota(jnp.int32, (1, GROUP), 1) >> 6
    t2 = lax.broadcasted_iota(jnp.int32, (2 * C, GROUP), 0)
    i2 = lax.broadcasted_iota(jnp.int32, (2 * C, GROUP), 1) & (C - 1)
    mask2 = ((t2 & (C - 1)) - i2) * sgn > jnp.where(t2 < C, 0, -1)
    rowhead = lax.broadcasted_iota(jnp.int32, (GROUP, GROUP), 0) >> 6
    colhead = lax.broadcasted_iota(jnp.int32, (GROUP, GROUP), 1) >> 6
    head_ones = (rowhead == colhead).astype(BF16)

    def mm(x, y):
        return _dot(x.astype(BF16), _expand(y.astype(BF16), lanehead))

    def each(f, *lists):
        return [f(*xs) for xs in zip(*lists)]

    n_ch = tm // C
    units = [(slice(ch * C, (ch + 1) * C), slice(g * GROUP, (g + 1) * GROUP), ch)
             for ch in range(n_ch) for g in range(N_GROUPS)]
    l_in_all, l_tot_all = [], []
    for ch in range(n_ch):
        lw_c = lw[ch * C:(ch + 1) * C]
        lw_hi = lw_c.astype(BF16)
        rest = lw_c - lw_hi.astype(F32)
        lw_mid = rest.astype(BF16)
        lw_lo = (rest - lw_mid.astype(F32)).astype(BF16)
        l_in_all.append(_dot(tri, lw_hi) + _dot(tri, lw_mid) + _dot(tri, lw_lo))
        l_tot_all.append(jnp.sum(lw_c, axis=0, keepdims=True))
    p_tot_all = [jnp.exp(l) for l in l_tot_all]

    a_bar, r_bar, lhs, b_inv, k_inv, b_end, k_end, vs = [], [], [], [], [], [], [], []
    for rows, ls, ch in units:
        kn_u = kn[rows, ls]
        k_u = kd[rows, ls]
        l_in = l_in_all[ch][:, ls]
        beta = kn_u * a[rows, ls]
        p_inv = jnp.exp(-l_in)
        p_end = p_tot_all[ch][:, ls] * p_inv
        a_bar.append(-kn_u * jnp.exp(l_in - lw[rows, ls]))
        r_bar.append(r[rows, ls] * jnp.exp(l_in))
        lhs.append(jnp.concatenate([a_bar[-1], r_bar[-1]], axis=0).astype(BF16))
        b_inv.append((beta * p_inv).astype(BF16))
        k_inv.append((k_u * p_inv).astype(BF16))
        b_end.append((beta * p_end).astype(BF16))
        k_end.append((k_u * p_end).astype(BF16))
        vs.append(v[rows, ls])

    mb = each(lambda l, y: jnp.where(mask2, _dot_nt(l, _expand(y, lanehead)), 0.0), lhs, b_inv)
    mk = each(lambda l, y: jnp.where(mask2, _dot_nt(l, _expand(y, lanehead)), 0.0), lhs, k_inv)
    m_ab = [m[0:C] for m in mb]
    m_rb = [m[C:2 * C] for m in mb]
    run_side()
    d1 = [jnp.where(same16, m, 0.0) for m in m_ab]
    in32 = [jnp.where(same32, m, 0.0) for m in m_ab]
    d2 = each(mm, d1, d1)
    run_side()
    d4 = each(mm, d2, d2)
    tinv = [jnp.where(eye, 1.0, m) for m in d1]
    tinv = each(lambda t, p: t + mm(t, p), tinv, d2)
    run_side()
    d8 = each(mm, d4, d4)
    tinv = each(lambda t, p: t + mm(t, p), tinv, d4)
    run_side()
    mkv = each(mm, mk, vs)
    x1 = [m[0:C] for m in mkv]
    o_loc = [m[C:2 * C] for m in mkv]
    tinv = each(lambda t, p: t + mm(t, p), tinv, d8)
    run_side()
    te = each(lambda t, a, b: mm(t, a - b), tinv, in32, d1)
    run_side()
    tinv = each(lambda t, e: t + mm(e, t), tinv, te)
    run_side()
    te = each(lambda t, a, b: mm(t, a - b), tinv, m_ab, in32)
    run_side()
    tinv = each(lambda t, e: t + mm(e, t), tinv, te)
    run_side()
    rkr = _dot_split(jnp.concatenate([r[rows, ls] * kd[rows, ls] * rk[:, ls] for rows, ls, ch in units], axis=0),
                     head_ones)

    if state is not None:
        o_ref, s_scr = state

        def chunk_scan(ch):
            ns = [ch * N_GROUPS + g for g in range(N_GROUPS)]
            cell = {}

            def from_state():
                cell["st"] = [s_scr[g] for g in range(N_GROUPS)]
                cell["fs"] = [_dot_nt(lhs[n], _expand(st.astype(BF16), lanehead)) for n, st in zip(ns, cell["st"])]

            def solve():
                cell["u"] = [mm(tinv[n], fs[0:C] + x1[n]) for n, fs in zip(ns, cell["fs"])]

            def emit():
                for g, n in enumerate(ns):
                    rows, ls, _ = units[n]
                    u = cell["u"][g]
                    o_ref[rows, ls] = (cell["fs"][g][C:2 * C] + o_loc[n] + mm(m_rb[n], u)
                                       + rkr[n * C:(n + 1) * C] * vs[n])
                    zh = _dot_tn(jnp.concatenate([vs[n], u], axis=0).astype(BF16),
                                 jnp.concatenate([k_end[n], b_end[n]], axis=0))
                    s_scr[g] = cell["st"][g] * p_tot_all[ch][:, ls] + _compact(zh, lanehead)

            return [from_state, solve, emit]

        while side:
            run_side()
        return [step for ch in range(n_ch) for step in chunk_scan(ch)]

    q_out, o_out, g_out, h_out, pc_out = outs
    for ch in range(n_ch):
        pc_out[ch * 8:(ch + 1) * 8, :] = jnp.broadcast_to(p_tot_all[ch], (8, lw.shape[1]))
    w_hat = each(mm, tinv, a_bar)
    run_side()
    u_hat = each(mm, tinv, x1)
    run_side()
    for (rows, ls, ch), rb, w, rbar in zip(units, m_rb, w_hat, r_bar):
        q_out[rows, ls] = (rbar + mm(rb, w)).astype(BF16)
    run_side()
    for n, ((rows, ls, ch), rb, u, o, v_u) in enumerate(zip(units, m_rb, u_hat, o_loc, vs)):
        o_out[rows, ls] = o + mm(rb, u) + rkr[n * C:(n + 1) * C] * v_u
    run_side()
    for (rows, ls, ch), w, be in zip(units, w_hat, b_end):
        g_out[rows, ls] = _compact(_dot_tn(w.astype(BF16), be), lanehead).astype(BF16)
    run_side()
    for (rows, ls, ch), v_u, u, ke, be in zip(units, vs, u_hat, k_end, b_end):
        zh = _dot_tn(jnp.concatenate([v_u, u], axis=0).astype(BF16), jnp.concatenate([ke, be], axis=0))
        h_out[rows, ls] = _compact(zh, lanehead)
    while side:
        run_side()
    return []


def _rwkv_chunk_kernel(zp_ref, z_ref, zn_ref, mu_ref, kk_ref, ka_ref, wup_ref, w0_ref, aup_ref, a0_ref, gup_ref,
                       hsum_ref, rk_ref, gate_ref, ofwd_ref, q_out, o_out, g_out, h_out, pc_out, s_scr,
                       *, n_ctx_blocks, n_blocks):
    i = pl.program_id(1)

    @pl.when(i == 0)
    def _():
        s_scr[...] = jnp.zeros_like(s_scr)

    has_prev = (i != 0) & (i != n_ctx_blocks)
    has_next = (i != n_ctx_blocks - 1) & (i != n_blocks - 1)
    z = z_ref[0]
    tm = z.shape[0]
    row = lax.broadcasted_iota(jnp.int32, (tm, 1), 0)
    z_first = jnp.where(has_prev, zp_ref[0, 7:8, :], 0.0)
    z_last = jnp.where(has_next, zn_ref[0, 0:1, :], 0.0)
    z_prev = jnp.where(row == 0, z_first, pltpu.roll(z, 1, axis=0))
    z_next = jnp.where(row == tm - 1, z_last, pltpu.roll(z, tm - 1, axis=0))
    mu = mu_ref[...]
    zs = (1.0 - mu[0:1] - mu[1:2]) * z + mu[0:1] * z_prev + mu[1:2] * z_next
    w = MIX
    r = zs[:, 0:w]
    k = zs[:, w:2 * w]
    v = zs[:, 2 * w:3 * w]
    wd = jnp.tanh(zs[:, 3 * w:3 * w + LANES]).astype(BF16)
    ad = zs[:, 3 * w + LANES:3 * w + 2 * LANES].astype(BF16)
    gd = _sigmoid(zs[:, 3 * w + 2 * LANES:3 * w + 3 * LANES]).astype(BF16)
    kk = k * kk_ref[...]
    norm = jnp.sqrt(_dot_split(kk * kk, hsum_ref[...]))
    kn = kk / jnp.maximum(norm, 1e-12)
    gate_ref[0] = _dot(gd, gup_ref[...])
    def direction_inputs(d):
        logw = w0_ref[d:d + 1] + _dot(wd, wup_ref[d])
        lw = -jnp.exp(-_softplus(-logw) - 0.5)
        a = _sigmoid(a0_ref[d:d + 1] + _dot(ad, aup_ref[d]))
        return lw, k * (1.0 + (a - 1.0) * ka_ref[...]), a

    lw, kd, a = direction_inputs(0)
    scan_steps = _chunk_algebra(0, r, v, kn, lw, kd, a, rk_ref[...], state=(ofwd_ref.at[0], s_scr))
    lw, kd, a = direction_inputs(1)
    _chunk_algebra(1, r, v, kn, lw, kd, a, rk_ref[...],
                   outs=(q_out.at[0], o_out.at[0], g_out.at[0], h_out.at[0], pc_out.at[0]), side=scan_steps)


def _rwkv_chunks(zr, mu, k_k, k_a, w_up, w0, a_up, a0, g_up, hsum, r_k, n_ctx):
    B, TT, NR = zr.shape
    tm = TOKEN_BLOCK
    C = RWKV_CHUNK
    assert C == HEAD_DIM and tm % C == 0
    nb = TT // tm
    ncb = n_ctx // tm
    rows = tm // 8
    n_pc = 8 * (tm // C)
    tok = pl.BlockSpec((1, tm, MIX), lambda b, i: (b, i, 0))
    shape = lambda dt: jax.ShapeDtypeStruct((B, TT, MIX), dt)
    return pl.pallas_call(
        functools.partial(_rwkv_chunk_kernel, n_ctx_blocks=ncb, n_blocks=nb),
        grid=(B, nb),
        in_specs=[pl.BlockSpec((1, 8, NR), lambda b, i: (b, jnp.maximum(i * rows - 1, 0), 0)),
                  pl.BlockSpec((1, tm, NR), lambda b, i: (b, i, 0)),
                  pl.BlockSpec((1, 8, NR), lambda b, i: (b, jnp.minimum((i + 1) * rows, TT // 8 - 1), 0)),
                  _resident((2, NR)), _resident((1, MIX)), _resident((1, MIX)),
                  _resident((2, LANES, MIX)), _resident((2, MIX)), _resident((2, LANES, MIX)), _resident((2, MIX)),
                  _resident((LANES, MIX)), _resident((MIX, MIX)), _resident((1, MIX))],
        out_specs=[tok, tok, tok, tok, tok, tok, pl.BlockSpec((1, n_pc, MIX), lambda b, i: (b, i, 0))],
        out_shape=[shape(F32), shape(F32), shape(BF16), shape(F32), shape(BF16), shape(F32),
                   jax.ShapeDtypeStruct((B, 8 * (TT // C), MIX), F32)],
        scratch_shapes=[pltpu.VMEM((N_GROUPS, HEAD_DIM, GROUP), F32)],
        compiler_params=_params("parallel", "arbitrary"),
        name="rwkv_chunks",
    )(zr, zr, zr, mu, k_k, k_a, w_up, w0, a_up, a0, g_up, hsum, r_k)


def _state_step(st, q, o0, gm, hm, pc, lanehead):
    sb = st.astype(BF16)
    out = _dot_nt(q, _expand(sb, lanehead)) + o0
    return out, st * pc + _dot(sb, _expand(gm, lanehead)) + hm


def _rwkv_state_kernel(q_ref, o_ref, g_ref, h_ref, p_ref, out_ref, s_scr):
    @pl.when(pl.program_id(0) == 0)
    def _():
        s_scr[...] = jnp.zeros_like(s_scr)

    lanehead = lax.broadcasted_iota(jnp.int32, (1, GROUP), 1) >> 6
    for b in range(out_ref.shape[0]):
        for g in range(N_GROUPS):
            ls = slice(g * GROUP, (g + 1) * GROUP)
            out_ref[b, :, ls], s_scr[b, g] = _state_step(
                s_scr[b, g], q_ref[b, :, ls], o_ref[b, :, ls], g_ref[b, :, ls], h_ref[b, :, ls],
                p_ref[b, 0:1, ls], lanehead)


def _rwkv_state_reverse(q, o0, gm, hm, pc, n_ctx):
    B, TT, W = q.shape
    C = RWKV_CHUNK
    nch = TT // C
    ncc = n_ctx // C
    rev = lambda s: jnp.where(s < ncc, ncc - 1 - s, nch - 1 - (s - ncc))
    spec = lambda rows: pl.BlockSpec((B, rows, W), lambda s: (0, rev(s), 0))
    return pl.pallas_call(
        _rwkv_state_kernel,
        grid=(nch,),
        in_specs=[spec(C)] * 4 + [spec(8)],
        out_specs=spec(C),
        out_shape=jax.ShapeDtypeStruct((B, TT, W), F32),
        scratch_shapes=[pltpu.VMEM((B, N_GROUPS, HEAD_DIM, GROUP), F32)],
        compiler_params=_params("arbitrary"),
        name="rwkv_state",
    )(q, o0, gm, hm, pc)


def _merge_ffn_kernel(x_ref, mod_ref, ya_ref, yl_ref, of_ref, or_ref, gate_ref, lng_ref, lnb_ref, havg_ref, zg_ref,
                      proj_ref, wo_ref, g_ref, wgu_ref, wd_ref, out_ref):
    D = x_ref.shape[2]
    mod = mod_ref[0, 0]
    o = of_ref[0] + or_ref[0]
    mean = _dot_split(o, havg_ref[...])
    cen = o - mean
    var = _dot_split(cen * cen, havg_ref[...])
    yr = ((cen * lax.rsqrt(var + RWKV_LN_EPS)) * lng_ref[...] + lnb_ref[...]) * gate_ref[0]
    ys = (ya_ref[0], yl_ref[...], yr.astype(BF16))
    acc = None
    for b in range(3):
        term = _sigmoid(zg_ref[0, :, b * D:(b + 1) * D]) * _dot(ys[b], proj_ref[b])
        acc = term if acc is None else acc + term
    x = x_ref[0] + mod[0:1] * _dot(acc.astype(BF16), wo_ref[...])
    out_ref[0] = _ffn_half_step(x, mod[1:4], g_ref[...], wgu_ref, wd_ref)


def _merge_ffn(x, mod4, ya, yl, o_fwd, o_rev, gate, ln_g, ln_b, havg, zg, proj, w_o, g, w_gu, w_d, layer,
               n_ctx_blocks, latent_only):
    B, TT, D = x.shape
    F = w_d.shape[2]
    tm = TOKEN_BLOCK
    first = n_ctx_blocks if latent_only else 0
    n_out = TT // tm - first
    kind = lambda i: jnp.where(i + first < n_ctx_blocks, 0, 1)
    tok = lambda w: pl.BlockSpec((1, tm, w), lambda b, i: (b, i + first, 0))
    return pl.pallas_call(
        _merge_ffn_kernel,
        grid=(B, n_out),
        in_specs=[tok(D),
                  pl.BlockSpec((1, 1, 4, D), lambda b, i: (b, kind(i), 0, 0)),
                  tok(MIX),
                  pl.BlockSpec((tm, MIX), lambda b, i: (i + first, b)),
                  tok(MIX), tok(MIX), tok(MIX),
                  _resident((1, MIX)), _resident((1, MIX)), _resident((MIX, MIX)),
                  tok(3 * D),
                  _layer_slice((3, MIX, D), layer), _layer_slice((D, D), layer),
                  _resident((1, D)), _layer_slice((D, 2 * F), layer, 1), _layer_slice((F, D), layer, 1)],
        out_specs=pl.BlockSpec((1, tm, D), lambda b, i: (b, i, 0)),
        out_shape=jax.ShapeDtypeStruct((B, n_out * tm, D), F32),
        compiler_params=_params("parallel", "parallel"),
        name="merge_ffn",
    )(x, mod4, ya, yl, o_fwd, o_rev, gate, ln_g, ln_b, havg, zg, proj, w_o, g, w_gu, w_d)


def _rope_tables(n_ctx, n_tok):
    n_freq = HEAD_DIM // 4
    t = jnp.arange(n_tok)
    row = (t // GRID_W).astype(F32)
    col = (t % GRID_W).astype(F32)
    inv = ROPE_BASE ** (-jnp.arange(n_freq, dtype=F32) / n_freq)
    ang_r = row[:, None] * inv
    ang_c = col[:, None] * inv
    cos = jnp.concatenate([jnp.cos(ang_r)] * 2 + [jnp.cos(ang_c)] * 2, axis=-1)
    sin = jnp.concatenate([-jnp.sin(ang_r), jnp.sin(ang_r), -jnp.sin(ang_c), jnp.sin(ang_c)], axis=-1)
    cos = jnp.concatenate([jnp.ones((n_ctx, HEAD_DIM), F32), cos], axis=0)
    sin = jnp.concatenate([jnp.zeros((n_ctx, HEAD_DIM), F32), sin], axis=0)
    return jnp.tile(cos, (1, LANES // HEAD_DIM)), jnp.tile(sin, (1, LANES // HEAD_DIM))


def _block_diag(w):
    n, r, c = w.shape
    eye = jnp.eye(n, dtype=w.dtype)
    return (eye[:, None, :, None] * w[:, :, None, :]).reshape(n * r, n * c)


def _pad_rows(w, rank):
    z = jnp.zeros_like(w[0])
    return jnp.stack([jnp.concatenate([w[0], z], axis=0), jnp.concatenate([z, w[1]], axis=0)])


def kernel(x, c, ctx, c_ctx, ada_w, ada_b, norm_g, ffn_w_gu, ffn_w_d, w_in, attn_q_gain, attn_k_gain, attn_sink, lru_conv_w, lru_conv_b, lru_gate_w, lru_gate_b, lru_lambda, rwkv_mu, rwkv_w_up, rwkv_w0, rwkv_a_up, rwkv_a0, rwkv_g_up, rwkv_k_k, rwkv_k_a, rwkv_r_k, rwkv_ln_g, rwkv_ln_b, branch_proj, w_out):
    B, T, D = x.shape
    n_ctx = ctx.shape[1]
    L = ada_w.shape[0]
    TT = n_ctx + T
    assert n_ctx % TOKEN_BLOCK == 0 and T % TOKEN_BLOCK == 0 and T % GRID_W == 0
    ncb = n_ctx // TOKEN_BLOCK
    n_rwkv = rwkv_mu.shape[2]
    n_gate = 3 * D

    n_rows = -(-(B + 1) // 8) * 8
    c_rows = jnp.zeros((n_rows, D), F32).at[:B].set(c).at[B].set(c_ctx)
    ada = _ada_vectors(c_rows, ada_w, ada_b)
    mod_lat = ada[:, :B].reshape(L, B, 1, N_ADA, D)
    mod_ctx = jnp.broadcast_to(ada[:, B].reshape(L, 1, 1, N_ADA, D), (L, B, 1, N_ADA, D))
    mods = jnp.concatenate([mod_ctx, mod_lat], axis=2)

    cos_t, sin_t = _rope_tables(n_ctx, T)
    hsum = jnp.kron(jnp.eye(RWKV_HEADS, dtype=BF16), jnp.ones((HEAD_DIM, HEAD_DIM), BF16))
    havg = hsum / HEAD_DIM
    tile2 = lambda g: jnp.tile(g, LANES // HEAD_DIM).reshape(1, LANES)

    w_gu_b = ffn_w_gu.astype(BF16)
    w_d_b = ffn_w_d.astype(BF16)
    w_in_b = w_in.astype(BF16)
    proj_b = branch_proj.astype(BF16)
    w_out_b = w_out.astype(BF16)

    xs = x
    for l in range(L):
        xs = _ffn(xs, mods[l, :, :, 0:3], norm_g[l, 0].reshape(1, D), w_gu_b, w_d_b, l, ncb,
                  ctx=ctx if l == 0 else None)

        q, k, v, ux, ug, zr, zg = _inproj(
            xs, mods[l, :, :, 3:6], norm_g[l, 1].reshape(1, D), w_in_b, l,
            tile2(attn_q_gain[l]), tile2(attn_k_gain[l]), cos_t, sin_t, ncb, n_rwkv, n_gate)

        ya = _attention(q, k, v, attn_sink[l], n_ctx)

        ux3 = ux.reshape(TT, B, MIX)
        ug3 = ug.reshape(TT, B, MIX)
        lru_args = lambda d: (
            lru_conv_w[l], lru_conv_b[l].reshape(1, MIX),
            (0.5 * jnp.concatenate([_block_diag(lru_gate_w[l, d, 0]), _block_diag(lru_gate_w[l, d, 1])],
                                   axis=1)).astype(BF16),
            0.5 * lru_gate_b[l, d].reshape(1, 2 * MIX), lru_lambda[l, d].reshape(1, MIX))
        h_fwd = _lru_direction(ux3, *lru_args(0), n_ctx, False)
        yl = _lru_direction(ux3, *lru_args(1), n_ctx, True, h_fwd, ug3).reshape(TT, B * MIX)

        gate, o_fwd, *summaries = _rwkv_chunks(
            zr, rwkv_mu[l], rwkv_k_k[l].reshape(1, MIX), rwkv_k_a[l].reshape(1, MIX),
            _pad_rows(rwkv_w_up[l], rwkv_w_up.shape[2]).astype(BF16), rwkv_w0[l],
            _pad_rows(rwkv_a_up[l], rwkv_a_up.shape[2]).astype(BF16), rwkv_a0[l],
            rwkv_g_up[l].astype(BF16), hsum, rwkv_r_k[l].reshape(1, MIX), n_ctx)
        o_rev = _rwkv_state_reverse(*summaries, n_ctx)

        xs = _merge_ffn(xs, mods[l, :, :, 5:9], ya, yl, o_fwd, o_rev, gate,
                        rwkv_ln_g[l].reshape(1, MIX), rwkv_ln_b[l].reshape(1, MIX), havg, zg,
                        proj_b, w_out_b, norm_g[l, 2].reshape(1, D), w_gu_b, w_d_b, l, ncb,
                        latent_only=(l == L - 1))
    return xs
```
